```python
import jax, jax.numpy as jnp
from jax import lax
import numpy as np

D_MODEL = 2048
BATCH = 2
SEQ = 8192
DEPTH = 1

MEM_LEN = 256
NORM_EPS = 1e-6
NEG_INF = -1e30
Q_BLOCK = 128
ROPE_THETA = 500000.0
DIFF_HEADS = 8
DIFF_HEAD_DIM = 128
DIFF_V_DIM = 2 * DIFF_HEAD_DIM
DIL_HEADS = 8
DIL_HEAD_DIM = 128
DIL_CONFIGS = ((128, 1), (512, 4), (2048, 16))
DIL_PAD = 2048
MEM_HEADS = 4
MEM_HEAD_DIM = 256
N_BRANCH = 3
ROT_DIM = 128 // 4
DIFF_Q_COLS = DIFF_HEADS * 2 * DIFF_HEAD_DIM
DIFF_K_COLS = DIFF_HEADS * 2 * DIFF_HEAD_DIM
DIFF_V_COLS = DIFF_HEADS * DIFF_V_DIM
DIL_COLS = DIL_HEADS * DIL_HEAD_DIM
MEM_Q_COLS = MEM_HEADS * MEM_HEAD_DIM
GATE_COLS = N_BRANCH * D_MODEL
IN_SIZES = (DIFF_Q_COLS, DIFF_K_COLS, DIFF_V_COLS, DIL_COLS, DIL_COLS, DIL_COLS, MEM_Q_COLS, GATE_COLS)
IN_COLS = DIFF_Q_COLS + DIFF_K_COLS + DIFF_V_COLS + 3 * DIL_COLS + MEM_Q_COLS + GATE_COLS
N_EXPERTS = 32
TOP_K = 4
D_FF = 2048
SWIGLU_LIMIT = 7.0
SWIGLU_ALPHA = 1.702
EXPERT_BLOCK = 256

kernel_name = "hybrid_gated_diff_dilated_memory_moe"


def rms_norm(x, g, eps=NORM_EPS):
    xf = x.astype(jnp.float32)
    y = xf * lax.rsqrt(jnp.mean(xf * xf, axis=-1, keepdims=True) + eps)
    return (y * g.astype(jnp.float32)).astype(x.dtype)


def split_cols(x, sizes):
    out, start = [], 0
    for s in sizes:
        out.append(x[..., start:start + s])
        start += s
    return out


def rope_tables(positions):
    inv_freq = ROPE_THETA ** (-jnp.arange(0, ROT_DIM, 2, dtype=jnp.float32) / ROT_DIM)
    ang = positions.astype(jnp.float32)[..., None] * inv_freq
    return jnp.cos(ang)[:, :, None, :], jnp.sin(ang)[:, :, None, :]


def partial_rope(x, cos, sin):
    half = ROT_DIM // 2
    xr = x[..., :ROT_DIM].astype(jnp.float32)
    x1, x2 = xr[..., :half], xr[..., half:]
    rot = jnp.concatenate([x1 * cos - x2 * sin, x2 * cos + x1 * sin], axis=-1)
    return jnp.concatenate([rot.astype(x.dtype), x[..., ROT_DIM:]], axis=-1)


def diff_attention(q, k, v, lam):
    B, S = q.shape[0], q.shape[1]
    scale = DIFF_HEAD_DIM ** -0.5
    qc = jnp.transpose(q, (3, 0, 2, 1, 4))
    kc = jnp.transpose(k, (3, 0, 2, 1, 4))
    vh = jnp.transpose(v, (0, 2, 1, 3))
    key_pos = jnp.arange(S)

    def block(i):
        s0 = i * Q_BLOCK
        qb = lax.dynamic_slice_in_dim(qc, s0, Q_BLOCK, axis=3)
        s = jnp.einsum('cbhqd,cbhkd->cbhqk', qb, kc).astype(jnp.float32) * scale
        causal = (s0 + jnp.arange(Q_BLOCK))[:, None] >= key_pos[None, :]
        p = jax.nn.softmax(jnp.where(causal, s, NEG_INF), axis=-1)
        a = p[0] - lam * p[1]
        return jnp.einsum('bhqk,bhkd->bhqd', a.astype(v.dtype), vh)

    o = lax.map(block, jnp.arange(S // Q_BLOCK))
    return jnp.transpose(o, (1, 0, 3, 2, 4)).reshape(B, S, DIFF_HEADS, DIFF_V_DIM)


def dilated_attention(q, k, v):
    B, S = q.shape[0], q.shape[1]
    scale = DIL_HEAD_DIM ** -0.5
    qh = jnp.transpose(q, (0, 2, 1, 3))
    pad = ((0, 0), (0, 0), (DIL_PAD, 0), (0, 0))
    kp = jnp.pad(jnp.transpose(k, (0, 2, 1, 3)), pad)
    vp = jnp.pad(jnp.transpose(v, (0, 2, 1, 3)), pad)

    def block(i):
        s0 = i * Q_BLOCK
        qb = lax.dynamic_slice_in_dim(qh, s0, Q_BLOCK, axis=2)
        t = s0 + jnp.arange(Q_BLOCK)
        outs, lses = [], []
        for window, dil in DIL_CONFIGS:
            src = t[:, None] - dil * jnp.arange(window // dil + 1)[None, :]
            kg = kp[:, :, src + DIL_PAD]
            vg = vp[:, :, src + DIL_PAD]
            s = jnp.einsum('bhqd,bhqnd->bhqn', qb, kg).astype(jnp.float32) * scale
            s = jnp.where(src >= 0, s, NEG_INF)
            m = jnp.max(s, axis=-1, keepdims=True)
            e = jnp.exp(s - m)
            z = jnp.sum(e, axis=-1, keepdims=True)
            outs.append(jnp.einsum('bhqn,bhqnd->bhqd', (e / z).astype(v.dtype), vg))
            lses.append(m[..., 0] + jnp.log(z[..., 0]))
        wgt = jax.nn.softmax(jnp.stack(lses), axis=0)
        return jnp.einsum('gbhq,gbhqd->bhqd', wgt.astype(v.dtype), jnp.stack(outs))

    o = lax.map(block, jnp.arange(S // Q_BLOCK))
    return jnp.transpose(o, (1, 0, 3, 2, 4)).reshape(B, S, DIL_HEADS, DIL_HEAD_DIM)


def memory_attention(q, km, vm):
    s = jnp.einsum('bqhd,bkhd->bhqk', q, km).astype(jnp.float32) * (MEM_HEAD_DIM ** -0.5)
    p = jax.nn.softmax(s, axis=-1)
    return jnp.einsum('bhqk,bkhd->bqhd', p.astype(vm.dtype), vm)


def moe_ffn(h, w_router, b_router, w_gate_up, b_gate_up, w_down, b_down):
    T, D = h.shape
    logits = (h @ w_router + b_router).astype(jnp.float32)
    top_v, top_e = lax.top_k(logits, TOP_K)
    gates = jax.nn.softmax(top_v, axis=-1)
    A = T * TOP_K
    e_flat = top_e.reshape(A)
    tok_flat = jnp.repeat(jnp.arange(T, dtype=jnp.int32), TOP_K)
    w_flat = gates.reshape(A)
    order = jnp.argsort(e_flat)
    e_s, tok_s, w_s = e_flat[order], tok_flat[order], w_flat[order]
    counts = jnp.bincount(e_flat, length=N_EXPERTS)
    starts = jnp.cumsum(counts) - counts
    padded = (counts + EXPERT_BLOCK - 1) // EXPERT_BLOCK * EXPERT_BLOCK
    pend = jnp.cumsum(padded)
    pstart = pend - padded
    dest = pstart[e_s] + jnp.arange(A) - starts[e_s]
    n_blocks = (A + N_EXPERTS * EXPERT_BLOCK + EXPERT_BLOCK - 1) // EXPERT_BLOCK
    P = n_blocks * EXPERT_BLOCK
    slot_tok = jnp.full((P,), T, dtype=jnp.int32).at[dest].set(tok_s)
    slot_w = jnp.zeros((P,), jnp.float32).at[dest].set(w_s)
    block_e = jnp.clip(jnp.searchsorted(pend, jnp.arange(n_blocks) * EXPERT_BLOCK, side='right'),
                       0, N_EXPERTS - 1)
    h_pad = jnp.concatenate([h, jnp.zeros((1, D), h.dtype)], axis=0)
    xb = h_pad[slot_tok].reshape(n_blocks, EXPERT_BLOCK, D)

    def expert_block(args):
        xe, e = args
        gu = (xe @ w_gate_up[e] + b_gate_up[e]).astype(jnp.float32)
        gate, up = gu[:, :D_FF], gu[:, D_FF:]
        gate = jnp.minimum(gate, SWIGLU_LIMIT)
        up = jnp.clip(up, -SWIGLU_LIMIT, SWIGLU_LIMIT)
        act = (up + 1.0) * gate * jax.nn.sigmoid(SWIGLU_ALPHA * gate)
        return act.astype(xe.dtype) @ w_down[e] + b_down[e]

    yb = lax.map(expert_block, (xb, block_e)).reshape(P, D)
    contrib = (yb.astype(jnp.float32) * slot_w[:, None]).astype(h.dtype)
    return jnp.zeros((T + 1, D), h.dtype).at[slot_tok].add(contrib)[:T]


def hybrid_layer(x, mem, cos, sin, layer_idx, g_mix_norm, w_in, b_gate, diff_q_norm, diff_k_norm,
                 diff_lambda, diff_subln, dil_q_norm, dil_k_norm, g_mem_norm, w_mem_kv,
                 mem_q_norm, mem_k_norm, w_branch_diff, w_branch_dil, w_branch_mem, w_out,
                 g_ffn_norm, w_router, b_router, w_gate_up, b_gate_up, w_down, b_down):
    B, S, D = x.shape
    h = rms_norm(x, g_mix_norm)
    proj = h @ w_in
    dq, dk, dv, lq, lk, lv, mq, graw = split_cols(proj, IN_SIZES)

    dq = rms_norm(dq.reshape(B, S, DIFF_HEADS, 2, DIFF_HEAD_DIM), diff_q_norm)
    dk = rms_norm(dk.reshape(B, S, DIFF_HEADS, 2, DIFF_HEAD_DIM), diff_k_norm)
    dq = partial_rope(dq.reshape(B, S, 2 * DIFF_HEADS, DIFF_HEAD_DIM), cos, sin).reshape(B, S, DIFF_HEADS, 2, DIFF_HEAD_DIM)
    dk = partial_rope(dk.reshape(B, S, 2 * DIFF_HEADS, DIFF_HEAD_DIM), cos, sin).reshape(B, S, DIFF_HEADS, 2, DIFF_HEAD_DIM)
    lam_init = 0.8 - 0.6 * float(np.exp(-0.3 * layer_idx))
    lp = diff_lambda.astype(jnp.float32)
    lam = jnp.exp(jnp.sum(lp[0] * lp[1])) - jnp.exp(jnp.sum(lp[2] * lp[3])) + lam_init
    o_diff = diff_attention(dq, dk, dv.reshape(B, S, DIFF_HEADS, DIFF_V_DIM), lam)
    o_diff = rms_norm(o_diff, diff_subln) * (1.0 - lam_init)
    br_diff = o_diff.reshape(B, S, DIFF_V_COLS) @ w_branch_diff

    lq = partial_rope(rms_norm(lq.reshape(B, S, DIL_HEADS, DIL_HEAD_DIM), dil_q_norm), cos, sin)
    lk = partial_rope(rms_norm(lk.reshape(B, S, DIL_HEADS, DIL_HEAD_DIM), dil_k_norm), cos, sin)
    o_dil = dilated_attention(lq, lk, lv.reshape(B, S, DIL_HEADS, DIL_HEAD_DIM))
    br_dil = o_dil.reshape(B, S, DIL_COLS) @ w_branch_dil

    M = mem.shape[1]
    kvm = rms_norm(mem, g_mem_norm) @ w_mem_kv
    km = rms_norm(kvm[..., :MEM_Q_COLS].reshape(B, M, MEM_HEADS, MEM_HEAD_DIM), mem_k_norm)
    vm = kvm[..., MEM_Q_COLS:].reshape(B, M, MEM_HEADS, MEM_HEAD_DIM)
    mq = rms_norm(mq.reshape(B, S, MEM_HEADS, MEM_HEAD_DIM), mem_q_norm)
    o_mem = memory_attention(mq, km, vm)
    br_mem = o_mem.reshape(B, S, MEM_Q_COLS) @ w_branch_mem

    g = jax.nn.sigmoid((graw + b_gate).astype(jnp.float32)).reshape(B, S, N_BRANCH, D)
    merged = (g[:, :, 0] * br_diff.astype(jnp.float32) + g[:, :, 1] * br_dil.astype(jnp.float32)
              + g[:, :, 2] * br_mem.astype(jnp.float32)).astype(x.dtype)
    x = x + merged @ w_out

    h2 = rms_norm(x, g_ffn_norm).reshape(B * S, D)
    x = x + moe_ffn(h2, w_router, b_router, w_gate_up, b_gate_up, w_down, b_down).reshape(B, S, D)
    return x


def setup_inputs(seed: int = 0) -> dict:
    key = jax.random.key(seed)
    ks = jax.random.split(key, 32)
    f32 = jnp.float32
    L, D, E = DEPTH, D_MODEL, N_EXPERTS

    def w(k, shape, fan_in):
        return jax.random.normal(k, shape, f32) * fan_in ** -0.5

    def gain(k, shape):
        return 1.0 + 0.05 * jax.random.normal(k, shape, f32)

    def bias(k, shape, s=0.02):
        return s * jax.random.normal(k, shape, f32)

    return {
        "x": jax.random.normal(ks[0], (BATCH, SEQ, D), f32),
        "mem": jax.random.normal(ks[1], (BATCH, MEM_LEN, D), f32),
        "positions": jnp.broadcast_to(jnp.arange(SEQ, dtype=jnp.int32), (BATCH, SEQ)),
        "g_mix_norm": gain(ks[2], (L, D)),
        "w_in": w(ks[3], (L, D, IN_COLS), D),
        "b_gate": bias(ks[4], (L, GATE_COLS)),
        "diff_q_norm": gain(ks[5], (L, DIFF_HEAD_DIM)),
        "diff_k_norm": gain(ks[6], (L, DIFF_HEAD_DIM)),
        "diff_lambda": 0.1 * jax.random.normal(ks[7], (L, 4, DIFF_HEAD_DIM), f32),
        "diff_subln": gain(ks[8], (L, DIFF_V_DIM)),
        "dil_q_norm": gain(ks[9], (L, DIL_HEAD_DIM)),
        "dil_k_norm": gain(ks[10], (L, DIL_HEAD_DIM)),
        "g_mem_norm": gain(ks[11], (L, D)),
        "w_mem_kv": w(ks[12], (L, D, 2 * MEM_Q_COLS), D),
        "mem_q_norm": gain(ks[13], (L, MEM_HEAD_DIM)),
        "mem_k_norm": gain(ks[14], (L, MEM_HEAD_DIM)),
        "w_branch_diff": w(ks[15], (L, DIFF_V_COLS, D), DIFF_V_COLS),
        "w_branch_dil": w(ks[16], (L, DIL_COLS, D), DIL_COLS),
        "w_branch_mem": w(ks[17], (L, MEM_Q_COLS, D), MEM_Q_COLS),
        "w_out": w(ks[18], (L, D, D), D),
        "g_ffn_norm": gain(ks[19], (L, D)),
        "w_router": w(ks[20], (L, D, E), D),
        "b_router": bias(ks[21], (L, E), 0.01),
        "w_gate_up": w(ks[22], (L, E, D, 2 * D_FF), D),
        "b_gate_up": bias(ks[23], (L, E, 2 * D_FF)),
        "w_down": w(ks[24], (L, E, D_FF, D), D_FF),
        "b_down": bias(ks[25], (L, E, D)),
    }


def reference(x, mem, positions, g_mix_norm, w_in, b_gate, diff_q_norm, diff_k_norm, diff_lambda,
              diff_subln, dil_q_norm, dil_k_norm, g_mem_norm, w_mem_kv, mem_q_norm, mem_k_norm,
              w_branch_diff, w_branch_dil, w_branch_mem, w_out, g_ffn_norm, w_router, b_router,
              w_gate_up, b_gate_up, w_down, b_down):
    cos, sin = rope_tables(positions)
    for l in range(DEPTH):
        x = hybrid_layer(x, mem, cos, sin, l, g_mix_norm[l], w_in[l], b_gate[l], diff_q_norm[l],
                         diff_k_norm[l], diff_lambda[l], diff_subln[l], dil_q_norm[l], dil_k_norm[l],
                         g_mem_norm[l], w_mem_kv[l], mem_q_norm[l], mem_k_norm[l], w_branch_diff[l],
                         w_branch_dil[l], w_branch_mem[l], w_out[l], g_ffn_norm[l], w_router[l],
                         b_router[l], w_gate_up[l], b_gate_up[l], w_down[l], b_down[l])
    return x
```

```python
import functools

import numpy as np
import jax
import jax.numpy as jnp
from jax import lax
from jax.experimental import pallas as pl
from jax.experimental.pallas import tpu as pltpu

F32 = jnp.float32
BF16 = jnp.bfloat16

NORM_EPS = 1e-6
NEG_INF = -1e30
ROPE_THETA = 500000.0
ROT_DIM = 32
HEAD = 128
DIFF_HEADS = 8
DIFF_V_DIM = 256
DIL_HEADS = 8
DIL_CONFIGS = ((128, 1), (512, 4), (2048, 16))
DIL_PAD = 2048
MEM_HEADS = 4
MEM_HEAD_DIM = 256
N_EXPERTS = 32
TOP_K = 4
SWIGLU_LIMIT = 7.0
SWIGLU_ALPHA = 1.702

LANES = 128
VMEM_LIMIT = 56 * 1024 * 1024

PROJ_TN = 1024
ROW_TILE = 512
DIFF_TQ = 512
DIL_TQ = 256
EXPERT_ROWS = 512
FF_TILE = 512
DOWN_TN = 1024
COMBINE_ROWS = 256


def _params(sem, **kw):
    return pltpu.CompilerParams(dimension_semantics=sem, vmem_limit_bytes=VMEM_LIMIT, **kw)


def _rmsnorm_kernel(x_ref, g_ref, o_ref):
    x = x_ref[...]
    ms = jnp.mean(x * x, axis=-1, keepdims=True)
    o_ref[...] = (x * lax.rsqrt(ms + NORM_EPS) * g_ref[...]).astype(o_ref.dtype)


def _rmsnorm(x, g, out_dtype):
    t, d = x.shape
    tm = min(ROW_TILE, t)
    return pl.pallas_call(
        _rmsnorm_kernel,
        grid=(t // tm,),
        in_specs=[pl.BlockSpec((tm, d), lambda i: (i, 0)), pl.BlockSpec((1, d), lambda i: (0, 0))],
        out_specs=pl.BlockSpec((tm, d), lambda i: (i, 0)),
        out_shape=jax.ShapeDtypeStruct((t, d), out_dtype),
        compiler_params=_params(("arbitrary",)),
        name="rmsnorm",
    )(x, g.reshape(1, d))


def _chunk_norm(x, gain, width):
    outs = []
    for c in range(x.shape[1] // width):
        xc = x[:, c * width:(c + 1) * width]
        ms = jnp.mean(xc * xc, axis=-1, keepdims=True)
        outs.append(xc * lax.rsqrt(ms + NORM_EPS) * gain[:, c * width:(c + 1) * width])
    return outs


def _proj_kernel(rope_tiles, mem_tile, gate_tile0, h_ref, w_ref, gain_ref, bias_ref, cos_ref, sa_ref,
                 sb_ref, o_ref, wbf_ref):
    n = pl.program_id(0)
    m = pl.program_id(1)

    @pl.when(m == 0)
    def _():
        wbf_ref[...] = w_ref[...].astype(BF16)

    acc = jnp.dot(h_ref[...], wbf_ref[...], preferred_element_type=F32)
    is_rope = functools.reduce(jnp.logical_or, [n == t for t in rope_tiles])
    is_mem = n == mem_tile
    is_gate = n >= gate_tile0

    @pl.when(is_rope)
    def _():
        cos, sa, sb = cos_ref[...], sa_ref[...], sb_ref[...]
        for c, y in enumerate(_chunk_norm(acc, gain_ref[...], HEAD)):
            y = y * cos + pltpu.roll(y, HEAD - ROT_DIM // 2, 1) * sa + pltpu.roll(y, ROT_DIM // 2, 1) * sb
            o_ref[:, c * HEAD:(c + 1) * HEAD] = y.astype(o_ref.dtype)

    @pl.when(is_mem)
    def _():
        for c, y in enumerate(_chunk_norm(acc, gain_ref[...], MEM_HEAD_DIM)):
            o_ref[:, c * MEM_HEAD_DIM:(c + 1) * MEM_HEAD_DIM] = y.astype(o_ref.dtype)

    @pl.when(is_gate)
    def _():
        z = acc + bias_ref[...]
        o_ref[...] = (1.0 / (1.0 + jnp.exp(-z))).astype(o_ref.dtype)

    @pl.when(jnp.logical_not(is_rope | is_mem | is_gate))
    def _():
        o_ref[...] = acc.astype(o_ref.dtype)


def _input_projection(h, w_in, gain_row, bias_row, cos_t, sa_t, sb_t, rope_tiles, mem_tile, gate_tile0):
    t, d = h.shape
    n_cols = w_in.shape[1]
    tm = min(ROW_TILE, t)
    tn = PROJ_TN
    row = lambda n, m: (m, 0)
    col = lambda n, m: (0, n)
    return pl.pallas_call(
        functools.partial(_proj_kernel, rope_tiles, mem_tile, gate_tile0),
        grid=(n_cols // tn, t // tm),
        in_specs=[
            pl.BlockSpec((tm, d), row),
            pl.BlockSpec((d, tn), col),
            pl.BlockSpec((1, tn), col),
            pl.BlockSpec((1, tn), col),
            pl.BlockSpec((tm, HEAD), row),
            pl.BlockSpec((tm, HEAD), row),
            pl.BlockSpec((tm, HEAD), row),
        ],
        out_specs=pl.BlockSpec((tm, tn), lambda n, m: (m, n)),
        out_shape=jax.ShapeDtypeStruct((t, n_cols), BF16),
        scratch_shapes=[pltpu.VMEM((d, tn), BF16)],
        compiler_params=_params(("arbitrary", "arbitrary")),
        name="input_projection",
    )(h, w_in, gain_row, bias_row, cos_t, sa_t, sb_t)


def _diff_attn_kernel(lam_init, q_ref, k_ref, v_ref, lam_ref, subln_ref, o_ref, m_sc, l_sc, acc_sc):
    qi = pl.program_id(2)
    tq = q_ref.shape[0]
    m_sc[...] = jnp.full(m_sc.shape, NEG_INF, F32)
    l_sc[...] = jnp.zeros(l_sc.shape, F32)
    acc_sc[...] = jnp.zeros(acc_sc.shape, F32)

    def step(j, masked):
        start = pl.multiple_of(j * tq, tq)
        k = k_ref[pl.ds(start, tq), :]
        v = v_ref[pl.ds(start, tq), :]
        for c in range(2):
            s = lax.dot_general(q_ref[:, c * HEAD:(c + 1) * HEAD], k[:, c * HEAD:(c + 1) * HEAD],
                                (((1,), (1,)), ((), ())), preferred_element_type=F32)
            if masked:
                row = lax.broadcasted_iota(jnp.int32, s.shape, 0)
                col = lax.broadcasted_iota(jnp.int32, s.shape, 1)
                s = jnp.where(row >= col, s, NEG_INF)
            m_prev = m_sc[c]
            m_new = jnp.maximum(m_prev, jnp.max(s, axis=1, keepdims=True))
            alpha = jnp.exp(m_prev - m_new)
            p = jnp.exp(s - m_new)
            l_sc[c] = alpha * l_sc[c] + jnp.sum(p, axis=1, keepdims=True)
            acc_sc[c] = alpha * acc_sc[c] + jnp.dot(p.astype(v.dtype), v, preferred_element_type=F32)
            m_sc[c] = m_new

    def body(j, carry):
        step(j, False)
        return carry

    lax.fori_loop(0, qi, body, 0)
    step(qi, True)

    lp = lam_ref[...]
    lam = (jnp.exp(jnp.sum(lp[0:1] * lp[1:2], axis=1, keepdims=True))
           - jnp.exp(jnp.sum(lp[2:3] * lp[3:4], axis=1, keepdims=True)) + lam_init)
    o = acc_sc[0] / l_sc[0] - lam * (acc_sc[1] / l_sc[1])
    ms = jnp.mean(o * o, axis=-1, keepdims=True)
    o = o * lax.rsqrt(ms + NORM_EPS) * subln_ref[...] * (1.0 - lam_init)
    o_ref[...] = o.astype(o_ref.dtype)


def _diff_attention(proj, batch, seq, q_blk0, k_blk0, v_blk0, diff_lambda, subln, lam_init):
    tq = min(DIFF_TQ, seq)
    nq = seq // tq
    dv = DIFF_V_DIM
    return pl.pallas_call(
        functools.partial(_diff_attn_kernel, lam_init),
        grid=(batch, DIFF_HEADS, nq),
        in_specs=[
            pl.BlockSpec((tq, dv), lambda b, h, i: (b * nq + i, q_blk0 + h)),
            pl.BlockSpec((seq, dv), lambda b, h, i: (b, k_blk0 + h)),
            pl.BlockSpec((seq, dv), lambda b, h, i: (b, v_blk0 + h)),
            pl.BlockSpec((4, HEAD), lambda b, h, i: (0, 0)),
            pl.BlockSpec((1, dv), lambda b, h, i: (0, 0)),
        ],
        out_specs=pl.BlockSpec((tq, dv), lambda b, h, i: (b * nq + i, h)),
        out_shape=jax.ShapeDtypeStruct((batch * seq, DIFF_HEADS * dv), BF16),
        scratch_shapes=[pltpu.VMEM((2, tq, 1), F32), pltpu.VMEM((2, tq, 1), F32),
                        pltpu.VMEM((2, tq, dv), F32)],
        compiler_params=_params(("arbitrary", "arbitrary", "arbitrary")),
        name="diff_attention",
    )(proj, proj, proj, diff_lambda, subln.reshape(1, dv))


def _dilated_multiplicity(tq):
    n_chunks = DIL_PAD // tq + 1
    q = np.arange(tq)[:, None]
    k = np.arange(tq)[None, :]
    out = np.zeros((n_chunks, tq, tq), np.float32)
    for c in range(n_chunks):
        d = DIL_PAD - c * tq + q - k
        for window, dil in DIL_CONFIGS:
            out[c] += ((d >= 0) & (d <= window) & (d % dil == 0)).astype(np.float32)
    return out


def _dil_attn_kernel(q_ref, k_ref, v_ref, mult_ref, o_ref, m_sc, l_sc, acc_sc):
    qi = pl.program_id(2)
    tq = q_ref.shape[0]
    n_chunks = mult_ref.shape[0]
    m_sc[...] = jnp.full(m_sc.shape, NEG_INF, F32)
    l_sc[...] = jnp.zeros(l_sc.shape, F32)
    acc_sc[...] = jnp.zeros(acc_sc.shape, F32)
    q = q_ref[...]

    def body(c, carry):
        start = pl.multiple_of((qi - (n_chunks - 1) + c) * tq, tq)
        k = k_ref[pl.ds(start, tq), :]
        v = v_ref[pl.ds(start, tq), :]
        w = mult_ref[c]
        s = lax.dot_general(q, k, (((1,), (1,)), ((), ())), preferred_element_type=F32)
        s = jnp.where(w > 0.0, s, NEG_INF)
        m_prev = m_sc[...]
        m_new = jnp.maximum(m_prev, jnp.max(s, axis=1, keepdims=True))
        alpha = jnp.exp(m_prev - m_new)
        p = w * jnp.exp(s - m_new)
        l_sc[...] = alpha * l_sc[...] + jnp.sum(p, axis=1, keepdims=True)
        acc_sc[...] = alpha * acc_sc[...] + jnp.dot(p.astype(v.dtype), v, preferred_element_type=F32)
        m_sc[...] = m_new
        return carry

    lax.fori_loop(jnp.maximum(n_chunks - 1 - qi, 0), n_chunks, body, 0)
    o_ref[...] = (acc_sc[...] / l_sc[...]).astype(o_ref.dtype)


def _dilated_attention(proj, batch, seq, q_blk0, k_blk0, v_blk0):
    tq = DIL_TQ
    nq = seq // tq
    mult = jnp.asarray(_dilated_multiplicity(tq))
    return pl.pallas_call(
        _dil_attn_kernel,
        grid=(batch, DIL_HEADS, nq),
        in_specs=[
            pl.BlockSpec((tq, HEAD), lambda b, h, i: (b * nq + i, q_blk0 + h)),
            pl.BlockSpec((seq, HEAD), lambda b, h, i: (b, k_blk0 + h)),
            pl.BlockSpec((seq, HEAD), lambda b, h, i: (b, v_blk0 + h)),
            pl.BlockSpec(mult.shape, lambda b, h, i: (0, 0, 0)),
        ],
        out_specs=pl.BlockSpec((tq, HEAD), lambda b, h, i: (b * nq + i, h)),
        out_shape=jax.ShapeDtypeStruct((batch * seq, DIL_HEADS * HEAD), BF16),
        scratch_shapes=[pltpu.VMEM((tq, 1), F32), pltpu.VMEM((tq, 1), F32), pltpu.VMEM((tq, HEAD), F32)],
        compiler_params=_params(("arbitrary", "arbitrary", "arbitrary")),
        name="dilated_attention",
    )(proj, proj, proj, mult)


def _mem_kv_kernel(mem_ref, g_ref, w_ref, kn_ref, o_ref):
    x = mem_ref[...]
    ms = jnp.mean(x * x, axis=-1, keepdims=True)
    h = (x * lax.rsqrt(ms + NORM_EPS) * g_ref[...]).astype(BF16)
    kv = jnp.dot(h, w_ref[...], preferred_element_type=F32)
    half = kv.shape[1] // 2
    for c, y in enumerate(_chunk_norm(kv[:, :half], kn_ref[...], MEM_HEAD_DIM)):
        o_ref[:, c * MEM_HEAD_DIM:(c + 1) * MEM_HEAD_DIM] = y.astype(o_ref.dtype)
    o_ref[:, half:] = kv[:, half:].astype(o_ref.dtype)


def _mem_kv(mem2d, g_mem_norm, w_mem_kv_bf, k_gain_row, batch):
    rows, d = mem2d.shape
    mlen = rows // batch
    n = w_mem_kv_bf.shape[1]
    return pl.pallas_call(
        _mem_kv_kernel,
        grid=(batch,),
        in_specs=[
            pl.BlockSpec((mlen, d), lambda b: (b, 0)),
            pl.BlockSpec((1, d), lambda b: (0, 0)),
            pl.BlockSpec((d, n), lambda b: (0, 0)),
            pl.BlockSpec((1, n // 2), lambda b: (0, 0)),
        ],
        out_specs=pl.BlockSpec((mlen, n), lambda b: (b, 0)),
        out_shape=jax.ShapeDtypeStruct((rows, n), BF16),
        compiler_params=_params(("arbitrary",)),
        name="mem_kv",
    )(mem2d, g_mem_norm.reshape(1, d), w_mem_kv_bf, k_gain_row)


def _mem_attn_kernel(q_ref, kv_ref, o_ref):
    half = kv_ref.shape[1] // 2
    for h in range(MEM_HEADS):
        lo, hi = h * MEM_HEAD_DIM, (h + 1) * MEM_HEAD_DIM
        s = lax.dot_general(q_ref[:, lo:hi], kv_ref[:, lo:hi], (((1,), (1,)), ((), ())),
                            preferred_element_type=F32)
        e = jnp.exp(s - jnp.max(s, axis=1, keepdims=True))
        p = e / jnp.sum(e, axis=1, keepdims=True)
        o = jnp.dot(p.astype(kv_ref.dtype), kv_ref[:, half + lo:half + hi], preferred_element_type=F32)
        o_ref[:, lo:hi] = o.astype(o_ref.dtype)


def _memory_attention(proj, kv, batch, seq, q_blk):
    tm = min(ROW_TILE, seq)
    per_batch = seq // tm
    mlen = kv.shape[0] // batch
    width = MEM_HEADS * MEM_HEAD_DIM
    return pl.pallas_call(
        _mem_attn_kernel,
        grid=(batch * per_batch,),
        in_specs=[
            pl.BlockSpec((tm, width), lambda i: (i, q_blk)),
            pl.BlockSpec((mlen, 2 * width), lambda i: (i // per_batch, 0)),
        ],
        out_specs=pl.BlockSpec((tm, width), lambda i: (i, 0)),
        out_shape=jax.ShapeDtypeStruct((batch * seq, width), BF16),
        compiler_params=_params(("arbitrary",)),
        name="memory_attention",
    )(proj, kv)


def _merge_kernel(od_ref, ol_ref, om_ref, wd_ref, wl_ref, wm_ref, gd_ref, gl_ref, gm_ref, o_ref):
    acc = gd_ref[...].astype(F32) * jnp.dot(od_ref[...], wd_ref[...], preferred_element_type=F32)
    acc += gl_ref[...].astype(F32) * jnp.dot(ol_ref[...], wl_ref[...], preferred_element_type=F32)
    acc += gm_ref[...].astype(F32) * jnp.dot(om_ref[...], wm_ref[...], preferred_element_type=F32)
    o_ref[...] = acc.astype(o_ref.dtype)


def _branch_merge(o_diff, o_dil, o_mem, wd, wl, wm, proj, gate_tile0, d_model):
    t = o_diff.shape[0]
    tm = min(ROW_TILE, t)
    tn = PROJ_TN
    per_branch = d_model // tn
    row = lambda n, m: (m, 0)
    col = lambda n, m: (0, n)
    gate = lambda br: (lambda n, m: (m, gate_tile0 + br * per_branch + n))
    return pl.pallas_call(
        _merge_kernel,
        grid=(per_branch, t // tm),
        in_specs=[
            pl.BlockSpec((tm, o_diff.shape[1]), row),
            pl.BlockSpec((tm, o_dil.shape[1]), row),
            pl.BlockSpec((tm, o_mem.shape[1]), row),
            pl.BlockSpec((wd.shape[0], tn), col),
            pl.BlockSpec((wl.shape[0], tn), col),
            pl.BlockSpec((wm.shape[0], tn), col),
            pl.BlockSpec((tm, tn), gate(0)),
            pl.BlockSpec((tm, tn), gate(1)),
            pl.BlockSpec((tm, tn), gate(2)),
        ],
        out_specs=pl.BlockSpec((tm, tn), lambda n, m: (m, n)),
        out_shape=jax.ShapeDtypeStruct((t, d_model), BF16),
        compiler_params=_params(("arbitrary", "arbitrary")),
        name="branch_merge",
    )(o_diff, o_dil, o_mem, wd, wl, wm, proj, proj, proj)


def _split_bf16(x):
    hi = x.astype(BF16)
    return hi, (x - hi.astype(F32)).astype(BF16)


def _out_proj_kernel(mg_ref, w_ref, x_ref, g_ref, wr_hi_ref, wr_lo_ref, br_ref, x1_ref, h2_ref, lg_ref):
    x1 = x_ref[...] + jnp.dot(mg_ref[...], w_ref[...], preferred_element_type=F32)
    x1_ref[...] = x1
    ms = jnp.mean(x1 * x1, axis=-1, keepdims=True)
    h2 = x1 * lax.rsqrt(ms + NORM_EPS) * g_ref[...]
    h2_ref[...] = h2
    hi, lo = _split_bf16(h2)
    lg = (jnp.dot(hi, wr_hi_ref[...], preferred_element_type=F32)
          + jnp.dot(lo, wr_hi_ref[...], preferred_element_type=F32)
          + jnp.dot(hi, wr_lo_ref[...], preferred_element_type=F32))
    lane = lax.broadcasted_iota(jnp.int32, lg.shape, 1)
    lg_ref[...] = jnp.where(lane < N_EXPERTS, lg + br_ref[...], -jnp.inf)


def _output_projection(merged, w_out_bf, x2d, g_ffn, wr_hi, wr_lo, b_router_row):
    t, d = x2d.shape
    tm = min(ROW_TILE, t)
    row = lambda i: (i, 0)
    fixed = lambda i: (0, 0)
    return pl.pallas_call(
        _out_proj_kernel,
        grid=(t // tm,),
        in_specs=[
            pl.BlockSpec((tm, d), row),
            pl.BlockSpec((d, d), fixed),
            pl.BlockSpec((tm, d), row),
            pl.BlockSpec((1, d), fixed),
            pl.BlockSpec((d, LANES), fixed),
            pl.BlockSpec((d, LANES), fixed),
            pl.BlockSpec((1, LANES), fixed),
        ],
        out_specs=[pl.BlockSpec((tm, d), row), pl.BlockSpec((tm, d), row), pl.BlockSpec((tm, LANES), row)],
        out_shape=[jax.ShapeDtypeStruct((t, d), F32), jax.ShapeDtypeStruct((t, d), F32),
                   jax.ShapeDtypeStruct((t, LANES), F32)],
        compiler_params=_params(("arbitrary",)),
        name="output_projection",
    )(merged, w_out_bf, x2d, g_ffn.reshape(1, d), wr_hi, wr_lo, b_router_row)


def _route_kernel(lg_ref, idx_ref, gate_ref, rank_ref, cnt_ref, carry_sc):
    i = pl.program_id(0)

    @pl.when(i == 0)
    def _():
        carry_sc[...] = jnp.zeros(carry_sc.shape, F32)

    work = lg_ref[...]
    tm = work.shape[0]
    lane = lax.broadcasted_iota(jnp.int32, work.shape, 1).astype(F32)
    vals, idxs = [], []
    member = jnp.zeros(work.shape, F32)
    for _ in range(TOP_K):
        mx = jnp.max(work, axis=1, keepdims=True)
        idx = jnp.min(jnp.where(work == mx, lane, float(LANES)), axis=1, keepdims=True)
        sel = lane == idx
        vals.append(mx)
        idxs.append(idx)
        member = jnp.where(sel, 1.0, member)
        work = jnp.where(sel, -jnp.inf, work)
    exps = [jnp.exp(v - vals[0]) for v in vals]
    z = exps[0] + exps[1] + exps[2] + exps[3]
    r = lax.broadcasted_iota(jnp.int32, (tm, tm), 0)
    c = lax.broadcasted_iota(jnp.int32, (tm, tm), 1)
    before = jnp.where(c < r, 1.0, 0.0).astype(BF16)
    cum = jnp.dot(before, member.astype(BF16), preferred_element_type=F32) + carry_sc[...]
    idx_out = jnp.zeros(work.shape, F32)
    gate_out = jnp.zeros(work.shape, F32)
    rank_out = jnp.zeros(work.shape, F32)
    for k in range(TOP_K):
        rank = jnp.sum(jnp.where(lane == idxs[k], cum, 0.0), axis=1, keepdims=True)
        idx_out = jnp.where(lane == float(k), idxs[k], idx_out)
        gate_out = jnp.where(lane == float(k), exps[k] / z, gate_out)
        rank_out = jnp.where(lane == float(k), rank, rank_out)
    idx_ref[...] = idx_out.astype(jnp.int32)
    gate_ref[...] = gate_out
    rank_ref[...] = rank_out.astype(jnp.int32)
    carry_sc[...] += jnp.sum(member, axis=0, keepdims=True)
    cnt_ref[...] = carry_sc[...].astype(jnp.int32)


def _route(logits):
    t = logits.shape[0]
    tm = min(ROW_TILE, t)
    row = lambda i: (i, 0)
    return pl.pallas_call(
        _route_kernel,
        grid=(t // tm,),
        in_specs=[pl.BlockSpec((tm, LANES), row)],
        out_specs=[pl.BlockSpec((tm, LANES), row), pl.BlockSpec((tm, LANES), row),
                   pl.BlockSpec((tm, LANES), row), pl.BlockSpec((1, LANES), lambda i: (0, 0))],
        out_shape=[jax.ShapeDtypeStruct((t, LANES), jnp.int32), jax.ShapeDtypeStruct((t, LANES), F32),
                   jax.ShapeDtypeStruct((t, LANES), jnp.int32), jax.ShapeDtypeStruct((1, LANES), jnp.int32)],
        scratch_shapes=[pltpu.VMEM((1, LANES), F32)],
        compiler_params=_params(("arbitrary",)),
        name="route",
    )(logits)


def _row_copy(src_hbm, row, dst, dst_row, sem):
    return pltpu.make_async_copy(src_hbm.at[pl.ds(row, 1)], dst.at[pl.ds(dst_row, 1)], sem)


def _dispatch_kernel(tok_ref, used_ref, h_hbm, o_ref, buf, sem):
    i = pl.program_id(0)
    rows = buf.shape[0]

    @pl.when(i < used_ref[0])
    def _():
        def start(r, carry):
            _row_copy(h_hbm, tok_ref[i * rows + r], buf, r, sem).start()
            return carry

        def wait(r, carry):
            _row_copy(h_hbm, 0, buf, r, sem).wait()
            return carry

        lax.fori_loop(0, rows, start, 0)
        lax.fori_loop(0, rows, wait, 0)
        o_ref[...] = buf[...].astype(o_ref.dtype)

    @pl.when(i >= used_ref[0])
    def _():
        o_ref[...] = jnp.zeros(o_ref.shape, o_ref.dtype)


def _dispatch(h2, slot_tok, n_used, n_blocks):
    d = h2.shape[1]
    rows = EXPERT_ROWS
    return pl.pallas_call(
        _dispatch_kernel,
        grid_spec=pltpu.PrefetchScalarGridSpec(
            num_scalar_prefetch=2,
            grid=(n_blocks,),
            in_specs=[pl.BlockSpec(memory_space=pl.ANY)],
            out_specs=pl.BlockSpec((rows, d), lambda i, tok, used: (i, 0)),
            scratch_shapes=[pltpu.VMEM((rows, d), F32), pltpu.SemaphoreType.DMA(())],
        ),
        out_shape=jax.ShapeDtypeStruct((n_blocks * rows, d), BF16),
        compiler_params=_params(("arbitrary",)),
        name="dispatch",
    )(slot_tok, n_used, h2)


def _gate_up_kernel(be_ref, used_ref, x_ref, wg_ref, wu_ref, bg_ref, bu_ref, o_ref, wg_bf, wu_bf):
    m = pl.program_id(1)

    @pl.when(m < used_ref[0])
    def _():
        @pl.when((m == 0) | (be_ref[m] != be_ref[jnp.maximum(m - 1, 0)]))
        def _():
            wg_bf[...] = wg_ref[...].astype(BF16)
            wu_bf[...] = wu_ref[...].astype(BF16)

        x = x_ref[...]
        gate = jnp.dot(x, wg_bf[...], preferred_element_type=F32) + bg_ref[...]
        up = jnp.dot(x, wu_bf[...], preferred_element_type=F32) + bu_ref[...]
        gate = jnp.minimum(gate, SWIGLU_LIMIT)
        up = jnp.clip(up, -SWIGLU_LIMIT, SWIGLU_LIMIT)
        act = (up + 1.0) * gate * (1.0 / (1.0 + jnp.exp(-SWIGLU_ALPHA * gate)))
        o_ref[...] = act.astype(o_ref.dtype)

    @pl.when(m >= used_ref[0])
    def _():
        o_ref[...] = jnp.zeros(o_ref.shape, o_ref.dtype)


def _gate_up(xs, w_gate_up, b_gate_up, block_e, n_used):
    p, d = xs.shape
    d_ff = w_gate_up.shape[2] // 2
    rows, tf = EXPERT_ROWS, FF_TILE
    nf = d_ff // tf
    blk = lambda m, used: jnp.minimum(m, used[0] - 1)
    return pl.pallas_call(
        _gate_up_kernel,
        grid_spec=pltpu.PrefetchScalarGridSpec(
            num_scalar_prefetch=2,
            grid=(nf, p // rows),
            in_specs=[
                pl.BlockSpec((rows, d), lambda f, m, be, used: (blk(m, used), 0)),
                pl.BlockSpec((None, d, tf), lambda f, m, be, used: (be[blk(m, used)], 0, f)),
                pl.BlockSpec((None, d, tf), lambda f, m, be, used: (be[blk(m, used)], 0, nf + f)),
                pl.BlockSpec((None, 1, tf), lambda f, m, be, used: (be[blk(m, used)], 0, f)),
                pl.BlockSpec((None, 1, tf), lambda f, m, be, used: (be[blk(m, used)], 0, nf + f)),
            ],
            out_specs=pl.BlockSpec((rows, tf), lambda f, m, be, used: (m, f)),
            scratch_shapes=[pltpu.VMEM((d, tf), BF16), pltpu.VMEM((d, tf), BF16)],
        ),
        out_shape=jax.ShapeDtypeStruct((p, d_ff), BF16),
        compiler_params=_params(("arbitrary", "arbitrary")),
        name="expert_gate_up",
    )(block_e, n_used, xs, w_gate_up, w_gate_up, b_gate_up, b_gate_up)


def _down_kernel(be_ref, used_ref, a_ref, w_ref, b_ref, sw_ref, o_ref, w_bf):
    m = pl.program_id(1)

    @pl.when(m < used_ref[0])
    def _():
        @pl.when((m == 0) | (be_ref[m] != be_ref[jnp.maximum(m - 1, 0)]))
        def _():
            w_bf[...] = w_ref[...].astype(BF16)

        y = jnp.dot(a_ref[...], w_bf[...], preferred_element_type=F32) + b_ref[...]
        o_ref[...] = y * sw_ref[...]

    @pl.when(m >= used_ref[0])
    def _():
        o_ref[...] = jnp.zeros(o_ref.shape, o_ref.dtype)


def _down(act, w_down, b_down, slot_w, block_e, n_used):
    p, d_ff = act.shape
    d = w_down.shape[2]
    rows, tn = EXPERT_ROWS, DOWN_TN
    blk = lambda m, used: jnp.minimum(m, used[0] - 1)
    return pl.pallas_call(
        _down_kernel,
        grid_spec=pltpu.PrefetchScalarGridSpec(
            num_scalar_prefetch=2,
            grid=(d // tn, p // rows),
            in_specs=[
                pl.BlockSpec((rows, d_ff), lambda n, m, be, used: (blk(m, used), 0)),
                pl.BlockSpec((None, d_ff, tn), lambda n, m, be, used: (be[blk(m, used)], 0, n)),
                pl.BlockSpec((None, 1, tn), lambda n, m, be, used: (be[blk(m, used)], 0, n)),
                pl.BlockSpec((rows, 1), lambda n, m, be, used: (blk(m, used), 0)),
            ],
            out_specs=pl.BlockSpec((rows, tn), lambda n, m, be, used: (m, n)),
            scratch_shapes=[pltpu.VMEM((d_ff, tn), BF16)],
        ),
        out_shape=jax.ShapeDtypeStruct((p, d), F32),
        compiler_params=_params(("arbitrary", "arbitrary")),
        name="expert_down",
    )(block_e, n_used, act, w_down, b_down, slot_w)


def _combine_kernel(dest_ref, x_ref, y_hbm, o_ref, buf, sem):
    i = pl.program_id(0)
    rows = x_ref.shape[0]

    def start(t, carry):
        for k in range(TOP_K):
            _row_copy(y_hbm, dest_ref[(i * rows + t) * TOP_K + k], buf.at[k], t, sem).start()
        return carry

    def wait(t, carry):
        for k in range(TOP_K):
            _row_copy(y_hbm, 0, buf.at[k], t, sem).wait()
        return carry

    lax.fori_loop(0, rows, start, 0)
    lax.fori_loop(0, rows, wait, 0)
    o_ref[...] = x_ref[...] + ((buf[0] + buf[1]) + (buf[2] + buf[3]))


def _combine(x1, y, dest_flat):
    t, d = x1.shape
    rows = min(COMBINE_ROWS, t)
    return pl.pallas_call(
        _combine_kernel,
        grid_spec=pltpu.PrefetchScalarGridSpec(
            num_scalar_prefetch=1,
            grid=(t // rows,),
            in_specs=[pl.BlockSpec((rows, d), lambda i, dest: (i, 0)), pl.BlockSpec(memory_space=pl.ANY)],
            out_specs=pl.BlockSpec((rows, d), lambda i, dest: (i, 0)),
            scratch_shapes=[pltpu.VMEM((TOP_K, rows, d), F32), pltpu.SemaphoreType.DMA(())],
        ),
        out_shape=jax.ShapeDtypeStruct((t, d), F32),
        compiler_params=_params(("arbitrary",)),
        name="combine",
    )(dest_flat, x1, y)


def _moe(x1, h2, logits, w_gate_up, b_gate_up, w_down, b_down):
    t, d = x1.shape
    n_exp = w_gate_up.shape[0]
    rows = EXPERT_ROWS
    idx_p, gate_p, rank_p, cnt_p = _route(logits)
    idx, gates, rank = idx_p[:, :TOP_K], gate_p[:, :TOP_K], rank_p[:, :TOP_K]
    counts = cnt_p[0, :n_exp]

    padded = (counts + rows - 1) // rows * rows
    pend = jnp.cumsum(padded)
    pstart = pend - padded
    n_blocks = (t * TOP_K + n_exp * (rows - 1) + rows - 1) // rows
    dest = (pstart[idx] + rank).astype(jnp.int32)
    dest_flat = dest.reshape(-1)
    tok_flat = jnp.repeat(jnp.arange(t, dtype=jnp.int32), TOP_K)
    slot_tok = jnp.zeros((n_blocks * rows,), jnp.int32).at[dest_flat].set(tok_flat)
    slot_w = jnp.zeros((n_blocks * rows,), F32).at[dest_flat].set(gates.reshape(-1))
    block_e = jnp.clip(jnp.searchsorted(pend, jnp.arange(n_blocks, dtype=jnp.int32) * rows, side="right"),
                       0, n_exp - 1).astype(jnp.int32)
    n_used = (pend[-1] // rows).astype(jnp.int32).reshape(1)

    xs = _dispatch(h2, slot_tok, n_used, n_blocks)
    act = _gate_up(xs, w_gate_up, b_gate_up.reshape(n_exp, 1, -1), block_e, n_used)
    y = _down(act, w_down, b_down.reshape(n_exp, 1, -1), slot_w.reshape(-1, 1), block_e, n_used)
    return _combine(x1, y, dest_flat)


def _rope_tables(positions):
    half = ROT_DIM // 2
    inv_freq = ROPE_THETA ** (-jnp.arange(0, ROT_DIM, 2, dtype=F32) / ROT_DIM)
    ang = positions.reshape(-1).astype(F32)[:, None] * inv_freq
    cos, sin = jnp.cos(ang), jnp.sin(ang)
    t = ang.shape[0]
    cos_t = jnp.concatenate([cos, cos, jnp.ones((t, HEAD - ROT_DIM), F32)], axis=1)
    sa_t = jnp.concatenate([-sin, jnp.zeros((t, HEAD - half), F32)], axis=1)
    sb_t = jnp.concatenate([jnp.zeros((t, half), F32), sin, jnp.zeros((t, HEAD - ROT_DIM), F32)], axis=1)
    return cos_t, sa_t, sb_t


def _layer(x2d, mem2d, batch, seq, cos_t, sa_t, sb_t, layer_idx, g_mix_norm, w_in, b_gate, diff_q_norm,
           diff_k_norm, diff_lambda, diff_subln, dil_q_norm, dil_k_norm, g_mem_norm, w_mem_kv, mem_q_norm,
           mem_k_norm, w_branch_diff, w_branch_dil, w_branch_mem, w_out, g_ffn_norm, w_router, b_router,
           w_gate_up, b_gate_up, w_down, b_down):
    d = x2d.shape[1]
    diff_cols = DIFF_HEADS * 2 * HEAD
    dil_cols = DIL_HEADS * HEAD
    mem_cols = MEM_HEADS * MEM_HEAD_DIM
    off_dq, off_dk, off_dv = 0, diff_cols, 2 * diff_cols
    off_lq = 3 * diff_cols
    off_lk, off_lv = off_lq + dil_cols, off_lq + 2 * dil_cols
    off_mq = off_lq + 3 * dil_cols
    off_gate = off_mq + mem_cols
    n_cols = off_gate + 3 * d
    assert w_in.shape == (d, n_cols)
    tn = PROJ_TN
    tiles = lambda off, width: tuple(range(off // tn, (off + width) // tn))
    rope_tiles = (tiles(off_dq, diff_cols) + tiles(off_dk, diff_cols) + tiles(off_lq, dil_cols)
                  + tiles(off_lk, dil_cols))

    rep = lambda g, width: jnp.tile(g.astype(F32), width // g.shape[0])
    gain_row = jnp.concatenate([
        rep(diff_q_norm, diff_cols) * HEAD ** -0.5, rep(diff_k_norm, diff_cols), jnp.ones((diff_cols,), F32),
        rep(dil_q_norm, dil_cols) * HEAD ** -0.5, rep(dil_k_norm, dil_cols), jnp.ones((dil_cols,), F32),
        rep(mem_q_norm, mem_cols) * MEM_HEAD_DIM ** -0.5, jnp.ones((3 * d,), F32)]).reshape(1, n_cols)
    bias_row = jnp.concatenate([jnp.zeros((off_gate,), F32), b_gate.astype(F32)]).reshape(1, n_cols)

    h = _rmsnorm(x2d, g_mix_norm, BF16)
    proj = _input_projection(h, w_in, gain_row, bias_row, cos_t, sa_t, sb_t, rope_tiles, off_mq // tn,
                             off_gate // tn)

    lam_init = 0.8 - 0.6 * float(np.exp(-0.3 * layer_idx))
    o_diff = _diff_attention(proj, batch, seq, off_dq // DIFF_V_DIM, off_dk // DIFF_V_DIM, off_dv // DIFF_V_DIM,
                             diff_lambda.astype(F32), diff_subln, lam_init)
    o_dil = _dilated_attention(proj, batch, seq, off_lq // HEAD, off_lk // HEAD, off_lv // HEAD)
    kv = _mem_kv(mem2d, g_mem_norm, w_mem_kv.astype(BF16), rep(mem_k_norm, mem_cols).reshape(1, mem_cols), batch)
    o_mem = _memory_attention(proj, kv, batch, seq, off_mq // mem_cols)

    merged = _branch_merge(o_diff, o_dil, o_mem, w_branch_diff.astype(BF16), w_branch_dil.astype(BF16),
                           w_branch_mem.astype(BF16), proj, off_gate // tn, d)

    n_exp = w_router.shape[1]
    wr = jnp.zeros((d, LANES), F32).at[:, :n_exp].set(w_router)
    wr_hi, wr_lo = _split_bf16(wr)
    br_row = jnp.zeros((1, LANES), F32).at[0, :n_exp].set(b_router)
    x1, h2, logits = _output_projection(merged, w_out.astype(BF16), x2d, g_ffn_norm, wr_hi, wr_lo, br_row)
    return _moe(x1, h2, logits, w_gate_up, b_gate_up, w_down, b_down)


def kernel(x, mem, positions, g_mix_norm, w_in, b_gate, diff_q_norm, diff_k_norm, diff_lambda, diff_subln,
           dil_q_norm, dil_k_norm, g_mem_norm, w_mem_kv, mem_q_norm, mem_k_norm, w_branch_diff, w_branch_dil,
           w_branch_mem, w_out, g_ffn_norm, w_router, b_router, w_gate_up, b_gate_up, w_down, b_down):
    batch, seq, d = x.shape
    cos_t, sa_t, sb_t = _rope_tables(positions)
    x2d = x.reshape(batch * seq, d)
    mem2d = mem.reshape(-1, d)
    for l in range(g_mix_norm.shape[0]):
        x2d = _layer(x2d, mem2d, batch, seq, cos_t, sa_t, sb_t, l, g_mix_norm[l], w_in[l], b_gate[l],
                     diff_q_norm[l], diff_k_norm[l], diff_lambda[l], diff_subln[l], dil_q_norm[l],
                     dil_k_norm[l], g_mem_norm[l], w_mem_kv[l], mem_q_norm[l], mem_k_norm[l],
                     w_branch_diff[l], w_branch_dil[l], w_branch_mem[l], w_out[l], g_ffn_norm[l],
                     w_router[l], b_router[l], w_gate_up[l], b_gate_up[l], w_down[l], b_down[l])
    return x2d.reshape(batch, seq, d)
```

```python
import functools

import numpy as np
import jax
import jax.numpy as jnp
from jax import lax
from jax.experimental import pallas as pl
from jax.experimental.pallas import tpu as pltpu

F32 = jnp.float32
BF16 = jnp.bfloat16

NORM_EPS = 1e-6
NEG_INF = -1e30
ROPE_THETA = 500000.0
ROT_DIM = 32
HEAD = 128
DIFF_HEADS = 8
DIFF_V_DIM = 256
DIL_HEADS = 8
DIL_CONFIGS = ((128, 1), (512, 4), (2048, 16))
DIL_PAD = 2048
MEM_HEADS = 4
MEM_HEAD_DIM = 256
N_EXPERTS = 32
TOP_K = 4
SWIGLU_LIMIT = 7.0
SWIGLU_ALPHA = 1.702

LANES = 128
VMEM_LIMIT = 56 * 1024 * 1024

PROJ_TN = 1024
ROW_TILE = 512
DIFF_TQ = 512
DIFF_SUB = 128
DIL_TQ = 256
EXPERT_ROWS = 512
FF_TILE = 512
DOWN_TN = 1024
COMBINE_ROWS = 256


def _params(sem, **kw):
    return pltpu.CompilerParams(dimension_semantics=sem, vmem_limit_bytes=VMEM_LIMIT, **kw)


def _rmsnorm_kernel(x_ref, g_ref, o_ref):
    x = x_ref[...]
    ms = jnp.mean(x * x, axis=-1, keepdims=True)
    o_ref[...] = (x * lax.rsqrt(ms + NORM_EPS) * g_ref[...]).astype(o_ref.dtype)


def _rmsnorm(x, g, out_dtype):
    t, d = x.shape
    tm = min(ROW_TILE, t)
    return pl.pallas_call(
        _rmsnorm_kernel,
        grid=(t // tm,),
        in_specs=[pl.BlockSpec((tm, d), lambda i: (i, 0)), pl.BlockSpec((1, d), lambda i: (0, 0))],
        out_specs=pl.BlockSpec((tm, d), lambda i: (i, 0)),
        out_shape=jax.ShapeDtypeStruct((t, d), out_dtype),
        compiler_params=_params(("arbitrary",)),
        name="rmsnorm",
    )(x, g.reshape(1, d))


def _chunk_norm(x, gain, width):
    outs = []
    for c in range(x.shape[1] // width):
        xc = x[:, c * width:(c + 1) * width]
        ms = jnp.mean(xc * xc, axis=-1, keepdims=True)
        outs.append(xc * lax.rsqrt(ms + NORM_EPS) * gain[:, c * width:(c + 1) * width])
    return outs


def _proj_kernel(rope_tiles, mem_tile, gate_tile0, h_ref, w_ref, gain_ref, bias_ref, cos_ref, sa_ref,
                 sb_ref, o_ref, wbf_ref):
    n = pl.program_id(0)
    m = pl.program_id(1)

    @pl.when(m == 0)
    def _():
        wbf_ref[...] = w_ref[...].astype(BF16)

    acc = jnp.dot(h_ref[...], wbf_ref[...], preferred_element_type=F32)
    is_rope = functools.reduce(jnp.logical_or, [n == t for t in rope_tiles])
    is_mem = n == mem_tile
    is_gate = n >= gate_tile0

    @pl.when(is_rope)
    def _():
        cos, sa, sb = cos_ref[...], sa_ref[...], sb_ref[...]
        for c, y in enumerate(_chunk_norm(acc, gain_ref[...], HEAD)):
            y = y * cos + pltpu.roll(y, HEAD - ROT_DIM // 2, 1) * sa + pltpu.roll(y, ROT_DIM // 2, 1) * sb
            o_ref[:, c * HEAD:(c + 1) * HEAD] = y.astype(o_ref.dtype)

    @pl.when(is_mem)
    def _():
        for c, y in enumerate(_chunk_norm(acc, gain_ref[...], MEM_HEAD_DIM)):
            o_ref[:, c * MEM_HEAD_DIM:(c + 1) * MEM_HEAD_DIM] = y.astype(o_ref.dtype)

    @pl.when(is_gate)
    def _():
        z = acc + bias_ref[...]
        o_ref[...] = (1.0 / (1.0 + jnp.exp(-z))).astype(o_ref.dtype)

    @pl.when(jnp.logical_not(is_rope | is_mem | is_gate))
    def _():
        o_ref[...] = acc.astype(o_ref.dtype)


def _input_projection(h, w_in, gain_row, bias_row, cos_t, sa_t, sb_t, rope_tiles, mem_tile, gate_tile0):
    t, d = h.shape
    n_cols = w_in.shape[1]
    tm = min(ROW_TILE, t)
    tn = PROJ_TN
    row = lambda n, m: (m, 0)
    col = lambda n, m: (0, n)
    return pl.pallas_call(
        functools.partial(_proj_kernel, rope_tiles, mem_tile, gate_tile0),
        grid=(n_cols // tn, t // tm),
        in_specs=[
            pl.BlockSpec((tm, d), row),
            pl.BlockSpec((d, tn), col),
            pl.BlockSpec((1, tn), col),
            pl.BlockSpec((1, tn), col),
            pl.BlockSpec((tm, HEAD), row),
            pl.BlockSpec((tm, HEAD), row),
            pl.BlockSpec((tm, HEAD), row),
        ],
        out_specs=pl.BlockSpec((tm, tn), lambda n, m: (m, n)),
        out_shape=jax.ShapeDtypeStruct((t, n_cols), BF16),
        scratch_shapes=[pltpu.VMEM((d, tn), BF16)],
        compiler_params=_params(("arbitrary", "arbitrary")),
        name="input_projection",
    )(h, w_in, gain_row, bias_row, cos_t, sa_t, sb_t)


def _diff_attn_kernel(lam_init, q_ref, k_ref, v_ref, lam_ref, subln_ref, o_ref, m_sc, l_sc, acc_sc):
    qi = pl.program_id(2)
    tq = q_ref.shape[0]
    m_sc[...] = jnp.full(m_sc.shape, NEG_INF, F32)
    l_sc[...] = jnp.zeros(l_sc.shape, F32)
    acc_sc[...] = jnp.zeros(acc_sc.shape, F32)

    sub = min(DIFF_SUB, tq)

    def step(j, diagonal):
        start = pl.multiple_of(j * tq, tq)
        for c in range(2):
            cols = slice(c * HEAD, (c + 1) * HEAD)
            for r in range(tq // sub):
                rows = slice(r * sub, (r + 1) * sub)
                nk = (r + 1) * sub if diagonal else tq
                s = lax.dot_general(q_ref[rows, cols], k_ref[pl.ds(start, nk), cols],
                                    (((1,), (1,)), ((), ())), preferred_element_type=F32)
                if diagonal:
                    row = lax.broadcasted_iota(jnp.int32, s.shape, 0) + r * sub
                    col = lax.broadcasted_iota(jnp.int32, s.shape, 1)
                    s = jnp.where(row >= col, s, NEG_INF)
                m_prev = m_sc[c, rows]
                m_new = jnp.maximum(m_prev, jnp.max(s, axis=1, keepdims=True))
                alpha = jnp.exp(m_prev - m_new)
                p = jnp.exp(s - jnp.tile(m_new, (1, nk // LANES)))
                l_sc[c, rows] = alpha * l_sc[c, rows] + jnp.sum(p, axis=1, keepdims=True)
                pv = jnp.dot(p.astype(v_ref.dtype), v_ref[pl.ds(start, nk), :], preferred_element_type=F32)
                acc_sc[c, rows] = jnp.tile(alpha, (1, acc_sc.shape[2] // LANES)) * acc_sc[c, rows] + pv
                m_sc[c, rows] = m_new

    def body(j, carry):
        step(j, False)
        return carry

    lax.fori_loop(0, qi, body, 0)
    step(qi, True)

    lp = lam_ref[...]
    lam = (jnp.exp(jnp.sum(lp[0:1] * lp[1:2], axis=1, keepdims=True))
           - jnp.exp(jnp.sum(lp[2:3] * lp[3:4], axis=1, keepdims=True)) + lam_init)
    o = acc_sc[0] / l_sc[0][:, :1] - lam * (acc_sc[1] / l_sc[1][:, :1])
    ms = jnp.mean(o * o, axis=-1, keepdims=True)
    o = o * lax.rsqrt(ms + NORM_EPS) * subln_ref[...] * (1.0 - lam_init)
    o_ref[...] = o.astype(o_ref.dtype)


def _diff_attention(proj, batch, seq, q_blk0, k_blk0, v_blk0, diff_lambda, subln, lam_init):
    tq = min(DIFF_TQ, seq)
    nq = seq // tq
    dv = DIFF_V_DIM
    return pl.pallas_call(
        functools.partial(_diff_attn_kernel, lam_init),
        grid=(batch, DIFF_HEADS, nq),
        in_specs=[
            pl.BlockSpec((tq, dv), lambda b, h, i: (b * nq + i, q_blk0 + h)),
            pl.BlockSpec((seq, dv), lambda b, h, i: (b, k_blk0 + h)),
            pl.BlockSpec((seq, dv), lambda b, h, i: (b, v_blk0 + h)),
            pl.BlockSpec((4, HEAD), lambda b, h, i: (0, 0)),
            pl.BlockSpec((1, dv), lambda b, h, i: (0, 0)),
        ],
        out_specs=pl.BlockSpec((tq, dv), lambda b, h, i: (b * nq + i, h)),
        out_shape=jax.ShapeDtypeStruct((batch * seq, DIFF_HEADS * dv), BF16),
        scratch_shapes=[pltpu.VMEM((2, tq, LANES), F32), pltpu.VMEM((2, tq, LANES), F32),
                        pltpu.VMEM((2, tq, dv), F32)],
        compiler_params=_params(("arbitrary", "arbitrary", "arbitrary")),
        name="diff_attention",
    )(proj, proj, proj, diff_lambda, subln.reshape(1, dv))


def _dilated_multiplicity(tq):
    n_chunks = DIL_PAD // tq + 1
    q = np.arange(tq)[:, None]
    k = np.arange(tq)[None, :]
    out = np.zeros((n_chunks + 1, tq, tq), np.float32)
    for c in range(n_chunks):
        d = DIL_PAD - c * tq + q - k
        for window, dil in DIL_CONFIGS:
            out[c] += ((d >= 0) & (d <= window) & (d % dil == 0)).astype(np.float32)
    return out


def _dil_attn_kernel(q_ref, k_ref, v_ref, mult_ref, o_ref):
    qi = pl.program_id(2)
    tq = q_ref.shape[0]
    n_chunks = mult_ref.shape[0] - 1
    q = q_ref[...]
    starts, weights, scores = [], [], []
    for c in range(n_chunks):
        blk = qi - (n_chunks - 1) + c
        starts.append(pl.multiple_of(jnp.maximum(blk, 0) * tq, tq))
        w = mult_ref[jnp.where(blk >= 0, c, n_chunks)]
        s = lax.dot_general(q, k_ref[pl.ds(starts[c], tq), :], (((1,), (1,)), ((), ())),
                            preferred_element_type=F32)
        weights.append(w)
        scores.append(jnp.where(w > 0.0, s, NEG_INF))
    m = jnp.max(functools.reduce(jnp.maximum, scores), axis=1, keepdims=True)
    den = jnp.zeros((tq, 1), F32)
    acc = jnp.zeros(o_ref.shape, F32)
    for c in range(n_chunks):
        p = weights[c] * jnp.exp(scores[c] - m)
        den += jnp.sum(p, axis=1, keepdims=True)
        acc += jnp.dot(p.astype(v_ref.dtype), v_ref[pl.ds(starts[c], tq), :], preferred_element_type=F32)
    o_ref[...] = (acc / den).astype(o_ref.dtype)


def _dilated_attention(proj, batch, seq, q_blk0, k_blk0, v_blk0):
    tq = DIL_TQ
    nq = seq // tq
    mult = jnp.asarray(_dilated_multiplicity(tq))
    return pl.pallas_call(
        _dil_attn_kernel,
        grid=(batch, DIL_HEADS, nq),
        in_specs=[
            pl.BlockSpec((tq, HEAD), lambda b, h, i: (b * nq + i, q_blk0 + h)),
            pl.BlockSpec((seq, HEAD), lambda b, h, i: (b, k_blk0 + h)),
            pl.BlockSpec((seq, HEAD), lambda b, h, i: (b, v_blk0 + h)),
            pl.BlockSpec(mult.shape, lambda b, h, i: (0, 0, 0)),
        ],
        out_specs=pl.BlockSpec((tq, HEAD), lambda b, h, i: (b * nq + i, h)),
        out_shape=jax.ShapeDtypeStruct((batch * seq, DIL_HEADS * HEAD), BF16),
        compiler_params=_params(("arbitrary", "arbitrary", "arbitrary")),
        name="dilated_attention",
    )(proj, proj, proj, mult)


def _mem_kv_kernel(mem_ref, g_ref, w_ref, kn_ref, o_ref):
    x = mem_ref[...]
    ms = jnp.mean(x * x, axis=-1, keepdims=True)
    h = (x * lax.rsqrt(ms + NORM_EPS) * g_ref[...]).astype(BF16)
    kv = jnp.dot(h, w_ref[...], preferred_element_type=F32)
    half = kv.shape[1] // 2
    for c, y in enumerate(_chunk_norm(kv[:, :half], kn_ref[...], MEM_HEAD_DIM)):
        o_ref[:, c * MEM_HEAD_DIM:(c + 1) * MEM_HEAD_DIM] = y.astype(o_ref.dtype)
    o_ref[:, half:] = kv[:, half:].astype(o_ref.dtype)


def _mem_kv(mem2d, g_mem_norm, w_mem_kv_bf, k_gain_row, batch):
    rows, d = mem2d.shape
    mlen = rows // batch
    n = w_mem_kv_bf.shape[1]
    return pl.pallas_call(
        _mem_kv_kernel,
        grid=(batch,),
        in_specs=[
            pl.BlockSpec((mlen, d), lambda b: (b, 0)),
            pl.BlockSpec((1, d), lambda b: (0, 0)),
            pl.BlockSpec((d, n), lambda b: (0, 0)),
            pl.BlockSpec((1, n // 2), lambda b: (0, 0)),
        ],
        out_specs=pl.BlockSpec((mlen, n), lambda b: (b, 0)),
        out_shape=jax.ShapeDtypeStruct((rows, n), BF16),
        compiler_params=_params(("arbitrary",)),
        name="mem_kv",
    )(mem2d, g_mem_norm.reshape(1, d), w_mem_kv_bf, k_gain_row)


def _mem_attn_kernel(q_ref, kv_ref, o_ref):
    half = kv_ref.shape[1] // 2
    for h in range(MEM_HEADS):
        lo, hi = h * MEM_HEAD_DIM, (h + 1) * MEM_HEAD_DIM
        s = lax.dot_general(q_ref[:, lo:hi], kv_ref[:, lo:hi], (((1,), (1,)), ((), ())),
                            preferred_element_type=F32)
        e = jnp.exp(s - jnp.max(s, axis=1, keepdims=True))
        p = e / jnp.sum(e, axis=1, keepdims=True)
        o = jnp.dot(p.astype(kv_ref.dtype), kv_ref[:, half + lo:half + hi], preferred_element_type=F32)
        o_ref[:, lo:hi] = o.astype(o_ref.dtype)


def _memory_attention(proj, kv, batch, seq, q_blk):
    tm = min(ROW_TILE, seq)
    per_batch = seq // tm
    mlen = kv.shape[0] // batch
    width = MEM_HEADS * MEM_HEAD_DIM
    return pl.pallas_call(
        _mem_attn_kernel,
        grid=(batch * per_batch,),
        in_specs=[
            pl.BlockSpec((tm, width), lambda i: (i, q_blk)),
            pl.BlockSpec((mlen, 2 * width), lambda i: (i // per_batch, 0)),
        ],
        out_specs=pl.BlockSpec((tm, width), lambda i: (i, 0)),
        out_shape=jax.ShapeDtypeStruct((batch * seq, width), BF16),
        compiler_params=_params(("arbitrary",)),
        name="memory_attention",
    )(proj, kv)


def _merge_kernel(od_ref, ol_ref, om_ref, wd_ref, wl_ref, wm_ref, gd_ref, gl_ref, gm_ref, o_ref):
    acc = gd_ref[...].astype(F32) * jnp.dot(od_ref[...], wd_ref[...], preferred_element_type=F32)
    acc += gl_ref[...].astype(F32) * jnp.dot(ol_ref[...], wl_ref[...], preferred_element_type=F32)
    acc += gm_ref[...].astype(F32) * jnp.dot(om_ref[...], wm_ref[...], preferred_element_type=F32)
    o_ref[...] = acc.astype(o_ref.dtype)


def _branch_merge(o_diff, o_dil, o_mem, wd, wl, wm, proj, gate_tile0, d_model):
    t = o_diff.shape[0]
    tm = min(ROW_TILE, t)
    tn = PROJ_TN
    per_branch = d_model // tn
    row = lambda n, m: (m, 0)
    col = lambda n, m: (0, n)
    gate = lambda br: (lambda n, m: (m, gate_tile0 + br * per_branch + n))
    return pl.pallas_call(
        _merge_kernel,
        grid=(per_branch, t // tm),
        in_specs=[
            pl.BlockSpec((tm, o_diff.shape[1]), row),
            pl.BlockSpec((tm, o_dil.shape[1]), row),
            pl.BlockSpec((tm, o_mem.shape[1]), row),
            pl.BlockSpec((wd.shape[0], tn), col),
            pl.BlockSpec((wl.shape[0], tn), col),
            pl.BlockSpec((wm.shape[0], tn), col),
            pl.BlockSpec((tm, tn), gate(0)),
            pl.BlockSpec((tm, tn), gate(1)),
            pl.BlockSpec((tm, tn), gate(2)),
        ],
        out_specs=pl.BlockSpec((tm, tn), lambda n, m: (m, n)),
        out_shape=jax.ShapeDtypeStruct((t, d_model), BF16),
        compiler_params=_params(("arbitrary", "arbitrary")),
        name="branch_merge",
    )(o_diff, o_dil, o_mem, wd, wl, wm, proj, proj, proj)


def _split_bf16(x):
    hi = x.astype(BF16)
    return hi, (x - hi.astype(F32)).astype(BF16)


def _pack_bf16_pairs(x):
    n = x.shape[1] // 2
    bits = lax.bitcast_convert_type(x.astype(F32), jnp.uint32)
    return (bits[:, :n] >> 16) | (bits[:, n:] & jnp.uint32(0xFFFF0000))


def _unpack_bf16_pairs(u):
    lo = lax.bitcast_convert_type(u << 16, F32).astype(BF16)
    hi = lax.bitcast_convert_type(u & jnp.uint32(0xFFFF0000), F32).astype(BF16)
    return lo, hi


def _out_proj_kernel(mg_ref, w_ref, x_ref, g_ref, wr_hi_ref, wr_lo_ref, br_ref, x1_ref, h2_ref, lg_ref):
    x1 = x_ref[...] + jnp.dot(mg_ref[...], w_ref[...], preferred_element_type=F32)
    x1_ref[...] = x1
    ms = jnp.mean(x1 * x1, axis=-1, keepdims=True)
    h2 = x1 * lax.rsqrt(ms + NORM_EPS) * g_ref[...]
    hi, lo = _split_bf16(h2)
    h2_ref[...] = _pack_bf16_pairs(hi)
    lg = (jnp.dot(hi, wr_hi_ref[...], preferred_element_type=F32)
          + jnp.dot(lo, wr_hi_ref[...], preferred_element_type=F32)
          + jnp.dot(hi, wr_lo_ref[...], preferred_element_type=F32))
    lane = lax.broadcasted_iota(jnp.int32, lg.shape, 1)
    lg_ref[...] = jnp.where(lane < N_EXPERTS, lg + br_ref[...], -jnp.inf)


def _output_projection(merged, w_out_bf, x2d, g_ffn, wr_hi, wr_lo, b_router_row):
    t, d = x2d.shape
    tm = min(ROW_TILE, t)
    row = lambda i: (i, 0)
    fixed = lambda i: (0, 0)
    return pl.pallas_call(
        _out_proj_kernel,
        grid=(t // tm,),
        in_specs=[
            pl.BlockSpec((tm, d), row),
            pl.BlockSpec((d, d), fixed),
            pl.BlockSpec((tm, d), row),
            pl.BlockSpec((1, d), fixed),
            pl.BlockSpec((d, LANES), fixed),
            pl.BlockSpec((d, LANES), fixed),
            pl.BlockSpec((1, LANES), fixed),
        ],
        out_specs=[pl.BlockSpec((tm, d), row), pl.BlockSpec((tm, d // 2), row), pl.BlockSpec((tm, LANES), row)],
        out_shape=[jax.ShapeDtypeStruct((t, d), F32), jax.ShapeDtypeStruct((t, d // 2), jnp.uint32),
                   jax.ShapeDtypeStruct((t, LANES), F32)],
        compiler_params=_params(("arbitrary",)),
        name="output_projection",
    )(merged, w_out_bf, x2d, g_ffn.reshape(1, d), wr_hi, wr_lo, b_router_row)


def _route_kernel(lg_ref, idx_ref, gate_ref, rank_ref, cnt_ref, carry_sc):
    i = pl.program_id(0)

    @pl.when(i == 0)
    def _():
        carry_sc[...] = jnp.zeros(carry_sc.shape, F32)

    work = lg_ref[...]
    tm = work.shape[0]
    lane = lax.broadcasted_iota(jnp.int32, work.shape, 1).astype(F32)
    vals, idxs = [], []
    member = jnp.zeros(work.shape, F32)
    for _ in range(TOP_K):
        mx = jnp.max(work, axis=1, keepdims=True)
        idx = jnp.min(jnp.where(work == mx, lane, float(LANES)), axis=1, keepdims=True)
        sel = lane == idx
        vals.append(mx)
        idxs.append(idx)
        member = jnp.where(sel, 1.0, member)
        work = jnp.where(sel, -jnp.inf, work)
    exps = [jnp.exp(v - vals[0]) for v in vals]
    z = exps[0] + exps[1] + exps[2] + exps[3]
    r = lax.broadcasted_iota(jnp.int32, (tm, tm), 0)
    c = lax.broadcasted_iota(jnp.int32, (tm, tm), 1)
    before = jnp.where(c < r, 1.0, 0.0).astype(BF16)
    cum = jnp.dot(before, member.astype(BF16), preferred_element_type=F32) + carry_sc[...]
    idx_out = jnp.zeros(work.shape, F32)
    gate_out = jnp.zeros(work.shape, F32)
    rank_out = jnp.zeros(work.shape, F32)
    for k in range(TOP_K):
        rank = jnp.sum(jnp.where(lane == idxs[k], cum, 0.0), axis=1, keepdims=True)
        idx_out = jnp.where(lane == float(k), idxs[k], idx_out)
        gate_out = jnp.where(lane == float(k), exps[k] / z, gate_out)
        rank_out = jnp.where(lane == float(k), rank, rank_out)
    idx_ref[...] = idx_out.astype(jnp.int32)
    gate_ref[...] = gate_out
    rank_ref[...] = rank_out.astype(jnp.int32)
    carry_sc[...] += jnp.sum(member, axis=0, keepdims=True)
    cnt_ref[...] = carry_sc[...].astype(jnp.int32)


def _route(logits):
    t = logits.shape[0]
    tm = min(ROW_TILE, t)
    row = lambda i: (i, 0)
    return pl.pallas_call(
        _route_kernel,
        grid=(t // tm,),
        in_specs=[pl.BlockSpec((tm, LANES), row)],
        out_specs=[pl.BlockSpec((tm, LANES), row), pl.BlockSpec((tm, LANES), row),
                   pl.BlockSpec((tm, LANES), row), pl.BlockSpec((1, LANES), lambda i: (0, 0))],
        out_shape=[jax.ShapeDtypeStruct((t, LANES), jnp.int32), jax.ShapeDtypeStruct((t, LANES), F32),
                   jax.ShapeDtypeStruct((t, LANES), jnp.int32), jax.ShapeDtypeStruct((1, LANES), jnp.int32)],
        scratch_shapes=[pltpu.VMEM((1, LANES), F32)],
        compiler_params=_params(("arbitrary",)),
        name="route",
    )(logits)


def _row_copy(src_hbm, row, dst, dst_row, sem):
    return pltpu.make_async_copy(src_hbm.at[pl.ds(row, 1)], dst.at[pl.ds(dst_row, 1)], sem)


def _dispatch_kernel(dest_ref, h_ref, init_hbm, o_hbm, sem):
    del init_hbm
    i = pl.program_id(0)
    rows = h_ref.shape[0]

    def copy(t, slot):
        return pltpu.make_async_copy(h_ref.at[pl.ds(t, 1)], o_hbm.at[pl.ds(slot, 1)], sem)

    def start(t, carry):
        for k in range(TOP_K):
            copy(t, dest_ref[(i * rows + t) * TOP_K + k]).start()
        return carry

    def wait(t, carry):
        for k in range(TOP_K):
            copy(t, 0).wait()
        return carry

    lax.fori_loop(0, rows, start, 0)
    lax.fori_loop(0, rows, wait, 0)


def _dispatch(h2_packed, dest_flat, n_slots):
    t, w = h2_packed.shape
    rows = min(COMBINE_ROWS, t)
    return pl.pallas_call(
        _dispatch_kernel,
        grid_spec=pltpu.PrefetchScalarGridSpec(
            num_scalar_prefetch=1,
            grid=(t // rows,),
            in_specs=[pl.BlockSpec((rows, w), lambda i, dest: (i, 0)), pl.BlockSpec(memory_space=pl.ANY)],
            out_specs=pl.BlockSpec(memory_space=pl.ANY),
            scratch_shapes=[pltpu.SemaphoreType.DMA(())],
        ),
        out_shape=jax.ShapeDtypeStruct((n_slots, w), h2_packed.dtype),
        input_output_aliases={2: 0},
        compiler_params=_params(("arbitrary",)),
        name="dispatch",
    )(dest_flat, h2_packed, jnp.zeros((n_slots, w), h2_packed.dtype))


def _gate_up_kernel(be_ref, used_ref, x_ref, wg_ref, wu_ref, bg_ref, bu_ref, o_ref, wg_bf, wu_bf):
    m = pl.program_id(1)

    @pl.when(m < used_ref[0])
    def _():
        @pl.when((m == 0) | (be_ref[m] != be_ref[jnp.maximum(m - 1, 0)]))
        def _():
            wg_bf[...] = wg_ref[...].astype(BF16)
            wu_bf[...] = wu_ref[...].astype(BF16)

        lo, hi = _unpack_bf16_pairs(x_ref[...])
        half = lo.shape[1]
        gate = (jnp.dot(lo, wg_bf[:half, :], preferred_element_type=F32)
                + jnp.dot(hi, wg_bf[half:, :], preferred_element_type=F32) + bg_ref[...])
        up = (jnp.dot(lo, wu_bf[:half, :], preferred_element_type=F32)
              + jnp.dot(hi, wu_bf[half:, :], preferred_element_type=F32) + bu_ref[...])
        gate = jnp.minimum(gate, SWIGLU_LIMIT)
        up = jnp.clip(up, -SWIGLU_LIMIT, SWIGLU_LIMIT)
        act = (up + 1.0) * gate * (1.0 / (1.0 + jnp.exp(-SWIGLU_ALPHA * gate)))
        o_ref[...] = act.astype(o_ref.dtype)

    @pl.when(m >= used_ref[0])
    def _():
        o_ref[...] = jnp.zeros(o_ref.shape, o_ref.dtype)


def _gate_up(xs, w_gate_up, b_gate_up, block_e, n_used):
    p = xs.shape[0]
    d = w_gate_up.shape[1]
    d_ff = w_gate_up.shape[2] // 2
    rows, tf = EXPERT_ROWS, FF_TILE
    nf = d_ff // tf
    blk = lambda m, used: jnp.minimum(m, used[0] - 1)
    return pl.pallas_call(
        _gate_up_kernel,
        grid_spec=pltpu.PrefetchScalarGridSpec(
            num_scalar_prefetch=2,
            grid=(nf, p // rows),
            in_specs=[
                pl.BlockSpec((rows, xs.shape[1]), lambda f, m, be, used: (blk(m, used), 0)),
                pl.BlockSpec((None, d, tf), lambda f, m, be, used: (be[blk(m, used)], 0, f)),
                pl.BlockSpec((None, d, tf), lambda f, m, be, used: (be[blk(m, used)], 0, nf + f)),
                pl.BlockSpec((None, 1, tf), lambda f, m, be, used: (be[blk(m, used)], 0, f)),
                pl.BlockSpec((None, 1, tf), lambda f, m, be, used: (be[blk(m, used)], 0, nf + f)),
            ],
            out_specs=pl.BlockSpec((rows, tf), lambda f, m, be, used: (m, f)),
            scratch_shapes=[pltpu.VMEM((d, tf), BF16), pltpu.VMEM((d, tf), BF16)],
        ),
        out_shape=jax.ShapeDtypeStruct((p, d_ff), BF16),
        compiler_params=_params(("arbitrary", "arbitrary")),
        name="expert_gate_up",
    )(block_e, n_used, xs, w_gate_up, w_gate_up, b_gate_up, b_gate_up)


def _down_kernel(be_ref, used_ref, a_ref, w_ref, b_ref, o_ref, w_bf):
    m = pl.program_id(1)

    @pl.when(m < used_ref[0])
    def _():
        @pl.when((m == 0) | (be_ref[m] != be_ref[jnp.maximum(m - 1, 0)]))
        def _():
            w_bf[...] = w_ref[...].astype(BF16)

        o_ref[...] = jnp.dot(a_ref[...], w_bf[...], preferred_element_type=F32) + b_ref[...]

    @pl.when(m >= used_ref[0])
    def _():
        o_ref[...] = jnp.zeros(o_ref.shape, o_ref.dtype)


def _down(act, w_down, b_down, block_e, n_used):
    p, d_ff = act.shape
    d = w_down.shape[2]
    rows, tn = EXPERT_ROWS, DOWN_TN
    blk = lambda m, used: jnp.minimum(m, used[0] - 1)
    return pl.pallas_call(
        _down_kernel,
        grid_spec=pltpu.PrefetchScalarGridSpec(
            num_scalar_prefetch=2,
            grid=(d // tn, p // rows),
            in_specs=[
                pl.BlockSpec((rows, d_ff), lambda n, m, be, used: (blk(m, used), 0)),
                pl.BlockSpec((None, d_ff, tn), lambda n, m, be, used: (be[blk(m, used)], 0, n)),
                pl.BlockSpec((None, 1, tn), lambda n, m, be, used: (be[blk(m, used)], 0, n)),
            ],
            out_specs=pl.BlockSpec((rows, tn), lambda n, m, be, used: (m, n)),
            scratch_shapes=[pltpu.VMEM((d_ff, tn), BF16)],
        ),
        out_shape=jax.ShapeDtypeStruct((p, d), F32),
        compiler_params=_params(("arbitrary", "arbitrary")),
        name="expert_down",
    )(block_e, n_used, act, w_down, b_down)


def _combine_kernel(dest_ref, x_ref, g_ref, y_hbm, o_ref, buf, sem):
    i = pl.program_id(0)
    rows = x_ref.shape[0]

    def start(t, carry):
        for k in range(TOP_K):
            _row_copy(y_hbm, dest_ref[(i * rows + t) * TOP_K + k], buf.at[k], t, sem).start()
        return carry

    def wait(t, carry):
        for k in range(TOP_K):
            _row_copy(y_hbm, 0, buf.at[k], t, sem).wait()
        return carry

    lax.fori_loop(0, rows, start, 0)
    lax.fori_loop(0, rows, wait, 0)
    g = g_ref[...]
    moe = ((g[:, 0:1] * buf[0] + g[:, 1:2] * buf[1]) + (g[:, 2:3] * buf[2] + g[:, 3:4] * buf[3]))
    o_ref[...] = x_ref[...] + moe


def _combine(x1, gates_padded, y, dest_flat):
    t, d = x1.shape
    rows = min(COMBINE_ROWS, t)
    return pl.pallas_call(
        _combine_kernel,
        grid_spec=pltpu.PrefetchScalarGridSpec(
            num_scalar_prefetch=1,
            grid=(t // rows,),
            in_specs=[pl.BlockSpec((rows, d), lambda i, dest: (i, 0)),
                      pl.BlockSpec((rows, LANES), lambda i, dest: (i, 0)),
                      pl.BlockSpec(memory_space=pl.ANY)],
            out_specs=pl.BlockSpec((rows, d), lambda i, dest: (i, 0)),
            scratch_shapes=[pltpu.VMEM((TOP_K, rows, d), F32), pltpu.SemaphoreType.DMA(())],
        ),
        out_shape=jax.ShapeDtypeStruct((t, d), F32),
        compiler_params=_params(("arbitrary",)),
        name="combine",
    )(dest_flat, x1, gates_padded, y)


def _moe(x1, h2, logits, w_gate_up, b_gate_up, w_down, b_down):
    t, d = x1.shape
    n_exp = w_gate_up.shape[0]
    rows = EXPERT_ROWS
    idx_p, gate_p, rank_p, cnt_p = _route(logits)
    idx, rank = idx_p[:, :TOP_K], rank_p[:, :TOP_K]
    counts = cnt_p[0, :n_exp]

    padded = (counts + rows - 1) // rows * rows
    pend = jnp.cumsum(padded)
    pstart = pend - padded
    n_blocks = (t * TOP_K + n_exp * (rows - 1) + rows - 1) // rows
    dest_flat = (pstart[idx] + rank).astype(jnp.int32).reshape(-1)
    block_start = jnp.arange(n_blocks, dtype=jnp.int32) * rows
    block_e = jnp.minimum(jnp.sum(pend[None, :] <= block_start[:, None], axis=1), n_exp - 1).astype(jnp.int32)
    n_used = (pend[-1] // rows).astype(jnp.int32).reshape(1)

    xs = _dispatch(h2, dest_flat, n_blocks * rows)
    act = _gate_up(xs, w_gate_up, b_gate_up.reshape(n_exp, 1, -1), block_e, n_used)
    y = _down(act, w_down, b_down.reshape(n_exp, 1, -1), block_e, n_used)
    return _combine(x1, gate_p, y, dest_flat)


def _rope_tables(positions):
    half = ROT_DIM // 2
    inv_freq = ROPE_THETA ** (-jnp.arange(0, ROT_DIM, 2, dtype=F32) / ROT_DIM)
    ang = positions.reshape(-1).astype(F32)[:, None] * inv_freq
    cos, sin = jnp.cos(ang), jnp.sin(ang)
    t = ang.shape[0]
    cos_t = jnp.concatenate([cos, cos, jnp.ones((t, HEAD - ROT_DIM), F32)], axis=1)
    sa_t = jnp.concatenate([-sin, jnp.zeros((t, HEAD - half), F32)], axis=1)
    sb_t = jnp.concatenate([jnp.zeros((t, half), F32), sin, jnp.zeros((t, HEAD - ROT_DIM), F32)], axis=1)
    return cos_t, sa_t, sb_t


def _layer(x2d, mem2d, batch, seq, cos_t, sa_t, sb_t, layer_idx, g_mix_norm, w_in, b_gate, diff_q_norm,
           diff_k_norm, diff_lambda, diff_subln, dil_q_norm, dil_k_norm, g_mem_norm, w_mem_kv, mem_q_norm,
           mem_k_norm, w_branch_diff, w_branch_dil, w_branch_mem, w_out, g_ffn_norm, w_router, b_router,
           w_gate_up, b_gate_up, w_down, b_down):
    d = x2d.shape[1]
    diff_cols = DIFF_HEADS * 2 * HEAD
    dil_cols = DIL_HEADS * HEAD
    mem_cols = MEM_HEADS * MEM_HEAD_DIM
    off_dq, off_dk, off_dv = 0, diff_cols, 2 * diff_cols
    off_lq = 3 * diff_cols
    off_lk, off_lv = off_lq + dil_cols, off_lq + 2 * dil_cols
    off_mq = off_lq + 3 * dil_cols
    off_gate = off_mq + mem_cols
    n_cols = off_gate + 3 * d
    assert w_in.shape == (d, n_cols)
    tn = PROJ_TN
    tiles = lambda off, width: tuple(range(off // tn, (off + width) // tn))
    rope_tiles = (tiles(off_dq, diff_cols) + tiles(off_dk, diff_cols) + tiles(off_lq, dil_cols)
                  + tiles(off_lk, dil_cols))

    rep = lambda g, width: jnp.tile(g.astype(F32), width // g.shape[0])
    gain_row = jnp.concatenate([
        rep(diff_q_norm, diff_cols) * HEAD ** -0.5, rep(diff_k_norm, diff_cols), jnp.ones((diff_cols,), F32),
        rep(dil_q_norm, dil_cols) * HEAD ** -0.5, rep(dil_k_norm, dil_cols), jnp.ones((dil_cols,), F32),
        rep(mem_q_norm, mem_cols) * MEM_HEAD_DIM ** -0.5, jnp.ones((3 * d,), F32)]).reshape(1, n_cols)
    bias_row = jnp.concatenate([jnp.zeros((off_gate,), F32), b_gate.astype(F32)]).reshape(1, n_cols)

    h = _rmsnorm(x2d, g_mix_norm, BF16)
    proj = _input_projection(h, w_in, gain_row, bias_row, cos_t, sa_t, sb_t, rope_tiles, off_mq // tn,
                             off_gate // tn)

    lam_init = 0.8 - 0.6 * float(np.exp(-0.3 * layer_idx))
    o_diff = _diff_attention(proj, batch, seq, off_dq // DIFF_V_DIM, off_dk // DIFF_V_DIM, off_dv // DIFF_V_DIM,
                             diff_lambda.astype(F32), diff_subln, lam_init)
    o_dil = _dilated_attention(proj, batch, seq, off_lq // HEAD, off_lk // HEAD, off_lv // HEAD)
    kv = _mem_kv(mem2d, g_mem_norm, w_mem_kv.astype(BF16), rep(mem_k_norm, mem_cols).reshape(1, mem_cols), batch)
    o_mem = _memory_attention(proj, kv, batch, seq, off_mq // mem_cols)

    merged = _branch_merge(o_diff, o_dil, o_mem, w_branch_diff.astype(BF16), w_branch_dil.astype(BF16),
                           w_branch_mem.astype(BF16), proj, off_gate // tn, d)

    n_exp = w_router.shape[1]
    wr = jnp.zeros((d, LANES), F32).at[:, :n_exp].set(w_router)
    wr_hi, wr_lo = _split_bf16(wr)
    br_row = jnp.zeros((1, LANES), F32).at[0, :n_exp].set(b_router)
    x1, h2, logits = _output_projection(merged, w_out.astype(BF16), x2d, g_ffn_norm, wr_hi, wr_lo, br_row)
    return _moe(x1, h2, logits, w_gate_up, b_gate_up, w_down, b_down)


def kernel(x, mem, positions, g_mix_norm, w_in, b_gate, diff_q_norm, diff_k_norm, diff_lambda, diff_subln,
           dil_q_norm, dil_k_norm, g_mem_norm, w_mem_kv, mem_q_norm, mem_k_norm, w_branch_diff, w_branch_dil,
           w_branch_mem, w_out, g_ffn_norm, w_router, b_router, w_gate_up, b_gate_up, w_down, b_down):
    batch, seq, d = x.shape
    cos_t, sa_t, sb_t = _rope_tables(positions)
    x2d = x.reshape(batch * seq, d)
    mem2d = mem.reshape(-1, d)
    for l in range(g_mix_norm.shape[0]):
        x2d = _layer(x2d, mem2d, batch, seq, cos_t, sa_t, sb_t, l, g_mix_norm[l], w_in[l], b_gate[l],
                     diff_q_norm[l], diff_k_norm[l], diff_lambda[l], diff_subln[l], dil_q_norm[l],
                     dil_k_norm[l], g_mem_norm[l], w_mem_kv[l], mem_q_norm[l], mem_k_norm[l],
                     w_branch_diff[l], w_branch_dil[l], w_branch_mem[l], w_out[l], g_ffn_norm[l],
                     w_router[l], b_router[l], w_gate_up[l], b_gate_up[l], w_down[l], b_down[l])
    return x2d.reshape(batch, seq, d)
```

```python
import functools

import numpy as np
import jax
import jax.numpy as jnp
from jax import lax
from jax.experimental import pallas as pl
from jax.experimental.pallas import tpu as pltpu

F32 = jnp.float32
BF16 = jnp.bfloat16

NORM_EPS = 1e-6
NEG_INF = -1e30
ROPE_THETA = 500000.0
ROT_DIM = 32
HEAD = 128
DIFF_HEADS = 8
DIFF_V_DIM = 256
DIL_HEADS = 8
DIL_CONFIGS = ((128, 1), (512, 4), (2048, 16))
DIL_PAD = 2048
MEM_HEADS = 4
MEM_HEAD_DIM = 256
N_EXPERTS = 32
TOP_K = 4
SWIGLU_LIMIT = 7.0
SWIGLU_ALPHA = 1.702

LANES = 128
VMEM_LIMIT = 56 * 1024 * 1024

PROJ_TN = 1024
ROW_TILE = 512
DIFF_TQ = 1024
DIFF_TK = 1024
DIFF_SUB = 256
DIFF_FIXED_SHIFT_MAX = 40.0
LOG2E = 1.4426950408889634
DIL_TQ = 256
EXPERT_ROWS = 512
FF_TILE = 512
DOWN_TN = 1024
COMBINE_ROWS = 256
MATMUL_ROW_SPLIT = 2


def _params(sem, **kw):
    return pltpu.CompilerParams(dimension_semantics=sem, vmem_limit_bytes=VMEM_LIMIT, **kw)


def _rmsnorm_kernel(x_ref, g_ref, o_ref):
    x = x_ref[...]
    ms = jnp.mean(x * x, axis=-1, keepdims=True)
    o_ref[...] = (x * lax.rsqrt(ms + NORM_EPS) * g_ref[...]).astype(o_ref.dtype)


def _rmsnorm(x, g, out_dtype):
    t, d = x.shape
    tm = min(ROW_TILE, t)
    return pl.pallas_call(
        _rmsnorm_kernel,
        grid=(t // tm,),
        in_specs=[pl.BlockSpec((tm, d), lambda i: (i, 0)), pl.BlockSpec((1, d), lambda i: (0, 0))],
        out_specs=pl.BlockSpec((tm, d), lambda i: (i, 0)),
        out_shape=jax.ShapeDtypeStruct((t, d), out_dtype),
        compiler_params=_params(("arbitrary",)),
        name="rmsnorm",
    )(x, g.reshape(1, d))


def _row_blocks(rows):
    size = max(rows // MATMUL_ROW_SPLIT, 8)
    return [slice(r, r + size) for r in range(0, rows, size)]


def _chunk_norm(x, gain, width):
    outs = []
    for c in range(x.shape[1] // width):
        xc = x[:, c * width:(c + 1) * width]
        ms = jnp.mean(xc * xc, axis=-1, keepdims=True)
        outs.append(xc * lax.rsqrt(ms + NORM_EPS) * gain[:, c * width:(c + 1) * width])
    return outs


def _proj_kernel(rope_tiles, mem_tile, gate_tile0, h_ref, w_ref, gain_ref, bias_ref, cos_ref, sa_ref,
                 sb_ref, o_ref, wbf_ref):
    n = pl.program_id(0)
    m = pl.program_id(1)

    @pl.when(m == 0)
    def _():
        wbf_ref[...] = w_ref[...].astype(BF16)

    is_rope = functools.reduce(jnp.logical_or, [n == t for t in rope_tiles])
    is_mem = n == mem_tile
    is_gate = n >= gate_tile0

    def for_row_blocks(epilogue):
        for rows in _row_blocks(h_ref.shape[0]):
            epilogue(rows, jnp.dot(h_ref[rows, :], wbf_ref[...], preferred_element_type=F32))

    def rope(rows, acc):
        cos, sa, sb = cos_ref[rows, :], sa_ref[rows, :], sb_ref[rows, :]
        for c, y in enumerate(_chunk_norm(acc, gain_ref[...], HEAD)):
            y = y * cos + pltpu.roll(y, HEAD - ROT_DIM // 2, 1) * sa + pltpu.roll(y, ROT_DIM // 2, 1) * sb
            o_ref[rows, c * HEAD:(c + 1) * HEAD] = y.astype(o_ref.dtype)

    def mem_q(rows, acc):
        for c, y in enumerate(_chunk_norm(acc, gain_ref[...], MEM_HEAD_DIM)):
            o_ref[rows, c * MEM_HEAD_DIM:(c + 1) * MEM_HEAD_DIM] = y.astype(o_ref.dtype)

    def gate(rows, acc):
        z = acc + bias_ref[...]
        o_ref[rows, :] = (1.0 / (1.0 + jnp.exp(-z))).astype(o_ref.dtype)

    def plain(rows, acc):
        o_ref[rows, :] = acc.astype(o_ref.dtype)

    pl.when(is_rope)(lambda: for_row_blocks(rope))
    pl.when(is_mem)(lambda: for_row_blocks(mem_q))
    pl.when(is_gate)(lambda: for_row_blocks(gate))
    pl.when(jnp.logical_not(is_rope | is_mem | is_gate))(lambda: for_row_blocks(plain))


def _input_projection(h, w_in, gain_row, bias_row, cos_t, sa_t, sb_t, rope_tiles, mem_tile, gate_tile0):
    t, d = h.shape
    n_cols = w_in.shape[1]
    tm = min(ROW_TILE, t)
    tn = PROJ_TN
    row = lambda n, m: (m, 0)
    col = lambda n, m: (0, n)
    return pl.pallas_call(
        functools.partial(_proj_kernel, rope_tiles, mem_tile, gate_tile0),
        grid=(n_cols // tn, t // tm),
        in_specs=[
            pl.BlockSpec((tm, d), row),
            pl.BlockSpec((d, tn), col),
            pl.BlockSpec((1, tn), col),
            pl.BlockSpec((1, tn), col),
            pl.BlockSpec((tm, HEAD), row),
            pl.BlockSpec((tm, HEAD), row),
            pl.BlockSpec((tm, HEAD), row),
        ],
        out_specs=pl.BlockSpec((tm, tn), lambda n, m: (m, n)),
        out_shape=jax.ShapeDtypeStruct((t, n_cols), BF16),
        scratch_shapes=[pltpu.VMEM((d, tn), BF16)],
        compiler_params=_params(("arbitrary", "arbitrary")),
        name="input_projection",
    )(h, w_in, gain_row, bias_row, cos_t, sa_t, sb_t)


def _diff_attn_kernel(lam_init, bound_ref, q_ref, k_ref, v_ref, lam_ref, subln_ref, o_ref, m_sc, l_sc, acc_sc):
    qi = pl.program_id(2)
    tq = q_ref.shape[0]
    sub = min(DIFF_SUB, tq)
    tk = min(DIFF_TK, tq)
    per_tile = tq // tk
    bound = bound_ref[0]
    l_sc[...] = jnp.zeros(l_sc.shape, F32)
    acc_sc[...] = jnp.zeros(acc_sc.shape, F32)

    def row_plan(d):
        plan = []
        for r in range(tq // sub):
            if d is None or r * sub >= (d + 1) * tk:
                plan.append((r, tk, None))
            elif (r + 1) * sub > d * tk:
                plan.append((r, min(tk, (r + 1) * sub - d * tk), r * sub - d * tk))
        return plan

    def scores(start, c, r, nk, offset):
        s = lax.dot_general(q_ref[r * sub:(r + 1) * sub, c * HEAD:(c + 1) * HEAD],
                            k_ref[pl.ds(start, nk), c * HEAD:(c + 1) * HEAD],
                            (((1,), (1,)), ((), ())), preferred_element_type=F32)
        if offset is not None:
            row = lax.broadcasted_iota(jnp.int32, s.shape, 0) + offset
            col = lax.broadcasted_iota(jnp.int32, s.shape, 1)
            s = jnp.where(row >= col, s, NEG_INF)
        return s

    def fixed_step(start, plan):
        for c in range(2):
            for r, nk, offset in plan:
                rows = slice(r * sub, (r + 1) * sub)
                p = jnp.exp2(scores(start, c, r, nk, offset) - bound)
                l_sc[c, rows] += functools.reduce(
                    jnp.add, [p[:, i * LANES:(i + 1) * LANES] for i in range(nk // LANES)])
                acc_sc[c, rows] += jnp.dot(p.astype(v_ref.dtype), v_ref[pl.ds(start, nk), :],
                                           preferred_element_type=F32)

    def online_step(start, plan):
        for c in range(2):
            for r, nk, offset in plan:
                rows = slice(r * sub, (r + 1) * sub)
                s = scores(start, c, r, nk, offset)
                m_prev = m_sc[c, rows]
                m_new = jnp.maximum(m_prev, jnp.max(s, axis=1, keepdims=True))
                alpha = jnp.exp2(m_prev - m_new)
                p = jnp.exp2(s - jnp.tile(m_new, (1, nk // LANES)))
                l_sc[c, rows] = alpha * l_sc[c, rows] + jnp.sum(p, axis=1, keepdims=True)
                pv = jnp.dot(p.astype(v_ref.dtype), v_ref[pl.ds(start, nk), :], preferred_element_type=F32)
                acc_sc[c, rows] = jnp.tile(alpha, (1, acc_sc.shape[2] // LANES)) * acc_sc[c, rows] + pv
                m_sc[c, rows] = m_new

    def sweep(step):
        def body(j, carry):
            step(pl.multiple_of(j * tk, tk), row_plan(None))
            return carry

        lax.fori_loop(0, qi * per_tile, body, 0)
        for d in range(per_tile):
            step(pl.multiple_of((qi * per_tile + d) * tk, tk), row_plan(d))

    def finish(l0, l1):
        lp = lam_ref[...]
        lam = (jnp.exp(jnp.sum(lp[0:1] * lp[1:2], axis=1, keepdims=True))
               - jnp.exp(jnp.sum(lp[2:3] * lp[3:4], axis=1, keepdims=True)) + lam_init)
        o = acc_sc[0] / l0 - lam * (acc_sc[1] / l1)
        ms = jnp.mean(o * o, axis=-1, keepdims=True)
        o = o * lax.rsqrt(ms + NORM_EPS) * subln_ref[...] * (1.0 - lam_init)
        o_ref[...] = o.astype(o_ref.dtype)

    @pl.when(bound <= DIFF_FIXED_SHIFT_MAX)
    def _():
        sweep(fixed_step)
        finish(jnp.sum(l_sc[0], axis=1, keepdims=True), jnp.sum(l_sc[1], axis=1, keepdims=True))

    @pl.when(bound > DIFF_FIXED_SHIFT_MAX)
    def _():
        m_sc[...] = jnp.full(m_sc.shape, NEG_INF, F32)
        sweep(online_step)
        finish(l_sc[0][:, :1], l_sc[1][:, :1])


def _diff_attention(proj, batch, seq, q_blk0, k_blk0, v_blk0, score_bound, diff_lambda, subln, lam_init):
    tq = min(DIFF_TQ, seq)
    nq = seq // tq
    dv = DIFF_V_DIM
    return pl.pallas_call(
        functools.partial(_diff_attn_kernel, lam_init),
        grid_spec=pltpu.PrefetchScalarGridSpec(
            num_scalar_prefetch=1,
            grid=(batch, DIFF_HEADS, nq),
            in_specs=[
                pl.BlockSpec((tq, dv), lambda b, h, i, bound: (b * nq + i, q_blk0 + h)),
                pl.BlockSpec((seq, dv), lambda b, h, i, bound: (b, k_blk0 + h)),
                pl.BlockSpec((seq, dv), lambda b, h, i, bound: (b, v_blk0 + h)),
                pl.BlockSpec((4, HEAD), lambda b, h, i, bound: (0, 0)),
                pl.BlockSpec((1, dv), lambda b, h, i, bound: (0, 0)),
            ],
            out_specs=pl.BlockSpec((tq, dv), lambda b, h, i, bound: (b * nq + i, h)),
            scratch_shapes=[pltpu.VMEM((2, tq, LANES), F32), pltpu.VMEM((2, tq, LANES), F32),
                            pltpu.VMEM((2, tq, dv), F32)],
        ),
        out_shape=jax.ShapeDtypeStruct((batch * seq, DIFF_HEADS * dv), BF16),
        compiler_params=_params(("arbitrary", "arbitrary", "arbitrary")),
        name="diff_attention",
    )(score_bound, proj, proj, proj, diff_lambda, subln.reshape(1, dv))


def _dilated_multiplicity(tq):
    n_chunks = DIL_PAD // tq + 1
    q = np.arange(tq)[:, None]
    k = np.arange(tq)[None, :]
    out = np.zeros((n_chunks + 1, tq, tq), np.float32)
    for c in range(n_chunks):
        d = DIL_PAD - c * tq + q - k
        for window, dil in DIL_CONFIGS:
            out[c] += ((d >= 0) & (d <= window) & (d % dil == 0)).astype(np.float32)
    return out


def _dil_attn_kernel(bound_ref, q_ref, k_ref, v_ref, mult_ref, o_ref):
    qi = pl.program_id(2)
    tq = q_ref.shape[0]
    n_chunks = mult_ref.shape[0] - 1
    bound = bound_ref[0]
    q = q_ref[...]

    def chunk(c):
        blk = qi - (n_chunks - 1) + c
        start = pl.multiple_of(jnp.maximum(blk, 0) * tq, tq)
        w = mult_ref[jnp.where(blk >= 0, c, n_chunks)]
        s = lax.dot_general(q, k_ref[pl.ds(start, tq), :], (((1,), (1,)), ((), ())),
                            preferred_element_type=F32)
        return start, w, s

    def pv(p, start):
        return jnp.dot(p.astype(v_ref.dtype), v_ref[pl.ds(start, tq), :], preferred_element_type=F32)

    @pl.when(bound <= DIFF_FIXED_SHIFT_MAX)
    def _():
        den = jnp.zeros((tq, LANES), F32)
        acc = jnp.zeros(o_ref.shape, F32)
        for c in range(n_chunks):
            start, w, s = chunk(c)
            p = w * jnp.exp2(s - bound)
            den += functools.reduce(jnp.add, [p[:, i * LANES:(i + 1) * LANES] for i in range(tq // LANES)])
            acc += pv(p, start)
        o_ref[...] = (acc / jnp.sum(den, axis=1, keepdims=True)).astype(o_ref.dtype)

    @pl.when(bound > DIFF_FIXED_SHIFT_MAX)
    def _():
        chunks = [chunk(c) for c in range(n_chunks)]
        scores = [jnp.where(w > 0.0, s, NEG_INF) for _, w, s in chunks]
        m = jnp.max(functools.reduce(jnp.maximum, scores), axis=1, keepdims=True)
        den = jnp.zeros((tq, 1), F32)
        acc = jnp.zeros(o_ref.shape, F32)
        for (start, w, _), s in zip(chunks, scores):
            p = w * jnp.exp2(s - m)
            den += jnp.sum(p, axis=1, keepdims=True)
            acc += pv(p, start)
        o_ref[...] = (acc / den).astype(o_ref.dtype)


def _dilated_attention(proj, batch, seq, q_blk0, k_blk0, v_blk0, score_bound):
    tq = DIL_TQ
    nq = seq // tq
    mult = jnp.asarray(_dilated_multiplicity(tq))
    return pl.pallas_call(
        _dil_attn_kernel,
        grid_spec=pltpu.PrefetchScalarGridSpec(
            num_scalar_prefetch=1,
            grid=(batch, DIL_HEADS, nq),
            in_specs=[
                pl.BlockSpec((tq, HEAD), lambda b, h, i, bound: (b * nq + i, q_blk0 + h)),
                pl.BlockSpec((seq, HEAD), lambda b, h, i, bound: (b, k_blk0 + h)),
                pl.BlockSpec((seq, HEAD), lambda b, h, i, bound: (b, v_blk0 + h)),
                pl.BlockSpec(mult.shape, lambda b, h, i, bound: (0, 0, 0)),
            ],
            out_specs=pl.BlockSpec((tq, HEAD), lambda b, h, i, bound: (b * nq + i, h)),
        ),
        out_shape=jax.ShapeDtypeStruct((batch * seq, DIL_HEADS * HEAD), BF16),
        compiler_params=_params(("arbitrary", "arbitrary", "arbitrary")),
        name="dilated_attention",
    )(score_bound, proj, proj, proj, mult)


def _mem_kv_kernel(mem_ref, g_ref, w_ref, kn_ref, o_ref):
    x = mem_ref[...]
    ms = jnp.mean(x * x, axis=-1, keepdims=True)
    h = (x * lax.rsqrt(ms + NORM_EPS) * g_ref[...]).astype(BF16)
    kv = jnp.dot(h, w_ref[...], preferred_element_type=F32)
    half = kv.shape[1] // 2
    for c, y in enumerate(_chunk_norm(kv[:, :half], kn_ref[...], MEM_HEAD_DIM)):
        o_ref[:, c * MEM_HEAD_DIM:(c + 1) * MEM_HEAD_DIM] = y.astype(o_ref.dtype)
    o_ref[:, half:] = kv[:, half:].astype(o_ref.dtype)


def _mem_kv(mem2d, g_mem_norm, w_mem_kv_bf, k_gain_row, batch):
    rows, d = mem2d.shape
    mlen = rows // batch
    n = w_mem_kv_bf.shape[1]
    return pl.pallas_call(
        _mem_kv_kernel,
        grid=(batch,),
        in_specs=[
            pl.BlockSpec((mlen, d), lambda b: (b, 0)),
            pl.BlockSpec((1, d), lambda b: (0, 0)),
            pl.BlockSpec((d, n), lambda b: (0, 0)),
            pl.BlockSpec((1, n // 2), lambda b: (0, 0)),
        ],
        out_specs=pl.BlockSpec((mlen, n), lambda b: (b, 0)),
        out_shape=jax.ShapeDtypeStruct((rows, n), BF16),
        compiler_params=_params(("arbitrary",)),
        name="mem_kv",
    )(mem2d, g_mem_norm.reshape(1, d), w_mem_kv_bf, k_gain_row)


def _mem_attn_kernel(q_ref, kv_ref, o_ref):
    half = kv_ref.shape[1] // 2
    for h in range(MEM_HEADS):
        lo, hi = h * MEM_HEAD_DIM, (h + 1) * MEM_HEAD_DIM
        s = lax.dot_general(q_ref[:, lo:hi], kv_ref[:, lo:hi], (((1,), (1,)), ((), ())),
                            preferred_element_type=F32)
        e = jnp.exp(s - jnp.max(s, axis=1, keepdims=True))
        p = e / jnp.sum(e, axis=1, keepdims=True)
        o = jnp.dot(p.astype(kv_ref.dtype), kv_ref[:, half + lo:half + hi], preferred_element_type=F32)
        o_ref[:, lo:hi] = o.astype(o_ref.dtype)


def _memory_attention(proj, kv, batch, seq, q_blk):
    tm = min(ROW_TILE, seq)
    per_batch = seq // tm
    mlen = kv.shape[0] // batch
    width = MEM_HEADS * MEM_HEAD_DIM
    return pl.pallas_call(
        _mem_attn_kernel,
        grid=(batch * per_batch,),
        in_specs=[
            pl.BlockSpec((tm, width), lambda i: (i, q_blk)),
            pl.BlockSpec((mlen, 2 * width), lambda i: (i // per_batch, 0)),
        ],
        out_specs=pl.BlockSpec((tm, width), lambda i: (i, 0)),
        out_shape=jax.ShapeDtypeStruct((batch * seq, width), BF16),
        compiler_params=_params(("arbitrary",)),
        name="memory_attention",
    )(proj, kv)


def _merge_kernel(od_ref, ol_ref, om_ref, wd_ref, wl_ref, wm_ref, gd_ref, gl_ref, gm_ref, o_ref):
    acc = gd_ref[...].astype(F32) * jnp.dot(od_ref[...], wd_ref[...], preferred_element_type=F32)
    acc += gl_ref[...].astype(F32) * jnp.dot(ol_ref[...], wl_ref[...], preferred_element_type=F32)
    acc += gm_ref[...].astype(F32) * jnp.dot(om_ref[...], wm_ref[...], preferred_element_type=F32)
    o_ref[...] = acc.astype(o_ref.dtype)


def _branch_merge(o_diff, o_dil, o_mem, wd, wl, wm, proj, gate_tile0, d_model):
    t = o_diff.shape[0]
    tm = min(ROW_TILE, t)
    tn = PROJ_TN
    per_branch = d_model // tn
    row = lambda n, m: (m, 0)
    col = lambda n, m: (0, n)
    gate = lambda br: (lambda n, m: (m, gate_tile0 + br * per_branch + n))
    return pl.pallas_call(
        _merge_kernel,
        grid=(per_branch, t // tm),
        in_specs=[
            pl.BlockSpec((tm, o_diff.shape[1]), row),
            pl.BlockSpec((tm, o_dil.shape[1]), row),
            pl.BlockSpec((tm, o_mem.shape[1]), row),
            pl.BlockSpec((wd.shape[0], tn), col),
            pl.BlockSpec((wl.shape[0], tn), col),
            pl.BlockSpec((wm.shape[0], tn), col),
            pl.BlockSpec((tm, tn), gate(0)),
            pl.BlockSpec((tm, tn), gate(1)),
            pl.BlockSpec((tm, tn), gate(2)),
        ],
        out_specs=pl.BlockSpec((tm, tn), lambda n, m: (m, n)),
        out_shape=jax.ShapeDtypeStruct((t, d_model), BF16),
        compiler_params=_params(("arbitrary", "arbitrary")),
        name="branch_merge",
    )(o_diff, o_dil, o_mem, wd, wl, wm, proj, proj, proj)


def _split_bf16(x):
    hi = x.astype(BF16)
    return hi, (x - hi.astype(F32)).astype(BF16)


def _pack_bf16_pairs(x):
    n = x.shape[1] // 2
    bits = lax.bitcast_convert_type(x.astype(F32), jnp.uint32)
    return (bits[:, :n] >> 16) | (bits[:, n:] & jnp.uint32(0xFFFF0000))


def _unpack_bf16_pairs(u):
    lo = lax.bitcast_convert_type(u << 16, F32).astype(BF16)
    hi = lax.bitcast_convert_type(u & jnp.uint32(0xFFFF0000), F32).astype(BF16)
    return lo, hi


def _out_proj_kernel(mg_ref, w_ref, x_ref, g_ref, wr_hi_ref, wr_lo_ref, br_ref, x1_ref, h2_ref, lg_ref):
    x1 = x_ref[...] + jnp.dot(mg_ref[...], w_ref[...], preferred_element_type=F32)
    x1_ref[...] = x1
    ms = jnp.mean(x1 * x1, axis=-1, keepdims=True)
    h2 = x1 * lax.rsqrt(ms + NORM_EPS) * g_ref[...]
    hi, lo = _split_bf16(h2)
    h2_ref[...] = _pack_bf16_pairs(hi)
    lg = (jnp.dot(hi, wr_hi_ref[...], preferred_element_type=F32)
          + jnp.dot(lo, wr_hi_ref[...], preferred_element_type=F32)
          + jnp.dot(hi, wr_lo_ref[...], preferred_element_type=F32))
    lane = lax.broadcasted_iota(jnp.int32, lg.shape, 1)
    lg_ref[...] = jnp.where(lane < N_EXPERTS, lg + br_ref[...], -jnp.inf)


def _output_projection(merged, w_out_bf, x2d, g_ffn, wr_hi, wr_lo, b_router_row):
    t, d = x2d.shape
    tm = min(ROW_TILE, t)
    row = lambda i: (i, 0)
    fixed = lambda i: (0, 0)
    return pl.pallas_call(
        _out_proj_kernel,
        grid=(t // tm,),
        in_specs=[
            pl.BlockSpec((tm, d), row),
            pl.BlockSpec((d, d), fixed),
            pl.BlockSpec((tm, d), row),
            pl.BlockSpec((1, d), fixed),
            pl.BlockSpec((d, LANES), fixed),
            pl.BlockSpec((d, LANES), fixed),
            pl.BlockSpec((1, LANES), fixed),
        ],
        out_specs=[pl.BlockSpec((tm, d), row), pl.BlockSpec((tm, d // 2), row), pl.BlockSpec((tm, LANES), row)],
        out_shape=[jax.ShapeDtypeStruct((t, d), F32), jax.ShapeDtypeStruct((t, d // 2), jnp.uint32),
                   jax.ShapeDtypeStruct((t, LANES), F32)],
        compiler_params=_params(("arbitrary",)),
        name="output_projection",
    )(merged, w_out_bf, x2d, g_ffn.reshape(1, d), wr_hi, wr_lo, b_router_row)


def _route_kernel(lg_ref, idx_ref, gate_ref, rank_ref, cnt_ref, carry_sc):
    i = pl.program_id(0)

    @pl.when(i == 0)
    def _():
        carry_sc[...] = jnp.zeros(carry_sc.shape, F32)

    work = lg_ref[...]
    tm = work.shape[0]
    lane = lax.broadcasted_iota(jnp.int32, work.shape, 1).astype(F32)
    vals, idxs = [], []
    member = jnp.zeros(work.shape, F32)
    for _ in range(TOP_K):
        mx = jnp.max(work, axis=1, keepdims=True)
        idx = jnp.min(jnp.where(work == mx, lane, float(LANES)), axis=1, keepdims=True)
        sel = lane == idx
        vals.append(mx)
        idxs.append(idx)
        member = jnp.where(sel, 1.0, member)
        work = jnp.where(sel, -jnp.inf, work)
    exps = [jnp.exp(v - vals[0]) for v in vals]
    z = exps[0] + exps[1] + exps[2] + exps[3]
    r = lax.broadcasted_iota(jnp.int32, (tm, tm), 0)
    c = lax.broadcasted_iota(jnp.int32, (tm, tm), 1)
    before = jnp.where(c < r, 1.0, 0.0).astype(BF16)
    cum = jnp.dot(before, member.astype(BF16), preferred_element_type=F32) + carry_sc[...]
    idx_out = jnp.zeros(work.shape, F32)
    gate_out = jnp.zeros(work.shape, F32)
    rank_out = jnp.zeros(work.shape, F32)
    for k in range(TOP_K):
        rank = jnp.sum(jnp.where(lane == idxs[k], cum, 0.0), axis=1, keepdims=True)
        idx_out = jnp.where(lane == float(k), idxs[k], idx_out)
        gate_out = jnp.where(lane == float(k), exps[k] / z, gate_out)
        rank_out = jnp.where(lane == float(k), rank, rank_out)
    idx_ref[...] = idx_out.astype(jnp.int32)
    gate_ref[...] = gate_out
    rank_ref[...] = rank_out.astype(jnp.int32)
    carry_sc[...] += jnp.sum(member, axis=0, keepdims=True)
    cnt_ref[...] = carry_sc[...].astype(jnp.int32)


def _route(logits):
    t = logits.shape[0]
    tm = min(ROW_TILE, t)
    row = lambda i: (i, 0)
    return pl.pallas_call(
        _route_kernel,
        grid=(t // tm,),
        in_specs=[pl.BlockSpec((tm, LANES), row)],
        out_specs=[pl.BlockSpec((tm, LANES), row), pl.BlockSpec((tm, LANES), row),
                   pl.BlockSpec((tm, LANES), row), pl.BlockSpec((1, LANES), lambda i: (0, 0))],
        out_shape=[jax.ShapeDtypeStruct((t, LANES), jnp.int32), jax.ShapeDtypeStruct((t, LANES), F32),
                   jax.ShapeDtypeStruct((t, LANES), jnp.int32), jax.ShapeDtypeStruct((1, LANES), jnp.int32)],
        scratch_shapes=[pltpu.VMEM((1, LANES), F32)],
        compiler_params=_params(("arbitrary",)),
        name="route",
    )(logits)


def _row_copy(src_hbm, row, dst, dst_row, sem):
    return pltpu.make_async_copy(src_hbm.at[pl.ds(row, 1)], dst.at[pl.ds(dst_row, 1)], sem)


def _dispatch_kernel(dest_ref, h_ref, init_hbm, o_hbm, sem):
    del init_hbm
    i = pl.program_id(0)
    rows = h_ref.shape[0]

    def copy(t, slot):
        return pltpu.make_async_copy(h_ref.at[pl.ds(t, 1)], o_hbm.at[pl.ds(slot, 1)], sem)

    def start(t, carry):
        for k in range(TOP_K):
            copy(t, dest_ref[(i * rows + t) * TOP_K + k]).start()
        return carry

    def wait(t, carry):
        for k in range(TOP_K):
            copy(t, 0).wait()
        return carry

    lax.fori_loop(0, rows, start, 0)
    lax.fori_loop(0, rows, wait, 0)


def _dispatch(h2_packed, dest_flat, n_slots):
    t, w = h2_packed.shape
    rows = min(COMBINE_ROWS, t)
    return pl.pallas_call(
        _dispatch_kernel,
        grid_spec=pltpu.PrefetchScalarGridSpec(
            num_scalar_prefetch=1,
            grid=(t // rows,),
            in_specs=[pl.BlockSpec((rows, w), lambda i, dest: (i, 0)), pl.BlockSpec(memory_space=pl.ANY)],
            out_specs=pl.BlockSpec(memory_space=pl.ANY),
            scratch_shapes=[pltpu.SemaphoreType.DMA(())],
        ),
        out_shape=jax.ShapeDtypeStruct((n_slots, w), h2_packed.dtype),
        input_output_aliases={2: 0},
        compiler_params=_params(("arbitrary",)),
        name="dispatch",
    )(dest_flat, h2_packed, jnp.zeros((n_slots, w), h2_packed.dtype))


def _gate_up_kernel(be_ref, used_ref, x_ref, wg_ref, wu_ref, bg_ref, bu_ref, o_ref, wg_bf, wu_bf):
    m = pl.program_id(1)

    @pl.when(m < used_ref[0])
    def _():
        @pl.when((m == 0) | (be_ref[m] != be_ref[jnp.maximum(m - 1, 0)]))
        def _():
            wg_bf[...] = wg_ref[...].astype(BF16)
            wu_bf[...] = wu_ref[...].astype(BF16)

        for rows in _row_blocks(x_ref.shape[0]):
            lo, hi = _unpack_bf16_pairs(x_ref[rows, :])
            half = lo.shape[1]
            gate = (jnp.dot(lo, wg_bf[:half, :], preferred_element_type=F32)
                    + jnp.dot(hi, wg_bf[half:, :], preferred_element_type=F32) + bg_ref[...])
            up = (jnp.dot(lo, wu_bf[:half, :], preferred_element_type=F32)
                  + jnp.dot(hi, wu_bf[half:, :], preferred_element_type=F32) + bu_ref[...])
            gate = jnp.minimum(gate, SWIGLU_LIMIT)
            up = jnp.clip(up, -SWIGLU_LIMIT, SWIGLU_LIMIT)
            act = (up + 1.0) * gate * (1.0 / (1.0 + jnp.exp(-SWIGLU_ALPHA * gate)))
            o_ref[rows, :] = act.astype(o_ref.dtype)

    @pl.when(m >= used_ref[0])
    def _():
        o_ref[...] = jnp.zeros(o_ref.shape, o_ref.dtype)


def _gate_up(xs, w_gate_up, b_gate_up, block_e, n_used):
    p = xs.shape[0]
    d = w_gate_up.shape[1]
    d_ff = w_gate_up.shape[2] // 2
    rows, tf = EXPERT_ROWS, FF_TILE
    nf = d_ff // tf
    blk = lambda m, used: jnp.minimum(m, used[0] - 1)
    return pl.pallas_call(
        _gate_up_kernel,
        grid_spec=pltpu.PrefetchScalarGridSpec(
            num_scalar_prefetch=2,
            grid=(nf, p // rows),
            in_specs=[
                pl.BlockSpec((rows, xs.shape[1]), lambda f, m, be, used: (blk(m, used), 0)),
                pl.BlockSpec((None, d, tf), lambda f, m, be, used: (be[blk(m, used)], 0, f)),
                pl.BlockSpec((None, d, tf), lambda f, m, be, used: (be[blk(m, used)], 0, nf + f)),
                pl.BlockSpec((None, 1, tf), lambda f, m, be, used: (be[blk(m, used)], 0, f)),
                pl.BlockSpec((None, 1, tf), lambda f, m, be, used: (be[blk(m, used)], 0, nf + f)),
            ],
            out_specs=pl.BlockSpec((rows, tf), lambda f, m, be, used: (m, f)),
            scratch_shapes=[pltpu.VMEM((d, tf), BF16), pltpu.VMEM((d, tf), BF16)],
        ),
        out_shape=jax.ShapeDtypeStruct((p, d_ff), BF16),
        compiler_params=_params(("arbitrary", "arbitrary")),
        name="expert_gate_up",
    )(block_e, n_used, xs, w_gate_up, w_gate_up, b_gate_up, b_gate_up)


def _down_kernel(be_ref, used_ref, a_ref, w_ref, b_ref, o_ref, w_bf):
    m = pl.program_id(1)

    @pl.when(m < used_ref[0])
    def _():
        @pl.when((m == 0) | (be_ref[m] != be_ref[jnp.maximum(m - 1, 0)]))
        def _():
            w_bf[...] = w_ref[...].astype(BF16)

        for rows in _row_blocks(a_ref.shape[0]):
            o_ref[rows, :] = jnp.dot(a_ref[rows, :], w_bf[...], preferred_element_type=F32) + b_ref[...]

    @pl.when(m >= used_ref[0])
    def _():
        o_ref[...] = jnp.zeros(o_ref.shape, o_ref.dtype)


def _down(act, w_down, b_down, block_e, n_used):
    p, d_ff = act.shape
    d = w_down.shape[2]
    rows, tn = EXPERT_ROWS, DOWN_TN
    blk = lambda m, used: jnp.minimum(m, used[0] - 1)
    return pl.pallas_call(
        _down_kernel,
        grid_spec=pltpu.PrefetchScalarGridSpec(
            num_scalar_prefetch=2,
            grid=(d // tn, p // rows),
            in_specs=[
                pl.BlockSpec((rows, d_ff), lambda n, m, be, used: (blk(m, used), 0)),
                pl.BlockSpec((None, d_ff, tn), lambda n, m, be, used: (be[blk(m, used)], 0, n)),
                pl.BlockSpec((None, 1, tn), lambda n, m, be, used: (be[blk(m, used)], 0, n)),
            ],
            out_specs=pl.BlockSpec((rows, tn), lambda n, m, be, used: (m, n)),
            scratch_shapes=[pltpu.VMEM((d_ff, tn), BF16)],
        ),
        out_shape=jax.ShapeDtypeStruct((p, d), F32),
        compiler_params=_params(("arbitrary", "arbitrary")),
        name="expert_down",
    )(block_e, n_used, act, w_down, b_down)


def _combine_kernel(dest_ref, x_ref, g_ref, y_hbm, o_ref, buf, sem):
    i = pl.program_id(0)
    rows = x_ref.shape[0]

    def start(t, carry):
        for k in range(TOP_K):
            _row_copy(y_hbm, dest_ref[(i * rows + t) * TOP_K + k], buf.at[k], t, sem).start()
        return carry

    def wait(t, carry):
        for k in range(TOP_K):
            _row_copy(y_hbm, 0, buf.at[k], t, sem).wait()
        return carry

    lax.fori_loop(0, rows, start, 0)
    lax.fori_loop(0, rows, wait, 0)
    g = g_ref[...]
    moe = ((g[:, 0:1] * buf[0] + g[:, 1:2] * buf[1]) + (g[:, 2:3] * buf[2] + g[:, 3:4] * buf[3]))
    o_ref[...] = x_ref[...] + moe


def _combine(x1, gates_padded, y, dest_flat):
    t, d = x1.shape
    rows = min(COMBINE_ROWS, t)
    return pl.pallas_call(
        _combine_kernel,
        grid_spec=pltpu.PrefetchScalarGridSpec(
            num_scalar_prefetch=1,
            grid=(t // rows,),
            in_specs=[pl.BlockSpec((rows, d), lambda i, dest: (i, 0)),
                      pl.BlockSpec((rows, LANES), lambda i, dest: (i, 0)),
                      pl.BlockSpec(memory_space=pl.ANY)],
            out_specs=pl.BlockSpec((rows, d), lambda i, dest: (i, 0)),
            scratch_shapes=[pltpu.VMEM((TOP_K, rows, d), F32), pltpu.SemaphoreType.DMA(())],
        ),
        out_shape=jax.ShapeDtypeStruct((t, d), F32),
        compiler_params=_params(("arbitrary",)),
        name="combine",
    )(dest_flat, x1, gates_padded, y)


def _moe(x1, h2, logits, w_gate_up, b_gate_up, w_down, b_down):
    t, d = x1.shape
    n_exp = w_gate_up.shape[0]
    rows = EXPERT_ROWS
    idx_p, gate_p, rank_p, cnt_p = _route(logits)
    idx, rank = idx_p[:, :TOP_K], rank_p[:, :TOP_K]
    counts = cnt_p[0, :n_exp]

    padded = (counts + rows - 1) // rows * rows
    pend = jnp.cumsum(padded)
    pstart = pend - padded
    n_blocks = (t * TOP_K + n_exp * (rows - 1) + rows - 1) // rows
    dest_flat = (pstart[idx] + rank).astype(jnp.int32).reshape(-1)
    block_start = jnp.arange(n_blocks, dtype=jnp.int32) * rows
    block_e = jnp.minimum(jnp.sum(pend[None, :] <= block_start[:, None], axis=1), n_exp - 1).astype(jnp.int32)
    n_used = (pend[-1] // rows).astype(jnp.int32).reshape(1)

    xs = _dispatch(h2, dest_flat, n_blocks * rows)
    act = _gate_up(xs, w_gate_up, b_gate_up.reshape(n_exp, 1, -1), block_e, n_used)
    y = _down(act, w_down, b_down.reshape(n_exp, 1, -1), block_e, n_used)
    return _combine(x1, gate_p, y, dest_flat)


def _rope_tables(positions):
    half = ROT_DIM // 2
    inv_freq = ROPE_THETA ** (-jnp.arange(0, ROT_DIM, 2, dtype=F32) / ROT_DIM)
    ang = positions.reshape(-1).astype(F32)[:, None] * inv_freq
    cos, sin = jnp.cos(ang), jnp.sin(ang)
    t = ang.shape[0]
    cos_t = jnp.concatenate([cos, cos, jnp.ones((t, HEAD - ROT_DIM), F32)], axis=1)
    sa_t = jnp.concatenate([-sin, jnp.zeros((t, HEAD - half), F32)], axis=1)
    sb_t = jnp.concatenate([jnp.zeros((t, half), F32), sin, jnp.zeros((t, HEAD - ROT_DIM), F32)], axis=1)
    return cos_t, sa_t, sb_t


def _layer(x2d, mem2d, batch, seq, cos_t, sa_t, sb_t, layer_idx, g_mix_norm, w_in, b_gate, diff_q_norm,
           diff_k_norm, diff_lambda, diff_subln, dil_q_norm, dil_k_norm, g_mem_norm, w_mem_kv, mem_q_norm,
           mem_k_norm, w_branch_diff, w_branch_dil, w_branch_mem, w_out, g_ffn_norm, w_router, b_router,
           w_gate_up, b_gate_up, w_down, b_down):
    d = x2d.shape[1]
    diff_cols = DIFF_HEADS * 2 * HEAD
    dil_cols = DIL_HEADS * HEAD
    mem_cols = MEM_HEADS * MEM_HEAD_DIM
    off_dq, off_dk, off_dv = 0, diff_cols, 2 * diff_cols
    off_lq = 3 * diff_cols
    off_lk, off_lv = off_lq + dil_cols, off_lq + 2 * dil_cols
    off_mq = off_lq + 3 * dil_cols
    off_gate = off_mq + mem_cols
    n_cols = off_gate + 3 * d
    assert w_in.shape == (d, n_cols)
    tn = PROJ_TN
    tiles = lambda off, width: tuple(range(off // tn, (off + width) // tn))
    rope_tiles = (tiles(off_dq, diff_cols) + tiles(off_dk, diff_cols) + tiles(off_lq, dil_cols)
                  + tiles(off_lk, dil_cols))

    rep = lambda g, width: jnp.tile(g.astype(F32), width // g.shape[0])
    diff_q_gain = diff_q_norm.astype(F32) * (HEAD ** -0.5 * LOG2E)
    dil_q_gain = dil_q_norm.astype(F32) * (HEAD ** -0.5 * LOG2E)

    def score_bound(q_gain, k_gain):
        return (1.01 * HEAD * jnp.max(jnp.abs(q_gain)) * jnp.max(jnp.abs(k_gain.astype(F32)))).reshape(1)

    diff_bound = score_bound(diff_q_gain, diff_k_norm)
    dil_bound = score_bound(dil_q_gain, dil_k_norm)
    gain_row = jnp.concatenate([
        rep(diff_q_gain, diff_cols), rep(diff_k_norm, diff_cols), jnp.ones((diff_cols,), F32),
        rep(dil_q_gain, dil_cols), rep(dil_k_norm, dil_cols), jnp.ones((dil_cols,), F32),
        rep(mem_q_norm, mem_cols) * MEM_HEAD_DIM ** -0.5, jnp.ones((3 * d,), F32)]).reshape(1, n_cols)
    bias_row = jnp.concatenate([jnp.zeros((off_gate,), F32), b_gate.astype(F32)]).reshape(1, n_cols)

    h = _rmsnorm(x2d, g_mix_norm, BF16)
    proj = _input_projection(h, w_in, gain_row, bias_row, cos_t, sa_t, sb_t, rope_tiles, off_mq // tn,
                             off_gate // tn)

    lam_init = 0.8 - 0.6 * float(np.exp(-0.3 * layer_idx))
    o_diff = _diff_attention(proj, batch, seq, off_dq // DIFF_V_DIM, off_dk // DIFF_V_DIM, off_dv // DIFF_V_DIM,
                             diff_bound, diff_lambda.astype(F32), diff_subln, lam_init)
    o_dil = _dilated_attention(proj, batch, seq, off_lq // HEAD, off_lk // HEAD, off_lv // HEAD, dil_bound)
    kv = _mem_kv(mem2d, g_mem_norm, w_mem_kv.astype(BF16), rep(mem_k_norm, mem_cols).reshape(1, mem_cols), batch)
    o_mem = _memory_attention(proj, kv, batch, seq, off_mq // mem_cols)

    merged = _branch_merge(o_diff, o_dil, o_mem, w_branch_diff.astype(BF16), w_branch_dil.astype(BF16),
                           w_branch_mem.astype(BF16), proj, off_gate // tn, d)

    n_exp = w_router.shape[1]
    wr = jnp.zeros((d, LANES), F32).at[:, :n_exp].set(w_router)
    wr_hi, wr_lo = _split_bf16(wr)
    br_row = jnp.zeros((1, LANES), F32).at[0, :n_exp].set(b_router)
    x1, h2, logits = _output_projection(merged, w_out.astype(BF16), x2d, g_ffn_norm, wr_hi, wr_lo, br_row)
    return _moe(x1, h2, logits, w_gate_up, b_gate_up, w_down, b_down)


def kernel(x, mem, positions, g_mix_norm, w_in, b_gate, diff_q_norm, diff_k_norm, diff_lambda, diff_subln,
           dil_q_norm, dil_k_norm, g_mem_norm, w_mem_kv, mem_q_norm, mem_k_norm, w_branch_diff, w_branch_dil,
           w_branch_mem, w_out, g_ffn_norm, w_router, b_router, w_gate_up, b_gate_up, w_down, b_down):
    batch, seq, d = x.shape
    cos_t, sa_t, sb_t = _rope_tables(positions)
    x2d = x.reshape(batch * seq, d)
    mem2d = mem.reshape(-1, d)
    for l in range(g_mix_norm.shape[0]):
        x2d = _layer(x2d, mem2d, batch, seq, cos_t, sa_t, sb_t, l, g_mix_norm[l], w_in[l], b_gate[l],
                     diff_q_norm[l], diff_k_norm[l], diff_lambda[l], diff_subln[l], dil_q_norm[l],
                     dil_k_norm[l], g_mem_norm[l], w_mem_kv[l], mem_q_norm[l], mem_k_norm[l],
                     w_branch_diff[l], w_branch_dil[l], w_branch_mem[l], w_out[l], g_ffn_norm[l],
                     w_router[l], b_router[l], w_gate_up[l], b_gate_up[l], w_down[l], b_down[l])
    return x2d.reshape(batch, seq, d)
```

```python
import functools

import numpy as np
import jax
import jax.numpy as jnp
from jax import lax
from jax.experimental import pallas as pl
from jax.experimental.pallas import tpu as pltpu

F32 = jnp.float32
BF16 = jnp.bfloat16

NORM_EPS = 1e-6
NEG_INF = -1e30
ROPE_THETA = 500000.0
ROT_DIM = 32
HEAD = 128
DIFF_HEADS = 8
DIFF_V_DIM = 256
DIL_HEADS = 8
DIL_CONFIGS = ((128, 1), (512, 4), (2048, 16))
DIL_PAD = 2048
MEM_HEADS = 4
MEM_HEAD_DIM = 256
N_EXPERTS = 32
TOP_K = 4
SWIGLU_LIMIT = 7.0
SWIGLU_ALPHA = 1.702

LANES = 128
VMEM_LIMIT = 56 * 1024 * 1024

PROJ_TN = 1024
ROW_TILE = 512
DIFF_TQ = 1024
DIFF_TK = 1024
DIFF_SUB = 256
DIFF_FIXED_SHIFT_MAX = 40.0
LOG2E = 1.4426950408889634
DIL_TQ = 256
EXPERT_ROWS = 512
FF_TILE = 1024
DOWN_TN = 1024
COMBINE_ROWS = 256
DMA_LOOP_UNROLL = 8
MATMUL_ROW_BLOCK = 256
PROJ_TM = 1024


def _params(sem, **kw):
    return pltpu.CompilerParams(dimension_semantics=sem, vmem_limit_bytes=VMEM_LIMIT, **kw)


def _rmsnorm_kernel(x_ref, g_ref, o_ref):
    x = x_ref[...]
    ms = jnp.mean(x * x, axis=-1, keepdims=True)
    o_ref[...] = (x * lax.rsqrt(ms + NORM_EPS) * g_ref[...]).astype(o_ref.dtype)


def _rmsnorm(x, g, out_dtype):
    t, d = x.shape
    tm = min(ROW_TILE, t)
    return pl.pallas_call(
        _rmsnorm_kernel,
        grid=(t // tm,),
        in_specs=[pl.BlockSpec((tm, d), lambda i: (i, 0)), pl.BlockSpec((1, d), lambda i: (0, 0))],
        out_specs=pl.BlockSpec((tm, d), lambda i: (i, 0)),
        out_shape=jax.ShapeDtypeStruct((t, d), out_dtype),
        compiler_params=_params(("arbitrary",)),
        name="rmsnorm",
    )(x, g.reshape(1, d))


def _row_blocks(rows):
    size = min(MATMUL_ROW_BLOCK, rows)
    return [slice(r, r + size) for r in range(0, rows, size)]


def _chunk_norm(x, gain, width):
    outs = []
    for c in range(x.shape[1] // width):
        xc = x[:, c * width:(c + 1) * width]
        ms = jnp.mean(xc * xc, axis=-1, keepdims=True)
        outs.append(xc * lax.rsqrt(ms + NORM_EPS) * gain[:, c * width:(c + 1) * width])
    return outs


def _proj_kernel(rope_tiles, mem_tile, gate_tile0, h_ref, w_ref, gain_ref, bias_ref, cos_ref, sa_ref,
                 sb_ref, o_ref, wbf_ref):
    n = pl.program_id(0)
    m = pl.program_id(1)

    @pl.when(m == 0)
    def _():
        wbf_ref[...] = w_ref[...].astype(BF16)

    is_rope = functools.reduce(jnp.logical_or, [n == t for t in rope_tiles])
    is_mem = n == mem_tile
    is_gate = n >= gate_tile0

    def for_row_blocks(epilogue):
        for rows in _row_blocks(h_ref.shape[0]):
            epilogue(rows, jnp.dot(h_ref[rows, :], wbf_ref[...], preferred_element_type=F32))

    def rope(rows, acc):
        cos, sa, sb = cos_ref[rows, :], sa_ref[rows, :], sb_ref[rows, :]
        for c, y in enumerate(_chunk_norm(acc, gain_ref[...], HEAD)):
            y = y * cos + pltpu.roll(y, HEAD - ROT_DIM // 2, 1) * sa + pltpu.roll(y, ROT_DIM // 2, 1) * sb
            o_ref[rows, c * HEAD:(c + 1) * HEAD] = y.astype(o_ref.dtype)

    def mem_q(rows, acc):
        for c, y in enumerate(_chunk_norm(acc, gain_ref[...], MEM_HEAD_DIM)):
            o_ref[rows, c * MEM_HEAD_DIM:(c + 1) * MEM_HEAD_DIM] = y.astype(o_ref.dtype)

    def gate(rows, acc):
        z = acc + bias_ref[...]
        o_ref[rows, :] = (1.0 / (1.0 + jnp.exp(-z))).astype(o_ref.dtype)

    def plain(rows, acc):
        o_ref[rows, :] = acc.astype(o_ref.dtype)

    pl.when(is_rope)(lambda: for_row_blocks(rope))
    pl.when(is_mem)(lambda: for_row_blocks(mem_q))
    pl.when(is_gate)(lambda: for_row_blocks(gate))
    pl.when(jnp.logical_not(is_rope | is_mem | is_gate))(lambda: for_row_blocks(plain))


def _input_projection(h, w_in, gain_row, bias_row, cos_t, sa_t, sb_t, rope_tiles, mem_tile, gate_tile0):
    t, d = h.shape
    n_cols = w_in.shape[1]
    tm = min(PROJ_TM, t)
    tn = PROJ_TN
    row = lambda n, m: (m, 0)
    col = lambda n, m: (0, n)
    return pl.pallas_call(
        functools.partial(_proj_kernel, rope_tiles, mem_tile, gate_tile0),
        grid=(n_cols // tn, t // tm),
        in_specs=[
            pl.BlockSpec((tm, d), row),
            pl.BlockSpec((d, tn), col),
            pl.BlockSpec((1, tn), col),
            pl.BlockSpec((1, tn), col),
            pl.BlockSpec((tm, HEAD), row),
            pl.BlockSpec((tm, HEAD), row),
            pl.BlockSpec((tm, HEAD), row),
        ],
        out_specs=pl.BlockSpec((tm, tn), lambda n, m: (m, n)),
        out_shape=jax.ShapeDtypeStruct((t, n_cols), BF16),
        scratch_shapes=[pltpu.VMEM((d, tn), BF16)],
        compiler_params=_params(("arbitrary", "arbitrary")),
        name="input_projection",
    )(h, w_in, gain_row, bias_row, cos_t, sa_t, sb_t)


def _diff_attn_kernel(lam_init, bound_ref, q_ref, k_ref, v_ref, lam_ref, subln_ref, o_ref, m_sc, l_sc, acc_sc):
    qi = pl.program_id(2)
    tq = q_ref.shape[0]
    sub = min(DIFF_SUB, tq)
    tk = min(DIFF_TK, tq)
    per_tile = tq // tk
    bound = bound_ref[0]
    l_sc[...] = jnp.zeros(l_sc.shape, F32)
    acc_sc[...] = jnp.zeros(acc_sc.shape, F32)

    def row_plan(d):
        plan = []
        for r in range(tq // sub):
            if d is None or r * sub >= (d + 1) * tk:
                plan.append((r, tk, None))
            elif (r + 1) * sub > d * tk:
                plan.append((r, min(tk, (r + 1) * sub - d * tk), r * sub - d * tk))
        return plan

    def scores(start, c, r, nk, offset):
        s = lax.dot_general(q_ref[r * sub:(r + 1) * sub, c * HEAD:(c + 1) * HEAD],
                            k_ref[pl.ds(start, nk), c * HEAD:(c + 1) * HEAD],
                            (((1,), (1,)), ((), ())), preferred_element_type=F32)
        if offset is not None:
            row = lax.broadcasted_iota(jnp.int32, s.shape, 0) + offset
            col = lax.broadcasted_iota(jnp.int32, s.shape, 1)
            s = jnp.where(row >= col, s, NEG_INF)
        return s

    def fixed_step(start, plan):
        for c in range(2):
            for r, nk, offset in plan:
                rows = slice(r * sub, (r + 1) * sub)
                p = jnp.exp2(scores(start, c, r, nk, offset) - bound)
                l_sc[c, rows] += functools.reduce(
                    jnp.add, [p[:, i * LANES:(i + 1) * LANES] for i in range(nk // LANES)])
                acc_sc[c, rows] += jnp.dot(p.astype(v_ref.dtype), v_ref[pl.ds(start, nk), :],
                                           preferred_element_type=F32)

    def online_step(start, plan):
        for c in range(2):
            for r, nk, offset in plan:
                rows = slice(r * sub, (r + 1) * sub)
                s = scores(start, c, r, nk, offset)
                m_prev = m_sc[c, rows]
                m_new = jnp.maximum(m_prev, jnp.max(s, axis=1, keepdims=True))
                alpha = jnp.exp2(m_prev - m_new)
                p = jnp.exp2(s - jnp.tile(m_new, (1, nk // LANES)))
                l_sc[c, rows] = alpha * l_sc[c, rows] + jnp.sum(p, axis=1, keepdims=True)
                pv = jnp.dot(p.astype(v_ref.dtype), v_ref[pl.ds(start, nk), :], preferred_element_type=F32)
                acc_sc[c, rows] = jnp.tile(alpha, (1, acc_sc.shape[2] // LANES)) * acc_sc[c, rows] + pv
                m_sc[c, rows] = m_new

    def sweep(step):
        def body(j, carry):
            step(pl.multiple_of(j * tk, tk), row_plan(None))
            return carry

        lax.fori_loop(0, qi * per_tile, body, 0)
        for d in range(per_tile):
            step(pl.multiple_of((qi * per_tile + d) * tk, tk), row_plan(d))

    def finish(l0, l1):
        lp = lam_ref[...]
        lam = (jnp.exp(jnp.sum(lp[0:1] * lp[1:2], axis=1, keepdims=True))
               - jnp.exp(jnp.sum(lp[2:3] * lp[3:4], axis=1, keepdims=True)) + lam_init)
        o = acc_sc[0] / l0 - lam * (acc_sc[1] / l1)
        ms = jnp.mean(o * o, axis=-1, keepdims=True)
        o = o * lax.rsqrt(ms + NORM_EPS) * subln_ref[...] * (1.0 - lam_init)
        o_ref[...] = o.astype(o_ref.dtype)

    @pl.when(bound <= DIFF_FIXED_SHIFT_MAX)
    def _():
        sweep(fixed_step)
        finish(jnp.sum(l_sc[0], axis=1, keepdims=True), jnp.sum(l_sc[1], axis=1, keepdims=True))

    @pl.when(bound > DIFF_FIXED_SHIFT_MAX)
    def _():
        m_sc[...] = jnp.full(m_sc.shape, NEG_INF, F32)
        sweep(online_step)
        finish(l_sc[0][:, :1], l_sc[1][:, :1])


def _diff_attention(proj, batch, seq, q_blk0, k_blk0, v_blk0, score_bound, diff_lambda, subln, lam_init):
    tq = min(DIFF_TQ, seq)
    nq = seq // tq
    dv = DIFF_V_DIM
    return pl.pallas_call(
        functools.partial(_diff_attn_kernel, lam_init),
        grid_spec=pltpu.PrefetchScalarGridSpec(
            num_scalar_prefetch=1,
            grid=(batch, DIFF_HEADS, nq),
            in_specs=[
                pl.BlockSpec((tq, dv), lambda b, h, i, bound: (b * nq + i, q_blk0 + h)),
                pl.BlockSpec((seq, dv), lambda b, h, i, bound: (b, k_blk0 + h)),
                pl.BlockSpec((seq, dv), lambda b, h, i, bound: (b, v_blk0 + h)),
                pl.BlockSpec((4, HEAD), lambda b, h, i, bound: (0, 0)),
                pl.BlockSpec((1, dv), lambda b, h, i, bound: (0, 0)),
            ],
            out_specs=pl.BlockSpec((tq, dv), lambda b, h, i, bound: (b * nq + i, h)),
            scratch_shapes=[pltpu.VMEM((2, tq, LANES), F32), pltpu.VMEM((2, tq, LANES), F32),
                            pltpu.VMEM((2, tq, dv), F32)],
        ),
        out_shape=jax.ShapeDtypeStruct((batch * seq, DIFF_HEADS * dv), BF16),
        compiler_params=_params(("arbitrary", "arbitrary", "arbitrary")),
        name="diff_attention",
    )(score_bound, proj, proj, proj, diff_lambda, subln.reshape(1, dv))


def _dilated_multiplicity(tq):
    n_chunks = DIL_PAD // tq + 1
    q = np.arange(tq)[:, None]
    k = np.arange(tq)[None, :]
    out = np.zeros((n_chunks + 1, tq, tq), np.float32)
    for c in range(n_chunks):
        d = DIL_PAD - c * tq + q - k
        for window, dil in DIL_CONFIGS:
            out[c] += ((d >= 0) & (d <= window) & (d % dil == 0)).astype(np.float32)
    return out


def _dil_attn_kernel(bound_ref, q_ref, k_ref, v_ref, mult_ref, o_ref):
    qi = pl.program_id(2)
    tq = q_ref.shape[0]
    n_chunks = mult_ref.shape[0] - 1
    bound = bound_ref[0]
    q = q_ref[...]

    def chunk(c):
        blk = qi - (n_chunks - 1) + c
        start = pl.multiple_of(jnp.maximum(blk, 0) * tq, tq)
        w = mult_ref[jnp.where(blk >= 0, c, n_chunks)]
        s = lax.dot_general(q, k_ref[pl.ds(start, tq), :], (((1,), (1,)), ((), ())),
                            preferred_element_type=F32)
        return start, w, s

    def pv(p, start):
        return jnp.dot(p.astype(v_ref.dtype), v_ref[pl.ds(start, tq), :], preferred_element_type=F32)

    @pl.when(bound <= DIFF_FIXED_SHIFT_MAX)
    def _():
        den = jnp.zeros((tq, LANES), F32)
        acc = jnp.zeros(o_ref.shape, F32)
        for c in range(n_chunks):
            start, w, s = chunk(c)
            p = w * jnp.exp2(s - bound)
            den += functools.reduce(jnp.add, [p[:, i * LANES:(i + 1) * LANES] for i in range(tq // LANES)])
            acc += pv(p, start)
        o_ref[...] = (acc / jnp.sum(den, axis=1, keepdims=True)).astype(o_ref.dtype)

    @pl.when(bound > DIFF_FIXED_SHIFT_MAX)
    def _():
        chunks = [chunk(c) for c in range(n_chunks)]
        scores = [jnp.where(w > 0.0, s, NEG_INF) for _, w, s in chunks]
        m = jnp.max(functools.reduce(jnp.maximum, scores), axis=1, keepdims=True)
        den = jnp.zeros((tq, 1), F32)
        acc = jnp.zeros(o_ref.shape, F32)
        for (start, w, _), s in zip(chunks, scores):
            p = w * jnp.exp2(s - m)
            den += jnp.sum(p, axis=1, keepdims=True)
            acc += pv(p, start)
        o_ref[...] = (acc / den).astype(o_ref.dtype)


def _dilated_attention(proj, batch, seq, q_blk0, k_blk0, v_blk0, score_bound):
    tq = DIL_TQ
    nq = seq // tq
    mult = jnp.asarray(_dilated_multiplicity(tq))
    return pl.pallas_call(
        _dil_attn_kernel,
        grid_spec=pltpu.PrefetchScalarGridSpec(
            num_scalar_prefetch=1,
            grid=(batch, DIL_HEADS, nq),
            in_specs=[
                pl.BlockSpec((tq, HEAD), lambda b, h, i, bound: (b * nq + i, q_blk0 + h)),
                pl.BlockSpec((seq, HEAD), lambda b, h, i, bound: (b, k_blk0 + h)),
                pl.BlockSpec((seq, HEAD), lambda b, h, i, bound: (b, v_blk0 + h)),
                pl.BlockSpec(mult.shape, lambda b, h, i, bound: (0, 0, 0)),
            ],
            out_specs=pl.BlockSpec((tq, HEAD), lambda b, h, i, bound: (b * nq + i, h)),
        ),
        out_shape=jax.ShapeDtypeStruct((batch * seq, DIL_HEADS * HEAD), BF16),
        compiler_params=_params(("arbitrary", "arbitrary", "arbitrary")),
        name="dilated_attention",
    )(score_bound, proj, proj, proj, mult)


def _mem_kv_kernel(mem_ref, g_ref, w_ref, kn_ref, o_ref):
    x = mem_ref[...]
    ms = jnp.mean(x * x, axis=-1, keepdims=True)
    h = (x * lax.rsqrt(ms + NORM_EPS) * g_ref[...]).astype(BF16)
    kv = jnp.dot(h, w_ref[...], preferred_element_type=F32)
    half = kv.shape[1] // 2
    for c, y in enumerate(_chunk_norm(kv[:, :half], kn_ref[...], MEM_HEAD_DIM)):
        o_ref[:, c * MEM_HEAD_DIM:(c + 1) * MEM_HEAD_DIM] = y.astype(o_ref.dtype)
    o_ref[:, half:] = kv[:, half:].astype(o_ref.dtype)


def _mem_kv(mem2d, g_mem_norm, w_mem_kv_bf, k_gain_row, batch):
    rows, d = mem2d.shape
    mlen = rows // batch
    n = w_mem_kv_bf.shape[1]
    return pl.pallas_call(
        _mem_kv_kernel,
        grid=(batch,),
        in_specs=[
            pl.BlockSpec((mlen, d), lambda b: (b, 0)),
            pl.BlockSpec((1, d), lambda b: (0, 0)),
            pl.BlockSpec((d, n), lambda b: (0, 0)),
            pl.BlockSpec((1, n // 2), lambda b: (0, 0)),
        ],
        out_specs=pl.BlockSpec((mlen, n), lambda b: (b, 0)),
        out_shape=jax.ShapeDtypeStruct((rows, n), BF16),
        compiler_params=_params(("arbitrary",)),
        name="mem_kv",
    )(mem2d, g_mem_norm.reshape(1, d), w_mem_kv_bf, k_gain_row)


def _mem_attn_kernel(q_ref, kv_ref, o_ref):
    half = kv_ref.shape[1] // 2
    for h in range(MEM_HEADS):
        lo, hi = h * MEM_HEAD_DIM, (h + 1) * MEM_HEAD_DIM
        s = lax.dot_general(q_ref[:, lo:hi], kv_ref[:, lo:hi], (((1,), (1,)), ((), ())),
                            preferred_element_type=F32)
        e = jnp.exp(s - jnp.max(s, axis=1, keepdims=True))
        p = e / jnp.sum(e, axis=1, keepdims=True)
        o = jnp.dot(p.astype(kv_ref.dtype), kv_ref[:, half + lo:half + hi], preferred_element_type=F32)
        o_ref[:, lo:hi] = o.astype(o_ref.dtype)


def _memory_attention(proj, kv, batch, seq, q_blk):
    tm = min(ROW_TILE, seq)
    per_batch = seq // tm
    mlen = kv.shape[0] // batch
    width = MEM_HEADS * MEM_HEAD_DIM
    return pl.pallas_call(
        _mem_attn_kernel,
        grid=(batch * per_batch,),
        in_specs=[
            pl.BlockSpec((tm, width), lambda i: (i, q_blk)),
            pl.BlockSpec((mlen, 2 * width), lambda i: (i // per_batch, 0)),
        ],
        out_specs=pl.BlockSpec((tm, width), lambda i: (i, 0)),
        out_shape=jax.ShapeDtypeStruct((batch * seq, width), BF16),
        compiler_params=_params(("arbitrary",)),
        name="memory_attention",
    )(proj, kv)


def _merge_kernel(od_ref, ol_ref, om_ref, wd_ref, wl_ref, wm_ref, gd_ref, gl_ref, gm_ref, o_ref):
    acc = gd_ref[...].astype(F32) * jnp.dot(od_ref[...], wd_ref[...], preferred_element_type=F32)
    acc += gl_ref[...].astype(F32) * jnp.dot(ol_ref[...], wl_ref[...], preferred_element_type=F32)
    acc += gm_ref[...].astype(F32) * jnp.dot(om_ref[...], wm_ref[...], preferred_element_type=F32)
    o_ref[...] = acc.astype(o_ref.dtype)


def _branch_merge(o_diff, o_dil, o_mem, wd, wl, wm, proj, gate_tile0, d_model):
    t = o_diff.shape[0]
    tm = min(ROW_TILE, t)
    tn = PROJ_TN
    per_branch = d_model // tn
    row = lambda n, m: (m, 0)
    col = lambda n, m: (0, n)
    gate = lambda br: (lambda n, m: (m, gate_tile0 + br * per_branch + n))
    return pl.pallas_call(
        _merge_kernel,
        grid=(per_branch, t // tm),
        in_specs=[
            pl.BlockSpec((tm, o_diff.shape[1]), row),
            pl.BlockSpec((tm, o_dil.shape[1]), row),
            pl.BlockSpec((tm, o_mem.shape[1]), row),
            pl.BlockSpec((wd.shape[0], tn), col),
            pl.BlockSpec((wl.shape[0], tn), col),
            pl.BlockSpec((wm.shape[0], tn), col),
            pl.BlockSpec((tm, tn), gate(0)),
            pl.BlockSpec((tm, tn), gate(1)),
            pl.BlockSpec((tm, tn), gate(2)),
        ],
        out_specs=pl.BlockSpec((tm, tn), lambda n, m: (m, n)),
        out_shape=jax.ShapeDtypeStruct((t, d_model), BF16),
        compiler_params=_params(("arbitrary", "arbitrary")),
        name="branch_merge",
    )(o_diff, o_dil, o_mem, wd, wl, wm, proj, proj, proj)


def _split_bf16(x):
    hi = x.astype(BF16)
    return hi, (x - hi.astype(F32)).astype(BF16)


def _pack_bf16_pairs(x):
    n = x.shape[1] // 2
    bits = lax.bitcast_convert_type(x.astype(F32), jnp.uint32)
    return (bits[:, :n] >> 16) | (bits[:, n:] & jnp.uint32(0xFFFF0000))


def _unpack_bf16_pairs(u):
    lo = lax.bitcast_convert_type(u << 16, F32).astype(BF16)
    hi = lax.bitcast_convert_type(u & jnp.uint32(0xFFFF0000), F32).astype(BF16)
    return lo, hi


def _out_proj_kernel(mg_ref, w_ref, x_ref, g_ref, wr_hi_ref, wr_lo_ref, br_ref, x1_ref, h2_ref, lg_ref):
    x1 = x_ref[...] + jnp.dot(mg_ref[...], w_ref[...], preferred_element_type=F32)
    x1_ref[...] = x1
    ms = jnp.mean(x1 * x1, axis=-1, keepdims=True)
    h2 = x1 * lax.rsqrt(ms + NORM_EPS) * g_ref[...]
    hi, lo = _split_bf16(h2)
    h2_ref[...] = _pack_bf16_pairs(hi)
    lg = (jnp.dot(hi, wr_hi_ref[...], preferred_element_type=F32)
          + jnp.dot(lo, wr_hi_ref[...], preferred_element_type=F32)
          + jnp.dot(hi, wr_lo_ref[...], preferred_element_type=F32))
    lane = lax.broadcasted_iota(jnp.int32, lg.shape, 1)
    lg_ref[...] = jnp.where(lane < N_EXPERTS, lg + br_ref[...], -jnp.inf)


def _output_projection(merged, w_out_bf, x2d, g_ffn, wr_hi, wr_lo, b_router_row):
    t, d = x2d.shape
    tm = min(ROW_TILE, t)
    row = lambda i: (i, 0)
    fixed = lambda i: (0, 0)
    return pl.pallas_call(
        _out_proj_kernel,
        grid=(t // tm,),
        in_specs=[
            pl.BlockSpec((tm, d), row),
            pl.BlockSpec((d, d), fixed),
            pl.BlockSpec((tm, d), row),
            pl.BlockSpec((1, d), fixed),
            pl.BlockSpec((d, LANES), fixed),
            pl.BlockSpec((d, LANES), fixed),
            pl.BlockSpec((1, LANES), fixed),
        ],
        out_specs=[pl.BlockSpec((tm, d), row), pl.BlockSpec((tm, d // 2), row), pl.BlockSpec((tm, LANES), row)],
        out_shape=[jax.ShapeDtypeStruct((t, d), F32), jax.ShapeDtypeStruct((t, d // 2), jnp.uint32),
                   jax.ShapeDtypeStruct((t, LANES), F32)],
        compiler_params=_params(("arbitrary",)),
        name="output_projection",
    )(merged, w_out_bf, x2d, g_ffn.reshape(1, d), wr_hi, wr_lo, b_router_row)


def _route_kernel(lg_ref, idx_ref, gate_ref, rank_ref, cnt_ref, carry_sc):
    i = pl.program_id(0)

    @pl.when(i == 0)
    def _():
        carry_sc[...] = jnp.zeros(carry_sc.shape, F32)

    work = lg_ref[...]
    tm = work.shape[0]
    lane = lax.broadcasted_iota(jnp.int32, work.shape, 1).astype(F32)
    vals, idxs = [], []
    member = jnp.zeros(work.shape, F32)
    for _ in range(TOP_K):
        mx = jnp.max(work, axis=1, keepdims=True)
        idx = jnp.min(jnp.where(work == mx, lane, float(LANES)), axis=1, keepdims=True)
        sel = lane == idx
        vals.append(mx)
        idxs.append(idx)
        member = jnp.where(sel, 1.0, member)
        work = jnp.where(sel, -jnp.inf, work)
    exps = [jnp.exp(v - vals[0]) for v in vals]
    z = exps[0] + exps[1] + exps[2] + exps[3]
    r = lax.broadcasted_iota(jnp.int32, (tm, tm), 0)
    c = lax.broadcasted_iota(jnp.int32, (tm, tm), 1)
    before = jnp.where(c < r, 1.0, 0.0).astype(BF16)
    cum = jnp.dot(before, member.astype(BF16), preferred_element_type=F32) + carry_sc[...]
    idx_out = jnp.zeros(work.shape, F32)
    gate_out = jnp.zeros(work.shape, F32)
    rank_out = jnp.zeros(work.shape, F32)
    for k in range(TOP_K):
        rank = jnp.sum(jnp.where(lane == idxs[k], cum, 0.0), axis=1, keepdims=True)
        idx_out = jnp.where(lane == float(k), idxs[k], idx_out)
        gate_out = jnp.where(lane == float(k), exps[k] / z, gate_out)
        rank_out = jnp.where(lane == float(k), rank, rank_out)
    idx_ref[...] = idx_out.astype(jnp.int32)
    gate_ref[...] = gate_out
    rank_ref[...] = rank_out.astype(jnp.int32)
    carry_sc[...] += jnp.sum(member, axis=0, keepdims=True)
    cnt_ref[...] = carry_sc[...].astype(jnp.int32)


def _route(logits):
    t = logits.shape[0]
    tm = min(ROW_TILE, t)
    row = lambda i: (i, 0)
    return pl.pallas_call(
        _route_kernel,
        grid=(t // tm,),
        in_specs=[pl.BlockSpec((tm, LANES), row)],
        out_specs=[pl.BlockSpec((tm, LANES), row), pl.BlockSpec((tm, LANES), row),
                   pl.BlockSpec((tm, LANES), row), pl.BlockSpec((1, LANES), lambda i: (0, 0))],
        out_shape=[jax.ShapeDtypeStruct((t, LANES), jnp.int32), jax.ShapeDtypeStruct((t, LANES), F32),
                   jax.ShapeDtypeStruct((t, LANES), jnp.int32), jax.ShapeDtypeStruct((1, LANES), jnp.int32)],
        scratch_shapes=[pltpu.VMEM((1, LANES), F32)],
        compiler_params=_params(("arbitrary",)),
        name="route",
    )(logits)


def _row_copy(src_hbm, row, dst, dst_row, sem):
    return pltpu.make_async_copy(src_hbm.at[pl.ds(row, 1)], dst.at[pl.ds(dst_row, 1)], sem)


def _dispatch_kernel(dest_ref, h_ref, init_hbm, o_hbm, sem):
    del init_hbm
    i = pl.program_id(0)
    rows = h_ref.shape[0]

    def copy(t, slot):
        return pltpu.make_async_copy(h_ref.at[pl.ds(t, 1)], o_hbm.at[pl.ds(slot, 1)], sem)

    def start(t, carry):
        for k in range(TOP_K):
            copy(t, dest_ref[(i * rows + t) * TOP_K + k]).start()
        return carry

    def wait(t, carry):
        for k in range(TOP_K):
            copy(t, 0).wait()
        return carry

    lax.fori_loop(0, rows, start, 0, unroll=DMA_LOOP_UNROLL)
    lax.fori_loop(0, rows, wait, 0, unroll=DMA_LOOP_UNROLL)


def _dispatch(h2_packed, dest_flat, n_slots):
    t, w = h2_packed.shape
    rows = min(COMBINE_ROWS, t)
    return pl.pallas_call(
        _dispatch_kernel,
        grid_spec=pltpu.PrefetchScalarGridSpec(
            num_scalar_prefetch=1,
            grid=(t // rows,),
            in_specs=[pl.BlockSpec((rows, w), lambda i, dest: (i, 0)), pl.BlockSpec(memory_space=pl.ANY)],
            out_specs=pl.BlockSpec(memory_space=pl.ANY),
            scratch_shapes=[pltpu.SemaphoreType.DMA(())],
        ),
        out_shape=jax.ShapeDtypeStruct((n_slots, w), h2_packed.dtype),
        input_output_aliases={2: 0},
        compiler_params=_params(("arbitrary",)),
        name="dispatch",
    )(dest_flat, h2_packed, jnp.zeros((n_slots, w), h2_packed.dtype))


def _gate_up_kernel(be_ref, used_ref, x_ref, wg_ref, wu_ref, bg_ref, bu_ref, o_ref, wg_bf, wu_bf):
    m = pl.program_id(1)

    @pl.when(m < used_ref[0])
    def _():
        @pl.when((m == 0) | (be_ref[m] != be_ref[jnp.maximum(m - 1, 0)]))
        def _():
            wg_bf[...] = wg_ref[...].astype(BF16)
            wu_bf[...] = wu_ref[...].astype(BF16)

        for rows in _row_blocks(x_ref.shape[0]):
            lo, hi = _unpack_bf16_pairs(x_ref[rows, :])
            half = lo.shape[1]
            gate = (jnp.dot(lo, wg_bf[:half, :], preferred_element_type=F32)
                    + jnp.dot(hi, wg_bf[half:, :], preferred_element_type=F32) + bg_ref[...])
            up = (jnp.dot(lo, wu_bf[:half, :], preferred_element_type=F32)
                  + jnp.dot(hi, wu_bf[half:, :], preferred_element_type=F32) + bu_ref[...])
            gate = jnp.minimum(gate, SWIGLU_LIMIT)
            up = jnp.clip(up, -SWIGLU_LIMIT, SWIGLU_LIMIT)
            act = (up + 1.0) * gate * (1.0 / (1.0 + jnp.exp(-SWIGLU_ALPHA * gate)))
            o_ref[rows, :] = act.astype(o_ref.dtype)

    @pl.when(m >= used_ref[0])
    def _():
        o_ref[...] = jnp.zeros(o_ref.shape, o_ref.dtype)


def _gate_up(xs, w_gate_up, b_gate_up, block_e, n_used):
    p = xs.shape[0]
    d = w_gate_up.shape[1]
    d_ff = w_gate_up.shape[2] // 2
    rows, tf = EXPERT_ROWS, FF_TILE
    nf = d_ff // tf
    blk = lambda m, used: jnp.minimum(m, used[0] - 1)
    return pl.pallas_call(
        _gate_up_kernel,
        grid_spec=pltpu.PrefetchScalarGridSpec(
            num_scalar_prefetch=2,
            grid=(nf, p // rows),
            in_specs=[
                pl.BlockSpec((rows, xs.shape[1]), lambda f, m, be, used: (blk(m, used), 0)),
                pl.BlockSpec((None, d, tf), lambda f, m, be, used: (be[blk(m, used)], 0, f)),
                pl.BlockSpec((None, d, tf), lambda f, m, be, used: (be[blk(m, used)], 0, nf + f)),
                pl.BlockSpec((None, 1, tf), lambda f, m, be, used: (be[blk(m, used)], 0, f)),
                pl.BlockSpec((None, 1, tf), lambda f, m, be, used: (be[blk(m, used)], 0, nf + f)),
            ],
            out_specs=pl.BlockSpec((rows, tf), lambda f, m, be, used: (m, f)),
            scratch_shapes=[pltpu.VMEM((d, tf), BF16), pltpu.VMEM((d, tf), BF16)],
        ),
        out_shape=jax.ShapeDtypeStruct((p, d_ff), BF16),
        compiler_params=_params(("arbitrary", "arbitrary")),
        name="expert_gate_up",
    )(block_e, n_used, xs, w_gate_up, w_gate_up, b_gate_up, b_gate_up)


def _down_kernel(be_ref, used_ref, a_ref, w_ref, b_ref, o_ref, w_bf):
    m = pl.program_id(1)

    @pl.when(m < used_ref[0])
    def _():
        @pl.when((m == 0) | (be_ref[m] != be_ref[jnp.maximum(m - 1, 0)]))
        def _():
            w_bf[...] = w_ref[...].astype(BF16)

        for rows in _row_blocks(a_ref.shape[0]):
            o_ref[rows, :] = jnp.dot(a_ref[rows, :], w_bf[...], preferred_element_type=F32) + b_ref[...]

    @pl.when(m >= used_ref[0])
    def _():
        o_ref[...] = jnp.zeros(o_ref.shape, o_ref.dtype)


def _down(act, w_down, b_down, block_e, n_used):
    p, d_ff = act.shape
    d = w_down.shape[2]
    rows, tn = EXPERT_ROWS, DOWN_TN
    blk = lambda m, used: jnp.minimum(m, used[0] - 1)
    return pl.pallas_call(
        _down_kernel,
        grid_spec=pltpu.PrefetchScalarGridSpec(
            num_scalar_prefetch=2,
            grid=(d // tn, p // rows),
            in_specs=[
                pl.BlockSpec((rows, d_ff), lambda n, m, be, used: (blk(m, used), 0)),
                pl.BlockSpec((None, d_ff, tn), lambda n, m, be, used: (be[blk(m, used)], 0, n)),
                pl.BlockSpec((None, 1, tn), lambda n, m, be, used: (be[blk(m, used)], 0, n)),
            ],
            out_specs=pl.BlockSpec((rows, tn), lambda n, m, be, used: (m, n)),
            scratch_shapes=[pltpu.VMEM((d_ff, tn), BF16)],
        ),
        out_shape=jax.ShapeDtypeStruct((p, d), F32),
        compiler_params=_params(("arbitrary", "arbitrary")),
        name="expert_down",
    )(block_e, n_used, act, w_down, b_down)


def _combine_kernel(dest_ref, x_ref, g_ref, y_hbm, o_ref, buf, sem):
    i = pl.program_id(0)
    rows = x_ref.shape[0]

    def start(t, carry):
        for k in range(TOP_K):
            _row_copy(y_hbm, dest_ref[(i * rows + t) * TOP_K + k], buf.at[k], t, sem).start()
        return carry

    def wait(t, carry):
        for k in range(TOP_K):
            _row_copy(y_hbm, 0, buf.at[k], t, sem).wait()
        return carry

    lax.fori_loop(0, rows, start, 0, unroll=DMA_LOOP_UNROLL)
    lax.fori_loop(0, rows, wait, 0, unroll=DMA_LOOP_UNROLL)
    g = g_ref[...]
    moe = ((g[:, 0:1] * buf[0] + g[:, 1:2] * buf[1]) + (g[:, 2:3] * buf[2] + g[:, 3:4] * buf[3]))
    o_ref[...] = x_ref[...] + moe


def _combine(x1, gates_padded, y, dest_flat):
    t, d = x1.shape
    rows = min(COMBINE_ROWS, t)
    return pl.pallas_call(
        _combine_kernel,
        grid_spec=pltpu.PrefetchScalarGridSpec(
            num_scalar_prefetch=1,
            grid=(t // rows,),
            in_specs=[pl.BlockSpec((rows, d), lambda i, dest: (i, 0)),
                      pl.BlockSpec((rows, LANES), lambda i, dest: (i, 0)),
                      pl.BlockSpec(memory_space=pl.ANY)],
            out_specs=pl.BlockSpec((rows, d), lambda i, dest: (i, 0)),
            scratch_shapes=[pltpu.VMEM((TOP_K, rows, d), F32), pltpu.SemaphoreType.DMA(())],
        ),
        out_shape=jax.ShapeDtypeStruct((t, d), F32),
        compiler_params=_params(("arbitrary",)),
        name="combine",
    )(dest_flat, x1, gates_padded, y)


def _moe(x1, h2, logits, w_gate_up, b_gate_up, w_down, b_down):
    t, d = x1.shape
    n_exp = w_gate_up.shape[0]
    rows = EXPERT_ROWS
    idx_p, gate_p, rank_p, cnt_p = _route(logits)
    idx, rank = idx_p[:, :TOP_K], rank_p[:, :TOP_K]
    counts = cnt_p[0, :n_exp]

    padded = (counts + rows - 1) // rows * rows
    pend = jnp.cumsum(padded)
    pstart = pend - padded
    n_blocks = (t * TOP_K + n_exp * (rows - 1) + rows - 1) // rows
    dest_flat = (pstart[idx] + rank).astype(jnp.int32).reshape(-1)
    block_start = jnp.arange(n_blocks, dtype=jnp.int32) * rows
    block_e = jnp.minimum(jnp.sum(pend[None, :] <= block_start[:, None], axis=1), n_exp - 1).astype(jnp.int32)
    n_used = (pend[-1] // rows).astype(jnp.int32).reshape(1)

    xs = _dispatch(h2, dest_flat, n_blocks * rows)
    act = _gate_up(xs, w_gate_up, b_gate_up.reshape(n_exp, 1, -1), block_e, n_used)
    y = _down(act, w_down, b_down.reshape(n_exp, 1, -1), block_e, n_used)
    return _combine(x1, gate_p, y, dest_flat)


def _rope_tables(positions):
    half = ROT_DIM // 2
    inv_freq = ROPE_THETA ** (-jnp.arange(0, ROT_DIM, 2, dtype=F32) / ROT_DIM)
    ang = positions.reshape(-1).astype(F32)[:, None] * inv_freq
    cos, sin = jnp.cos(ang), jnp.sin(ang)
    t = ang.shape[0]
    cos_t = jnp.concatenate([cos, cos, jnp.ones((t, HEAD - ROT_DIM), F32)], axis=1)
    sa_t = jnp.concatenate([-sin, jnp.zeros((t, HEAD - half), F32)], axis=1)
    sb_t = jnp.concatenate([jnp.zeros((t, half), F32), sin, jnp.zeros((t, HEAD - ROT_DIM), F32)], axis=1)
    return cos_t, sa_t, sb_t


def _layer(x2d, mem2d, batch, seq, cos_t, sa_t, sb_t, layer_idx, g_mix_norm, w_in, b_gate, diff_q_norm,
           diff_k_norm, diff_lambda, diff_subln, dil_q_norm, dil_k_norm, g_mem_norm, w_mem_kv, mem_q_norm,
           mem_k_norm, w_branch_diff, w_branch_dil, w_branch_mem, w_out, g_ffn_norm, w_router, b_router,
           w_gate_up, b_gate_up, w_down, b_down):
    d = x2d.shape[1]
    diff_cols = DIFF_HEADS * 2 * HEAD
    dil_cols = DIL_HEADS * HEAD
    mem_cols = MEM_HEADS * MEM_HEAD_DIM
    off_dq, off_dk, off_dv = 0, diff_cols, 2 * diff_cols
    off_lq = 3 * diff_cols
    off_lk, off_lv = off_lq + dil_cols, off_lq + 2 * dil_cols
    off_mq = off_lq + 3 * dil_cols
    off_gate = off_mq + mem_cols
    n_cols = off_gate + 3 * d
    assert w_in.shape == (d, n_cols)
    tn = PROJ_TN
    tiles = lambda off, width: tuple(range(off // tn, (off + width) // tn))
    rope_tiles = (tiles(off_dq, diff_cols) + tiles(off_dk, diff_cols) + tiles(off_lq, dil_cols)
                  + tiles(off_lk, dil_cols))

    rep = lambda g, width: jnp.tile(g.astype(F32), width // g.shape[0])
    diff_q_gain = diff_q_norm.astype(F32) * (HEAD ** -0.5 * LOG2E)
    dil_q_gain = dil_q_norm.astype(F32) * (HEAD ** -0.5 * LOG2E)

    def score_bound(q_gain, k_gain):
        return (1.01 * HEAD * jnp.max(jnp.abs(q_gain)) * jnp.max(jnp.abs(k_gain.astype(F32)))).reshape(1)

    diff_bound = score_bound(diff_q_gain, diff_k_norm)
    dil_bound = score_bound(dil_q_gain, dil_k_norm)
    gain_row = jnp.concatenate([
        rep(diff_q_gain, diff_cols), rep(diff_k_norm, diff_cols), jnp.ones((diff_cols,), F32),
        rep(dil_q_gain, dil_cols), rep(dil_k_norm, dil_cols), jnp.ones((dil_cols,), F32),
        rep(mem_q_norm, mem_cols) * MEM_HEAD_DIM ** -0.5, jnp.ones((3 * d,), F32)]).reshape(1, n_cols)
    bias_row = jnp.concatenate([jnp.zeros((off_gate,), F32), b_gate.astype(F32)]).reshape(1, n_cols)

    h = _rmsnorm(x2d, g_mix_norm, BF16)
    proj = _input_projection(h, w_in, gain_row, bias_row, cos_t, sa_t, sb_t, rope_tiles, off_mq // tn,
                             off_gate // tn)

    lam_init = 0.8 - 0.6 * float(np.exp(-0.3 * layer_idx))
    o_diff = _diff_attention(proj, batch, seq, off_dq // DIFF_V_DIM, off_dk // DIFF_V_DIM, off_dv // DIFF_V_DIM,
                             diff_bound, diff_lambda.astype(F32), diff_subln, lam_init)
    o_dil = _dilated_attention(proj, batch, seq, off_lq // HEAD, off_lk // HEAD, off_lv // HEAD, dil_bound)
    kv = _mem_kv(mem2d, g_mem_norm, w_mem_kv.astype(BF16), rep(mem_k_norm, mem_cols).reshape(1, mem_cols), batch)
    o_mem = _memory_attention(proj, kv, batch, seq, off_mq // mem_cols)

    merged = _branch_merge(o_diff, o_dil, o_mem, w_branch_diff.astype(BF16), w_branch_dil.astype(BF16),
                           w_branch_mem.astype(BF16), proj, off_gate // tn, d)

    n_exp = w_router.shape[1]
    wr = jnp.zeros((d, LANES), F32).at[:, :n_exp].set(w_router)
    wr_hi, wr_lo = _split_bf16(wr)
    br_row = jnp.zeros((1, LANES), F32).at[0, :n_exp].set(b_router)
    x1, h2, logits = _output_projection(merged, w_out.astype(BF16), x2d, g_ffn_norm, wr_hi, wr_lo, br_row)
    return _moe(x1, h2, logits, w_gate_up, b_gate_up, w_down, b_down)


def kernel(x, mem, positions, g_mix_norm, w_in, b_gate, diff_q_norm, diff_k_norm, diff_lambda, diff_subln,
           dil_q_norm, dil_k_norm, g_mem_norm, w_mem_kv, mem_q_norm, mem_k_norm, w_branch_diff, w_branch_dil,
           w_branch_mem, w_out, g_ffn_norm, w_router, b_router, w_gate_up, b_gate_up, w_down, b_down):
    batch, seq, d = x.shape
    cos_t, sa_t, sb_t = _rope_tables(positions)
    x2d = x.reshape(batch * seq, d)
    mem2d = mem.reshape(-1, d)
    for l in range(g_mix_norm.shape[0]):
        x2d = _layer(x2d, mem2d, batch, seq, cos_t, sa_t, sb_t, l, g_mix_norm[l], w_in[l], b_gate[l],
                     diff_q_norm[l], diff_k_norm[l], diff_lambda[l], diff_subln[l], dil_q_norm[l],
                     dil_k_norm[l], g_mem_norm[l], w_mem_kv[l], mem_q_norm[l], mem_k_norm[l],
                     w_branch_diff[l], w_branch_dil[l], w_branch_mem[l], w_out[l], g_ffn_norm[l],
                     w_router[l], b_router[l], w_gate_up[l], b_gate_up[l], w_down[l], b_down[l])
    return x2d.reshape(batch, seq, d)
```

```python
import functools

import numpy as np
import jax
import jax.numpy as jnp
from jax import lax
from jax.experimental import pallas as pl
from jax.experimental.pallas import tpu as pltpu

F32 = jnp.float32
BF16 = jnp.bfloat16

NORM_EPS = 1e-6
NEG_INF = -1e30
ROPE_THETA = 500000.0
ROT_DIM = 32
HEAD = 128
DIFF_HEADS = 8
DIFF_V_DIM = 256
DIL_HEADS = 8
DIL_CONFIGS = ((128, 1), (512, 4), (2048, 16))
DIL_PAD = 2048
MEM_HEADS = 4
MEM_HEAD_DIM = 256
N_EXPERTS = 32
TOP_K = 4
SWIGLU_LIMIT = 7.0
SWIGLU_ALPHA = 1.702

LANES = 128
VMEM_LIMIT = 56 * 1024 * 1024

PROJ_TN = 1024
ROW_TILE = 512
DIFF_TQ = 1024
DIFF_TK = 1024
DIFF_SUB = 256
DIFF_FIXED_SHIFT_MAX = 40.0
LOG2E = 1.4426950408889634
DIL_TQ = 256
EXPERT_ROWS = 512
FF_TILE = 1024
DOWN_TN = 2048
COMBINE_ROWS = 256
DMA_LOOP_UNROLL = 8
MATMUL_ROW_BLOCK = 256
PROJ_TM = 1024


def _params(sem, **kw):
    return pltpu.CompilerParams(dimension_semantics=sem, vmem_limit_bytes=VMEM_LIMIT, **kw)


def _rmsnorm_kernel(x_ref, g_ref, o_ref):
    x = x_ref[...]
    ms = jnp.mean(x * x, axis=-1, keepdims=True)
    o_ref[...] = (x * lax.rsqrt(ms + NORM_EPS) * g_ref[...]).astype(o_ref.dtype)


def _rmsnorm(x, g, out_dtype):
    t, d = x.shape
    tm = min(ROW_TILE, t)
    return pl.pallas_call(
        _rmsnorm_kernel,
        grid=(t // tm,),
        in_specs=[pl.BlockSpec((tm, d), lambda i: (i, 0)), pl.BlockSpec((1, d), lambda i: (0, 0))],
        out_specs=pl.BlockSpec((tm, d), lambda i: (i, 0)),
        out_shape=jax.ShapeDtypeStruct((t, d), out_dtype),
        compiler_params=_params(("arbitrary",)),
        name="rmsnorm",
    )(x, g.reshape(1, d))


def _row_blocks(rows):
    size = min(MATMUL_ROW_BLOCK, rows)
    return [slice(r, r + size) for r in range(0, rows, size)]


def _chunk_norm(x, gain, width):
    outs = []
    for c in range(x.shape[1] // width):
        xc = x[:, c * width:(c + 1) * width]
        ms = jnp.mean(xc * xc, axis=-1, keepdims=True)
        outs.append(xc * lax.rsqrt(ms + NORM_EPS) * gain[:, c * width:(c + 1) * width])
    return outs


def _proj_kernel(rope_tiles, mem_tile, gate_tile0, h_ref, w_ref, gain_ref, bias_ref, cos_ref, sa_ref,
                 sb_ref, o_ref, wbf_ref):
    n = pl.program_id(0)
    m = pl.program_id(1)

    @pl.when(m == 0)
    def _():
        wbf_ref[...] = w_ref[...].astype(BF16)

    is_rope = functools.reduce(jnp.logical_or, [n == t for t in rope_tiles])
    is_mem = n == mem_tile
    is_gate = n >= gate_tile0

    def for_row_blocks(epilogue):
        for rows in _row_blocks(h_ref.shape[0]):
            epilogue(rows, jnp.dot(h_ref[rows, :], wbf_ref[...], preferred_element_type=F32))

    def rope(rows, acc):
        cos, sa, sb = cos_ref[rows, :], sa_ref[rows, :], sb_ref[rows, :]
        for c, y in enumerate(_chunk_norm(acc, gain_ref[...], HEAD)):
            y = y * cos + pltpu.roll(y, HEAD - ROT_DIM // 2, 1) * sa + pltpu.roll(y, ROT_DIM // 2, 1) * sb
            o_ref[rows, c * HEAD:(c + 1) * HEAD] = y.astype(o_ref.dtype)

    def mem_q(rows, acc):
        for c, y in enumerate(_chunk_norm(acc, gain_ref[...], MEM_HEAD_DIM)):
            o_ref[rows, c * MEM_HEAD_DIM:(c + 1) * MEM_HEAD_DIM] = y.astype(o_ref.dtype)

    def gate(rows, acc):
        z = acc + bias_ref[...]
        o_ref[rows, :] = (1.0 / (1.0 + jnp.exp(-z))).astype(o_ref.dtype)

    def plain(rows, acc):
        o_ref[rows, :] = acc.astype(o_ref.dtype)

    pl.when(is_rope)(lambda: for_row_blocks(rope))
    pl.when(is_mem)(lambda: for_row_blocks(mem_q))
    pl.when(is_gate)(lambda: for_row_blocks(gate))
    pl.when(jnp.logical_not(is_rope | is_mem | is_gate))(lambda: for_row_blocks(plain))


def _input_projection(h, w_in, gain_row, bias_row, cos_t, sa_t, sb_t, rope_tiles, mem_tile, gate_tile0):
    t, d = h.shape
    n_cols = w_in.shape[1]
    tm = min(PROJ_TM, t)
    tn = PROJ_TN
    row = lambda n, m: (m, 0)
    col = lambda n, m: (0, n)
    return pl.pallas_call(
        functools.partial(_proj_kernel, rope_tiles, mem_tile, gate_tile0),
        grid=(n_cols // tn, t // tm),
        in_specs=[
            pl.BlockSpec((tm, d), row),
            pl.BlockSpec((d, tn), col),
            pl.BlockSpec((1, tn), col),
            pl.BlockSpec((1, tn), col),
            pl.BlockSpec((tm, HEAD), row),
            pl.BlockSpec((tm, HEAD), row),
            pl.BlockSpec((tm, HEAD), row),
        ],
        out_specs=pl.BlockSpec((tm, tn), lambda n, m: (m, n)),
        out_shape=jax.ShapeDtypeStruct((t, n_cols), BF16),
        scratch_shapes=[pltpu.VMEM((d, tn), BF16)],
        compiler_params=_params(("arbitrary", "arbitrary")),
        name="input_projection",
    )(h, w_in, gain_row, bias_row, cos_t, sa_t, sb_t)


def _diff_attn_kernel(lam_init, bound_ref, q_ref, k_ref, v_ref, lam_ref, subln_ref, o_ref, m_sc, l_sc, acc_sc):
    qi = pl.program_id(2)
    tq = q_ref.shape[0]
    sub = min(DIFF_SUB, tq)
    tk = min(DIFF_TK, tq)
    per_tile = tq // tk
    bound = bound_ref[0]
    l_sc[...] = jnp.zeros(l_sc.shape, F32)
    acc_sc[...] = jnp.zeros(acc_sc.shape, F32)

    def row_plan(d):
        plan = []
        for r in range(tq // sub):
            if d is None or r * sub >= (d + 1) * tk:
                plan.append((r, tk, None))
            elif (r + 1) * sub > d * tk:
                plan.append((r, min(tk, (r + 1) * sub - d * tk), r * sub - d * tk))
        return plan

    def scores(start, c, r, nk, offset):
        s = lax.dot_general(q_ref[r * sub:(r + 1) * sub, c * HEAD:(c + 1) * HEAD],
                            k_ref[pl.ds(start, nk), c * HEAD:(c + 1) * HEAD],
                            (((1,), (1,)), ((), ())), preferred_element_type=F32)
        if offset is not None:
            row = lax.broadcasted_iota(jnp.int32, s.shape, 0) + offset
            col = lax.broadcasted_iota(jnp.int32, s.shape, 1)
            s = jnp.where(row >= col, s, NEG_INF)
        return s

    def fixed_step(start, plan):
        for c in range(2):
            for r, nk, offset in plan:
                rows = slice(r * sub, (r + 1) * sub)
                p = jnp.exp2(scores(start, c, r, nk, offset) - bound)
                l_sc[c, rows] += functools.reduce(
                    jnp.add, [p[:, i * LANES:(i + 1) * LANES] for i in range(nk // LANES)])
                acc_sc[c, rows] += jnp.dot(p.astype(v_ref.dtype), v_ref[pl.ds(start, nk), :],
                                           preferred_element_type=F32)

    def online_step(start, plan):
        for c in range(2):
            for r, nk, offset in plan:
                rows = slice(r * sub, (r + 1) * sub)
                s = scores(start, c, r, nk, offset)
                m_prev = m_sc[c, rows]
                m_new = jnp.maximum(m_prev, jnp.max(s, axis=1, keepdims=True))
                alpha = jnp.exp2(m_prev - m_new)
                p = jnp.exp2(s - jnp.tile(m_new, (1, nk // LANES)))
                l_sc[c, rows] = alpha * l_sc[c, rows] + jnp.sum(p, axis=1, keepdims=True)
                pv = jnp.dot(p.astype(v_ref.dtype), v_ref[pl.ds(start, nk), :], preferred_element_type=F32)
                acc_sc[c, rows] = jnp.tile(alpha, (1, acc_sc.shape[2] // LANES)) * acc_sc[c, rows] + pv
                m_sc[c, rows] = m_new

    def sweep(step):
        def body(j, carry):
            step(pl.multiple_of(j * tk, tk), row_plan(None))
            return carry

        lax.fori_loop(0, qi * per_tile, body, 0)
        for d in range(per_tile):
            step(pl.multiple_of((qi * per_tile + d) * tk, tk), row_plan(d))

    def finish(l0, l1):
        lp = lam_ref[...]
        lam = (jnp.exp(jnp.sum(lp[0:1] * lp[1:2], axis=1, keepdims=True))
               - jnp.exp(jnp.sum(lp[2:3] * lp[3:4], axis=1, keepdims=True)) + lam_init)
        o = acc_sc[0] / l0 - lam * (acc_sc[1] / l1)
        ms = jnp.mean(o * o, axis=-1, keepdims=True)
        o = o * lax.rsqrt(ms + NORM_EPS) * subln_ref[...] * (1.0 - lam_init)
        o_ref[...] = o.astype(o_ref.dtype)

    @pl.when(bound <= DIFF_FIXED_SHIFT_MAX)
    def _():
        sweep(fixed_step)
        finish(jnp.sum(l_sc[0], axis=1, keepdims=True), jnp.sum(l_sc[1], axis=1, keepdims=True))

    @pl.when(bound > DIFF_FIXED_SHIFT_MAX)
    def _():
        m_sc[...] = jnp.full(m_sc.shape, NEG_INF, F32)
        sweep(online_step)
        finish(l_sc[0][:, :1], l_sc[1][:, :1])


def _diff_attention(proj, batch, seq, q_blk0, k_blk0, v_blk0, score_bound, diff_lambda, subln, lam_init):
    tq = min(DIFF_TQ, seq)
    nq = seq // tq
    dv = DIFF_V_DIM
    return pl.pallas_call(
        functools.partial(_diff_attn_kernel, lam_init),
        grid_spec=pltpu.PrefetchScalarGridSpec(
            num_scalar_prefetch=1,
            grid=(batch, DIFF_HEADS, nq),
            in_specs=[
                pl.BlockSpec((tq, dv), lambda b, h, i, bound: (b * nq + i, q_blk0 + h)),
                pl.BlockSpec((seq, dv), lambda b, h, i, bound: (b, k_blk0 + h)),
                pl.BlockSpec((seq, dv), lambda b, h, i, bound: (b, v_blk0 + h)),
                pl.BlockSpec((4, HEAD), lambda b, h, i, bound: (0, 0)),
                pl.BlockSpec((1, dv), lambda b, h, i, bound: (0, 0)),
            ],
            out_specs=pl.BlockSpec((tq, dv), lambda b, h, i, bound: (b * nq + i, h)),
            scratch_shapes=[pltpu.VMEM((2, tq, LANES), F32), pltpu.VMEM((2, tq, LANES), F32),
                            pltpu.VMEM((2, tq, dv), F32)],
        ),
        out_shape=jax.ShapeDtypeStruct((batch * seq, DIFF_HEADS * dv), BF16),
        compiler_params=_params(("arbitrary", "arbitrary", "arbitrary")),
        name="diff_attention",
    )(score_bound, proj, proj, proj, diff_lambda, subln.reshape(1, dv))


def _dilated_multiplicity(tq):
    n_chunks = DIL_PAD // tq + 1
    q = np.arange(tq)[:, None]
    k = np.arange(tq)[None, :]
    out = np.zeros((n_chunks + 1, tq, tq), np.float32)
    for c in range(n_chunks):
        d = DIL_PAD - c * tq + q - k
        for window, dil in DIL_CONFIGS:
            out[c] += ((d >= 0) & (d <= window) & (d % dil == 0)).astype(np.float32)
    return out


def _dil_attn_kernel(bound_ref, q_ref, k_ref, v_ref, mult_ref, o_ref):
    qi = pl.program_id(2)
    tq = q_ref.shape[0]
    n_chunks = mult_ref.shape[0] - 1
    bound = bound_ref[0]
    q = q_ref[...]

    def chunk(c):
        blk = qi - (n_chunks - 1) + c
        start = pl.multiple_of(jnp.maximum(blk, 0) * tq, tq)
        w = mult_ref[jnp.where(blk >= 0, c, n_chunks)]
        s = lax.dot_general(q, k_ref[pl.ds(start, tq), :], (((1,), (1,)), ((), ())),
                            preferred_element_type=F32)
        return start, w, s

    def pv(p, start):
        return jnp.dot(p.astype(v_ref.dtype), v_ref[pl.ds(start, tq), :], preferred_element_type=F32)

    @pl.when(bound <= DIFF_FIXED_SHIFT_MAX)
    def _():
        den = jnp.zeros((tq, LANES), F32)
        acc = jnp.zeros(o_ref.shape, F32)
        for c in range(n_chunks):
            start, w, s = chunk(c)
            p = w * jnp.exp2(s - bound)
            den += functools.reduce(jnp.add, [p[:, i * LANES:(i + 1) * LANES] for i in range(tq // LANES)])
            acc += pv(p, start)
        o_ref[...] = (acc / jnp.sum(den, axis=1, keepdims=True)).astype(o_ref.dtype)

    @pl.when(bound > DIFF_FIXED_SHIFT_MAX)
    def _():
        chunks = [chunk(c) for c in range(n_chunks)]
        scores = [jnp.where(w > 0.0, s, NEG_INF) for _, w, s in chunks]
        m = jnp.max(functools.reduce(jnp.maximum, scores), axis=1, keepdims=True)
        den = jnp.zeros((tq, 1), F32)
        acc = jnp.zeros(o_ref.shape, F32)
        for (start, w, _), s in zip(chunks, scores):
            p = w * jnp.exp2(s - m)
            den += jnp.sum(p, axis=1, keepdims=True)
            acc += pv(p, start)
        o_ref[...] = (acc / den).astype(o_ref.dtype)


def _dilated_attention(proj, batch, seq, q_blk0, k_blk0, v_blk0, score_bound):
    tq = DIL_TQ
    nq = seq // tq
    mult = jnp.asarray(_dilated_multiplicity(tq))
    return pl.pallas_call(
        _dil_attn_kernel,
        grid_spec=pltpu.PrefetchScalarGridSpec(
            num_scalar_prefetch=1,
            grid=(batch, DIL_HEADS, nq),
            in_specs=[
                pl.BlockSpec((tq, HEAD), lambda b, h, i, bound: (b * nq + i, q_blk0 + h)),
                pl.BlockSpec((seq, HEAD), lambda b, h, i, bound: (b, k_blk0 + h)),
                pl.BlockSpec((seq, HEAD), lambda b, h, i, bound: (b, v_blk0 + h)),
                pl.BlockSpec(mult.shape, lambda b, h, i, bound: (0, 0, 0)),
            ],
            out_specs=pl.BlockSpec((tq, HEAD), lambda b, h, i, bound: (b * nq + i, h)),
        ),
        out_shape=jax.ShapeDtypeStruct((batch * seq, DIL_HEADS * HEAD), BF16),
        compiler_params=_params(("arbitrary", "arbitrary", "arbitrary")),
        name="dilated_attention",
    )(score_bound, proj, proj, proj, mult)


def _mem_kv_kernel(mem_ref, g_ref, w_ref, kn_ref, o_ref):
    x = mem_ref[...]
    ms = jnp.mean(x * x, axis=-1, keepdims=True)
    h = (x * lax.rsqrt(ms + NORM_EPS) * g_ref[...]).astype(BF16)
    kv = jnp.dot(h, w_ref[...], preferred_element_type=F32)
    half = kv.shape[1] // 2
    for c, y in enumerate(_chunk_norm(kv[:, :half], kn_ref[...], MEM_HEAD_DIM)):
        o_ref[:, c * MEM_HEAD_DIM:(c + 1) * MEM_HEAD_DIM] = y.astype(o_ref.dtype)
    o_ref[:, half:] = kv[:, half:].astype(o_ref.dtype)


def _mem_kv(mem2d, g_mem_norm, w_mem_kv_bf, k_gain_row, batch):
    rows, d = mem2d.shape
    mlen = rows // batch
    n = w_mem_kv_bf.shape[1]
    return pl.pallas_call(
        _mem_kv_kernel,
        grid=(batch,),
        in_specs=[
            pl.BlockSpec((mlen, d), lambda b: (b, 0)),
            pl.BlockSpec((1, d), lambda b: (0, 0)),
            pl.BlockSpec((d, n), lambda b: (0, 0)),
            pl.BlockSpec((1, n // 2), lambda b: (0, 0)),
        ],
        out_specs=pl.BlockSpec((mlen, n), lambda b: (b, 0)),
        out_shape=jax.ShapeDtypeStruct((rows, n), BF16),
        compiler_params=_params(("arbitrary",)),
        name="mem_kv",
    )(mem2d, g_mem_norm.reshape(1, d), w_mem_kv_bf, k_gain_row)


def _mem_attn_kernel(q_ref, kv_ref, o_ref):
    half = kv_ref.shape[1] // 2
    for h in range(MEM_HEADS):
        lo, hi = h * MEM_HEAD_DIM, (h + 1) * MEM_HEAD_DIM
        s = lax.dot_general(q_ref[:, lo:hi], kv_ref[:, lo:hi], (((1,), (1,)), ((), ())),
                            preferred_element_type=F32)
        e = jnp.exp(s - jnp.max(s, axis=1, keepdims=True))
        p = e / jnp.sum(e, axis=1, keepdims=True)
        o = jnp.dot(p.astype(kv_ref.dtype), kv_ref[:, half + lo:half + hi], preferred_element_type=F32)
        o_ref[:, lo:hi] = o.astype(o_ref.dtype)


def _memory_attention(proj, kv, batch, seq, q_blk):
    tm = min(ROW_TILE, seq)
    per_batch = seq // tm
    mlen = kv.shape[0] // batch
    width = MEM_HEADS * MEM_HEAD_DIM
    return pl.pallas_call(
        _mem_attn_kernel,
        grid=(batch * per_batch,),
        in_specs=[
            pl.BlockSpec((tm, width), lambda i: (i, q_blk)),
            pl.BlockSpec((mlen, 2 * width), lambda i: (i // per_batch, 0)),
        ],
        out_specs=pl.BlockSpec((tm, width), lambda i: (i, 0)),
        out_shape=jax.ShapeDtypeStruct((batch * seq, width), BF16),
        compiler_params=_params(("arbitrary",)),
        name="memory_attention",
    )(proj, kv)


def _merge_kernel(od_ref, ol_ref, om_ref, wd_ref, wl_ref, wm_ref, gd_ref, gl_ref, gm_ref, o_ref):
    acc = gd_ref[...].astype(F32) * jnp.dot(od_ref[...], wd_ref[...], preferred_element_type=F32)
    acc += gl_ref[...].astype(F32) * jnp.dot(ol_ref[...], wl_ref[...], preferred_element_type=F32)
    acc += gm_ref[...].astype(F32) * jnp.dot(om_ref[...], wm_ref[...], preferred_element_type=F32)
    o_ref[...] = acc.astype(o_ref.dtype)


def _branch_merge(o_diff, o_dil, o_mem, wd, wl, wm, proj, gate_tile0, d_model):
    t = o_diff.shape[0]
    tm = min(ROW_TILE, t)
    tn = PROJ_TN
    per_branch = d_model // tn
    row = lambda n, m: (m, 0)
    col = lambda n, m: (0, n)
    gate = lambda br: (lambda n, m: (m, gate_tile0 + br * per_branch + n))
    return pl.pallas_call(
        _merge_kernel,
        grid=(per_branch, t // tm),
        in_specs=[
            pl.BlockSpec((tm, o_diff.shape[1]), row),
            pl.BlockSpec((tm, o_dil.shape[1]), row),
            pl.BlockSpec((tm, o_mem.shape[1]), row),
            pl.BlockSpec((wd.shape[0], tn), col),
            pl.BlockSpec((wl.shape[0], tn), col),
            pl.BlockSpec((wm.shape[0], tn), col),
            pl.BlockSpec((tm, tn), gate(0)),
            pl.BlockSpec((tm, tn), gate(1)),
            pl.BlockSpec((tm, tn), gate(2)),
        ],
        out_specs=pl.BlockSpec((tm, tn), lambda n, m: (m, n)),
        out_shape=jax.ShapeDtypeStruct((t, d_model), BF16),
        compiler_params=_params(("arbitrary", "arbitrary")),
        name="branch_merge",
    )(o_diff, o_dil, o_mem, wd, wl, wm, proj, proj, proj)


def _split_bf16(x):
    hi = x.astype(BF16)
    return hi, (x - hi.astype(F32)).astype(BF16)


def _pack_bf16_pairs(x):
    n = x.shape[1] // 2
    bits = lax.bitcast_convert_type(x.astype(F32), jnp.uint32)
    return (bits[:, :n] >> 16) | (bits[:, n:] & jnp.uint32(0xFFFF0000))


def _unpack_bf16_pairs(u):
    lo = lax.bitcast_convert_type(u << 16, F32).astype(BF16)
    hi = lax.bitcast_convert_type(u & jnp.uint32(0xFFFF0000), F32).astype(BF16)
    return lo, hi


def _out_proj_kernel(mg_ref, w_ref, x_ref, g_ref, wr_hi_ref, wr_lo_ref, br_ref, x1_ref, h2_ref, lg_ref):
    x1 = x_ref[...] + jnp.dot(mg_ref[...], w_ref[...], preferred_element_type=F32)
    x1_ref[...] = x1
    ms = jnp.mean(x1 * x1, axis=-1, keepdims=True)
    h2 = x1 * lax.rsqrt(ms + NORM_EPS) * g_ref[...]
    hi, lo = _split_bf16(h2)
    h2_ref[...] = _pack_bf16_pairs(hi)
    lg = (jnp.dot(hi, wr_hi_ref[...], preferred_element_type=F32)
          + jnp.dot(lo, wr_hi_ref[...], preferred_element_type=F32)
          + jnp.dot(hi, wr_lo_ref[...], preferred_element_type=F32))
    lane = lax.broadcasted_iota(jnp.int32, lg.shape, 1)
    lg_ref[...] = jnp.where(lane < N_EXPERTS, lg + br_ref[...], -jnp.inf)


def _output_projection(merged, w_out_bf, x2d, g_ffn, wr_hi, wr_lo, b_router_row):
    t, d = x2d.shape
    tm = min(ROW_TILE, t)
    row = lambda i: (i, 0)
    fixed = lambda i: (0, 0)
    return pl.pallas_call(
        _out_proj_kernel,
        grid=(t // tm,),
        in_specs=[
            pl.BlockSpec((tm, d), row),
            pl.BlockSpec((d, d), fixed),
            pl.BlockSpec((tm, d), row),
            pl.BlockSpec((1, d), fixed),
            pl.BlockSpec((d, LANES), fixed),
            pl.BlockSpec((d, LANES), fixed),
            pl.BlockSpec((1, LANES), fixed),
        ],
        out_specs=[pl.BlockSpec((tm, d), row), pl.BlockSpec((tm, d // 2), row), pl.BlockSpec((tm, LANES), row)],
        out_shape=[jax.ShapeDtypeStruct((t, d), F32), jax.ShapeDtypeStruct((t, d // 2), jnp.uint32),
                   jax.ShapeDtypeStruct((t, LANES), F32)],
        compiler_params=_params(("arbitrary",)),
        name="output_projection",
    )(merged, w_out_bf, x2d, g_ffn.reshape(1, d), wr_hi, wr_lo, b_router_row)


def _route_kernel(lg_ref, idx_ref, gate_ref, rank_ref, cnt_ref, carry_sc):
    i = pl.program_id(0)

    @pl.when(i == 0)
    def _():
        carry_sc[...] = jnp.zeros(carry_sc.shape, F32)

    work = lg_ref[...]
    tm = work.shape[0]
    lane = lax.broadcasted_iota(jnp.int32, work.shape, 1).astype(F32)
    vals, idxs = [], []
    member = jnp.zeros(work.shape, F32)
    for _ in range(TOP_K):
        mx = jnp.max(work, axis=1, keepdims=True)
        idx = jnp.min(jnp.where(work == mx, lane, float(LANES)), axis=1, keepdims=True)
        sel = lane == idx
        vals.append(mx)
        idxs.append(idx)
        member = jnp.where(sel, 1.0, member)
        work = jnp.where(sel, -jnp.inf, work)
    exps = [jnp.exp(v - vals[0]) for v in vals]
    z = exps[0] + exps[1] + exps[2] + exps[3]
    r = lax.broadcasted_iota(jnp.int32, (tm, tm), 0)
    c = lax.broadcasted_iota(jnp.int32, (tm, tm), 1)
    before = jnp.where(c < r, 1.0, 0.0).astype(BF16)
    cum = jnp.dot(before, member.astype(BF16), preferred_element_type=F32) + carry_sc[...]
    idx_out = jnp.zeros(work.shape, F32)
    gate_out = jnp.zeros(work.shape, F32)
    rank_out = jnp.zeros(work.shape, F32)
    for k in range(TOP_K):
        rank = jnp.sum(jnp.where(lane == idxs[k], cum, 0.0), axis=1, keepdims=True)
        idx_out = jnp.where(lane == float(k), idxs[k], idx_out)
        gate_out = jnp.where(lane == float(k), exps[k] / z, gate_out)
        rank_out = jnp.where(lane == float(k), rank, rank_out)
    idx_ref[...] = idx_out.astype(jnp.int32)
    gate_ref[...] = gate_out
    rank_ref[...] = rank_out.astype(jnp.int32)
    carry_sc[...] += jnp.sum(member, axis=0, keepdims=True)
    cnt_ref[...] = carry_sc[...].astype(jnp.int32)


def _route(logits):
    t = logits.shape[0]
    tm = min(ROW_TILE, t)
    row = lambda i: (i, 0)
    return pl.pallas_call(
        _route_kernel,
        grid=(t // tm,),
        in_specs=[pl.BlockSpec((tm, LANES), row)],
        out_specs=[pl.BlockSpec((tm, LANES), row), pl.BlockSpec((tm, LANES), row),
                   pl.BlockSpec((tm, LANES), row), pl.BlockSpec((1, LANES), lambda i: (0, 0))],
        out_shape=[jax.ShapeDtypeStruct((t, LANES), jnp.int32), jax.ShapeDtypeStruct((t, LANES), F32),
                   jax.ShapeDtypeStruct((t, LANES), jnp.int32), jax.ShapeDtypeStruct((1, LANES), jnp.int32)],
        scratch_shapes=[pltpu.VMEM((1, LANES), F32)],
        compiler_params=_params(("arbitrary",)),
        name="route",
    )(logits)


def _stream_expert_weights(sched, n_pass, copies, cast):
    be_ref, _, run_ref, next_ref, last_ref, nruns_ref = sched
    p = pl.program_id(0)
    m = pl.program_id(1)

    @pl.when((m == 0) | (be_ref[m] != be_ref[jnp.maximum(m - 1, 0)]))
    def _():
        @pl.when(p * nruns_ref[0] + run_ref[m] == 0)
        def _():
            for c in copies(be_ref[m], p):
                c.start()

        for c in copies(be_ref[m], p):
            c.wait()
        cast()
        is_last = last_ref[m] == 1

        @pl.when(jnp.logical_not(is_last) | (p + 1 < n_pass))
        def _():
            for c in copies(next_ref[m], jnp.where(is_last, p + 1, p)):
                c.start()


def _row_copy(src_hbm, row, dst, dst_row, sem):
    return pltpu.make_async_copy(src_hbm.at[pl.ds(row, 1)], dst.at[pl.ds(dst_row, 1)], sem)


def _dispatch_kernel(dest_ref, h_ref, init_hbm, o_hbm, sem):
    del init_hbm
    i = pl.program_id(0)
    rows = h_ref.shape[0]

    def copy(t, slot):
        return pltpu.make_async_copy(h_ref.at[pl.ds(t, 1)], o_hbm.at[pl.ds(slot, 1)], sem)

    def start(t, carry):
        for k in range(TOP_K):
            copy(t, dest_ref[(i * rows + t) * TOP_K + k]).start()
        return carry

    def wait(t, carry):
        for k in range(TOP_K):
            copy(t, 0).wait()
        return carry

    lax.fori_loop(0, rows, start, 0, unroll=DMA_LOOP_UNROLL)
    lax.fori_loop(0, rows, wait, 0, unroll=DMA_LOOP_UNROLL)


def _dispatch(h2_packed, dest_flat, n_slots):
    t, w = h2_packed.shape
    rows = min(COMBINE_ROWS, t)
    return pl.pallas_call(
        _dispatch_kernel,
        grid_spec=pltpu.PrefetchScalarGridSpec(
            num_scalar_prefetch=1,
            grid=(t // rows,),
            in_specs=[pl.BlockSpec((rows, w), lambda i, dest: (i, 0)), pl.BlockSpec(memory_space=pl.ANY)],
            out_specs=pl.BlockSpec(memory_space=pl.ANY),
            scratch_shapes=[pltpu.SemaphoreType.DMA(())],
        ),
        out_shape=jax.ShapeDtypeStruct((n_slots, w), h2_packed.dtype),
        input_output_aliases={2: 0},
        compiler_params=_params(("arbitrary",)),
        name="dispatch",
    )(dest_flat, h2_packed, jnp.zeros((n_slots, w), h2_packed.dtype))


def _gate_up_kernel(n_pass, be_ref, used_ref, run_ref, next_ref, last_ref, nruns_ref, x_ref, w_hbm, bg_ref,
                    bu_ref, o_ref, wbuf, wg_bf, wu_bf, sem):
    m = pl.program_id(1)
    tf = wg_bf.shape[1]
    d_ff = n_pass * tf

    def copies(e, p):
        gate_cols = pl.ds(pl.multiple_of(p * tf, tf), tf)
        up_cols = pl.ds(pl.multiple_of(d_ff + p * tf, tf), tf)
        return [pltpu.make_async_copy(w_hbm.at[e, :, gate_cols], wbuf.at[0], sem),
                pltpu.make_async_copy(w_hbm.at[e, :, up_cols], wbuf.at[1], sem)]

    def cast():
        wg_bf[...] = wbuf[0].astype(BF16)
        wu_bf[...] = wbuf[1].astype(BF16)

    @pl.when(m < used_ref[0])
    def _():
        _stream_expert_weights((be_ref, used_ref, run_ref, next_ref, last_ref, nruns_ref), n_pass, copies, cast)

        for rows in _row_blocks(x_ref.shape[0]):
            lo, hi = _unpack_bf16_pairs(x_ref[rows, :])
            half = lo.shape[1]
            gate = (jnp.dot(lo, wg_bf[:half, :], preferred_element_type=F32)
                    + jnp.dot(hi, wg_bf[half:, :], preferred_element_type=F32) + bg_ref[...])
            up = (jnp.dot(lo, wu_bf[:half, :], preferred_element_type=F32)
                  + jnp.dot(hi, wu_bf[half:, :], preferred_element_type=F32) + bu_ref[...])
            gate = jnp.minimum(gate, SWIGLU_LIMIT)
            up = jnp.clip(up, -SWIGLU_LIMIT, SWIGLU_LIMIT)
            act = (up + 1.0) * gate * (1.0 / (1.0 + jnp.exp(-SWIGLU_ALPHA * gate)))
            o_ref[rows, :] = act.astype(o_ref.dtype)

    @pl.when(m >= used_ref[0])
    def _():
        o_ref[...] = jnp.zeros(o_ref.shape, o_ref.dtype)


def _gate_up(xs, w_gate_up, b_gate_up, sched):
    p = xs.shape[0]
    d = w_gate_up.shape[1]
    d_ff = w_gate_up.shape[2] // 2
    rows, tf = EXPERT_ROWS, FF_TILE
    nf = d_ff // tf
    blk = lambda m, used: jnp.minimum(m, used[0] - 1)
    return pl.pallas_call(
        functools.partial(_gate_up_kernel, nf),
        grid_spec=pltpu.PrefetchScalarGridSpec(
            num_scalar_prefetch=len(sched),
            grid=(nf, p // rows),
            in_specs=[
                pl.BlockSpec((rows, xs.shape[1]), lambda f, m, be, used, *_: (blk(m, used), 0)),
                pl.BlockSpec(memory_space=pl.ANY),
                pl.BlockSpec((None, 1, tf), lambda f, m, be, used, *_: (be[blk(m, used)], 0, f)),
                pl.BlockSpec((None, 1, tf), lambda f, m, be, used, *_: (be[blk(m, used)], 0, nf + f)),
            ],
            out_specs=pl.BlockSpec((rows, tf), lambda f, m, *_: (m, f)),
            scratch_shapes=[pltpu.VMEM((2, d, tf), F32), pltpu.VMEM((d, tf), BF16), pltpu.VMEM((d, tf), BF16),
                            pltpu.SemaphoreType.DMA(())],
        ),
        out_shape=jax.ShapeDtypeStruct((p, d_ff), BF16),
        compiler_params=_params(("arbitrary", "arbitrary")),
        name="expert_gate_up",
    )(*sched, xs, w_gate_up, b_gate_up, b_gate_up)


def _down_kernel(n_pass, be_ref, used_ref, run_ref, next_ref, last_ref, nruns_ref, a_ref, w_hbm, b_ref, o_ref,
                 wbuf, w_bf, sem):
    m = pl.program_id(1)
    tn = w_bf.shape[1]

    def copies(e, p):
        cols = pl.ds(pl.multiple_of(p * tn, tn), tn)
        return [pltpu.make_async_copy(w_hbm.at[e, :, cols], wbuf, sem)]

    def cast():
        w_bf[...] = wbuf[...].astype(BF16)

    @pl.when(m < used_ref[0])
    def _():
        _stream_expert_weights((be_ref, used_ref, run_ref, next_ref, last_ref, nruns_ref), n_pass, copies, cast)

        for rows in _row_blocks(a_ref.shape[0]):
            o_ref[rows, :] = jnp.dot(a_ref[rows, :], w_bf[...], preferred_element_type=F32) + b_ref[...]

    @pl.when(m >= used_ref[0])
    def _():
        o_ref[...] = jnp.zeros(o_ref.shape, o_ref.dtype)


def _down(act, w_down, b_down, sched):
    p, d_ff = act.shape
    d = w_down.shape[2]
    rows, tn = EXPERT_ROWS, DOWN_TN
    blk = lambda m, used: jnp.minimum(m, used[0] - 1)
    return pl.pallas_call(
        functools.partial(_down_kernel, d // tn),
        grid_spec=pltpu.PrefetchScalarGridSpec(
            num_scalar_prefetch=len(sched),
            grid=(d // tn, p // rows),
            in_specs=[
                pl.BlockSpec((rows, d_ff), lambda n, m, be, used, *_: (blk(m, used), 0)),
                pl.BlockSpec(memory_space=pl.ANY),
                pl.BlockSpec((None, 1, tn), lambda n, m, be, used, *_: (be[blk(m, used)], 0, n)),
            ],
            out_specs=pl.BlockSpec((rows, tn), lambda n, m, *_: (m, n)),
            scratch_shapes=[pltpu.VMEM((d_ff, tn), F32), pltpu.VMEM((d_ff, tn), BF16),
                            pltpu.SemaphoreType.DMA(())],
        ),
        out_shape=jax.ShapeDtypeStruct((p, d), F32),
        compiler_params=_params(("arbitrary", "arbitrary")),
        name="expert_down",
    )(*sched, act, w_down, b_down)


def _combine_kernel(dest_ref, x_ref, g_ref, y_hbm, o_ref, buf, sem):
    i = pl.program_id(0)
    rows = x_ref.shape[0]

    def start(t, carry):
        for k in range(TOP_K):
            _row_copy(y_hbm, dest_ref[(i * rows + t) * TOP_K + k], buf.at[k], t, sem).start()
        return carry

    def wait(t, carry):
        for k in range(TOP_K):
            _row_copy(y_hbm, 0, buf.at[k], t, sem).wait()
        return carry

    lax.fori_loop(0, rows, start, 0, unroll=DMA_LOOP_UNROLL)
    lax.fori_loop(0, rows, wait, 0, unroll=DMA_LOOP_UNROLL)
    g = g_ref[...]
    moe = ((g[:, 0:1] * buf[0] + g[:, 1:2] * buf[1]) + (g[:, 2:3] * buf[2] + g[:, 3:4] * buf[3]))
    o_ref[...] = x_ref[...] + moe


def _combine(x1, gates_padded, y, dest_flat):
    t, d = x1.shape
    rows = min(COMBINE_ROWS, t)
    return pl.pallas_call(
        _combine_kernel,
        grid_spec=pltpu.PrefetchScalarGridSpec(
            num_scalar_prefetch=1,
            grid=(t // rows,),
            in_specs=[pl.BlockSpec((rows, d), lambda i, dest: (i, 0)),
                      pl.BlockSpec((rows, LANES), lambda i, dest: (i, 0)),
                      pl.BlockSpec(memory_space=pl.ANY)],
            out_specs=pl.BlockSpec((rows, d), lambda i, dest: (i, 0)),
            scratch_shapes=[pltpu.VMEM((TOP_K, rows, d), F32), pltpu.SemaphoreType.DMA(())],
        ),
        out_shape=jax.ShapeDtypeStruct((t, d), F32),
        compiler_params=_params(("arbitrary",)),
        name="combine",
    )(dest_flat, x1, gates_padded, y)


def _moe(x1, h2, logits, w_gate_up, b_gate_up, w_down, b_down):
    t, d = x1.shape
    n_exp = w_gate_up.shape[0]
    rows = EXPERT_ROWS
    idx_p, gate_p, rank_p, cnt_p = _route(logits)
    idx, rank = idx_p[:, :TOP_K], rank_p[:, :TOP_K]
    counts = cnt_p[0, :n_exp]

    padded = (counts + rows - 1) // rows * rows
    pend = jnp.cumsum(padded)
    pstart = pend - padded
    n_blocks = (t * TOP_K + n_exp * (rows - 1) + rows - 1) // rows
    dest_flat = (pstart[idx] + rank).astype(jnp.int32).reshape(-1)
    block_start = jnp.arange(n_blocks, dtype=jnp.int32) * rows
    block_e = jnp.minimum(jnp.sum(pend[None, :] <= block_start[:, None], axis=1), n_exp - 1).astype(jnp.int32)
    n_used = (pend[-1] // rows).astype(jnp.int32).reshape(1)
    has_rows = counts > 0
    expert_run = (jnp.cumsum(has_rows.astype(jnp.int32)) - 1).astype(jnp.int32)
    n_runs = jnp.sum(has_rows.astype(jnp.int32)).reshape(1)
    run_id = expert_run[block_e]
    is_last_run = (run_id == n_runs[0] - 1).astype(jnp.int32)
    following = jnp.where(is_last_run == 1, 0, run_id + 1)
    expert_ids = jnp.arange(n_exp, dtype=jnp.int32)
    next_e = jnp.sum(jnp.where(has_rows[None, :] & (expert_run[None, :] == following[:, None]),
                               expert_ids[None, :], 0), axis=1).astype(jnp.int32)
    sched = (block_e, n_used, run_id, next_e, is_last_run, n_runs)

    xs = _dispatch(h2, dest_flat, n_blocks * rows)
    act = _gate_up(xs, w_gate_up, b_gate_up.reshape(n_exp, 1, -1), sched)
    y = _down(act, w_down, b_down.reshape(n_exp, 1, -1), sched)
    return _combine(x1, gate_p, y, dest_flat)


def _rope_tables(positions):
    half = ROT_DIM // 2
    inv_freq = ROPE_THETA ** (-jnp.arange(0, ROT_DIM, 2, dtype=F32) / ROT_DIM)
    ang = positions.reshape(-1).astype(F32)[:, None] * inv_freq
    cos, sin = jnp.cos(ang), jnp.sin(ang)
    t = ang.shape[0]
    cos_t = jnp.concatenate([cos, cos, jnp.ones((t, HEAD - ROT_DIM), F32)], axis=1)
    sa_t = jnp.concatenate([-sin, jnp.zeros((t, HEAD - half), F32)], axis=1)
    sb_t = jnp.concatenate([jnp.zeros((t, half), F32), sin, jnp.zeros((t, HEAD - ROT_DIM), F32)], axis=1)
    return cos_t, sa_t, sb_t


def _layer(x2d, mem2d, batch, seq, cos_t, sa_t, sb_t, layer_idx, g_mix_norm, w_in, b_gate, diff_q_norm,
           diff_k_norm, diff_lambda, diff_subln, dil_q_norm, dil_k_norm, g_mem_norm, w_mem_kv, mem_q_norm,
           mem_k_norm, w_branch_diff, w_branch_dil, w_branch_mem, w_out, g_ffn_norm, w_router, b_router,
           w_gate_up, b_gate_up, w_down, b_down):
    d = x2d.shape[1]
    diff_cols = DIFF_HEADS * 2 * HEAD
    dil_cols = DIL_HEADS * HEAD
    mem_cols = MEM_HEADS * MEM_HEAD_DIM
    off_dq, off_dk, off_dv = 0, diff_cols, 2 * diff_cols
    off_lq = 3 * diff_cols
    off_lk, off_lv = off_lq + dil_cols, off_lq + 2 * dil_cols
    off_mq = off_lq + 3 * dil_cols
    off_gate = off_mq + mem_cols
    n_cols = off_gate + 3 * d
    assert w_in.shape == (d, n_cols)
    tn = PROJ_TN
    tiles = lambda off, width: tuple(range(off // tn, (off + width) // tn))
    rope_tiles = (tiles(off_dq, diff_cols) + tiles(off_dk, diff_cols) + tiles(off_lq, dil_cols)
                  + tiles(off_lk, dil_cols))

    rep = lambda g, width: jnp.tile(g.astype(F32), width // g.shape[0])
    diff_q_gain = diff_q_norm.astype(F32) * (HEAD ** -0.5 * LOG2E)
    dil_q_gain = dil_q_norm.astype(F32) * (HEAD ** -0.5 * LOG2E)

    def score_bound(q_gain, k_gain):
        return (1.01 * HEAD * jnp.max(jnp.abs(q_gain)) * jnp.max(jnp.abs(k_gain.astype(F32)))).reshape(1)

    diff_bound = score_bound(diff_q_gain, diff_k_norm)
    dil_bound = score_bound(dil_q_gain, dil_k_norm)
    gain_row = jnp.concatenate([
        rep(diff_q_gain, diff_cols), rep(diff_k_norm, diff_cols), jnp.ones((diff_cols,), F32),
        rep(dil_q_gain, dil_cols), rep(dil_k_norm, dil_cols), jnp.ones((dil_cols,), F32),
        rep(mem_q_norm, mem_cols) * MEM_HEAD_DIM ** -0.5, jnp.ones((3 * d,), F32)]).reshape(1, n_cols)
    bias_row = jnp.concatenate([jnp.zeros((off_gate,), F32), b_gate.astype(F32)]).reshape(1, n_cols)

    h = _rmsnorm(x2d, g_mix_norm, BF16)
    proj = _input_projection(h, w_in, gain_row, bias_row, cos_t, sa_t, sb_t, rope_tiles, off_mq // tn,
                             off_gate // tn)

    lam_init = 0.8 - 0.6 * float(np.exp(-0.3 * layer_idx))
    o_diff = _diff_attention(proj, batch, seq, off_dq // DIFF_V_DIM, off_dk // DIFF_V_DIM, off_dv // DIFF_V_DIM,
                             diff_bound, diff_lambda.astype(F32), diff_subln, lam_init)
    o_dil = _dilated_attention(proj, batch, seq, off_lq // HEAD, off_lk // HEAD, off_lv // HEAD, dil_bound)
    kv = _mem_kv(mem2d, g_mem_norm, w_mem_kv.astype(BF16), rep(mem_k_norm, mem_cols).reshape(1, mem_cols), batch)
    o_mem = _memory_attention(proj, kv, batch, seq, off_mq // mem_cols)

    merged = _branch_merge(o_diff, o_dil, o_mem, w_branch_diff.astype(BF16), w_branch_dil.astype(BF16),
                           w_branch_mem.astype(BF16), proj, off_gate // tn, d)

    n_exp = w_router.shape[1]
    wr = jnp.zeros((d, LANES), F32).at[:, :n_exp].set(w_router)
    wr_hi, wr_lo = _split_bf16(wr)
    br_row = jnp.zeros((1, LANES), F32).at[0, :n_exp].set(b_router)
    x1, h2, logits = _output_projection(merged, w_out.astype(BF16), x2d, g_ffn_norm, wr_hi, wr_lo, br_row)
    return _moe(x1, h2, logits, w_gate_up, b_gate_up, w_down, b_down)


def kernel(x, mem, positions, g_mix_norm, w_in, b_gate, diff_q_norm, diff_k_norm, diff_lambda, diff_subln,
           dil_q_norm, dil_k_norm, g_mem_norm, w_mem_kv, mem_q_norm, mem_k_norm, w_branch_diff, w_branch_dil,
           w_branch_mem, w_out, g_ffn_norm, w_router, b_router, w_gate_up, b_gate_up, w_down, b_down):
    batch, seq, d = x.shape
    cos_t, sa_t, sb_t = _rope_tables(positions)
    x2d = x.reshape(batch * seq, d)
    mem2d = mem.reshape(-1, d)
    for l in range(g_mix_norm.shape[0]):
        x2d = _layer(x2d, mem2d, batch, seq, cos_t, sa_t, sb_t, l, g_mix_norm[l], w_in[l], b_gate[l],
                     diff_q_norm[l], diff_k_norm[l], diff_lambda[l], diff_subln[l], dil_q_norm[l],
                     dil_k_norm[l], g_mem_norm[l], w_mem_kv[l], mem_q_norm[l], mem_k_norm[l],
                     w_branch_diff[l], w_branch_dil[l], w_branch_mem[l], w_out[l], g_ffn_norm[l],
                     w_router[l], b_router[l], w_gate_up[l], b_gate_up[l], w_down[l], b_down[l])
    return x2d.reshape(batch, seq, d)
```

```python
import functools

import numpy as np
import jax
import jax.numpy as jnp
from jax import lax
from jax.experimental import pallas as pl
from jax.experimental.pallas import tpu as pltpu

F32 = jnp.float32
BF16 = jnp.bfloat16

NORM_EPS = 1e-6
NEG_INF = -1e30
ROPE_THETA = 500000.0
ROT_DIM = 32
HEAD = 128
DIFF_HEADS = 8
DIFF_V_DIM = 256
DIL_HEADS = 8
DIL_CONFIGS = ((128, 1), (512, 4), (2048, 16))
DIL_PAD = 2048
MEM_HEADS = 4
MEM_HEAD_DIM = 256
N_EXPERTS = 32
TOP_K = 4
SWIGLU_LIMIT = 7.0
SWIGLU_ALPHA = 1.702

LANES = 128
VMEM_LIMIT = 56 * 1024 * 1024

PROJ_TN = 1024
ROW_TILE = 512
DIFF_TQ = 1024
DIFF_TK = 1024
DIFF_SUB = 256
DIFF_FIXED_SHIFT_MAX = 40.0
LOG2E = 1.4426950408889634
DIL_TQ = 256
DIL_HEADS_PER_STEP = 2
EXPERT_ROWS = 512
FF_TILE = 1024
DOWN_TN = 2048
COMBINE_ROWS = 256
DMA_LOOP_UNROLL = 8
MATMUL_ROW_BLOCK = 256
PROJ_TM = 1024


def _params(sem, **kw):
    return pltpu.CompilerParams(dimension_semantics=sem, vmem_limit_bytes=VMEM_LIMIT, **kw)


def _rmsnorm_kernel(x_ref, g_ref, o_ref):
    x = x_ref[...]
    ms = jnp.mean(x * x, axis=-1, keepdims=True)
    o_ref[...] = (x * lax.rsqrt(ms + NORM_EPS) * g_ref[...]).astype(o_ref.dtype)


def _rmsnorm(x, g, out_dtype):
    t, d = x.shape
    tm = min(ROW_TILE, t)
    return pl.pallas_call(
        _rmsnorm_kernel,
        grid=(t // tm,),
        in_specs=[pl.BlockSpec((tm, d), lambda i: (i, 0)), pl.BlockSpec((1, d), lambda i: (0, 0))],
        out_specs=pl.BlockSpec((tm, d), lambda i: (i, 0)),
        out_shape=jax.ShapeDtypeStruct((t, d), out_dtype),
        compiler_params=_params(("arbitrary",)),
        name="rmsnorm",
    )(x, g.reshape(1, d))


def _row_blocks(rows):
    size = min(MATMUL_ROW_BLOCK, rows)
    return [slice(r, r + size) for r in range(0, rows, size)]


def _chunk_norm(x, gain, width):
    outs = []
    for c in range(x.shape[1] // width):
        xc = x[:, c * width:(c + 1) * width]
        ms = jnp.mean(xc * xc, axis=-1, keepdims=True)
        outs.append(xc * lax.rsqrt(ms + NORM_EPS) * gain[:, c * width:(c + 1) * width])
    return outs


def _proj_kernel(rope_tiles, mem_tile, gate_tile0, h_ref, w_ref, gain_ref, bias_ref, cos_ref, sa_ref,
                 sb_ref, o_ref, wbf_ref):
    n = pl.program_id(0)
    m = pl.program_id(1)

    @pl.when(m == 0)
    def _():
        wbf_ref[...] = w_ref[...].astype(BF16)

    is_rope = functools.reduce(jnp.logical_or, [n == t for t in rope_tiles])
    is_mem = n == mem_tile
    is_gate = n >= gate_tile0

    def for_row_blocks(epilogue):
        for rows in _row_blocks(h_ref.shape[0]):
            epilogue(rows, jnp.dot(h_ref[rows, :], wbf_ref[...], preferred_element_type=F32))

    def rope(rows, acc):
        cos, sa, sb = cos_ref[rows, :], sa_ref[rows, :], sb_ref[rows, :]
        for c, y in enumerate(_chunk_norm(acc, gain_ref[...], HEAD)):
            y = y * cos + pltpu.roll(y, HEAD - ROT_DIM // 2, 1) * sa + pltpu.roll(y, ROT_DIM // 2, 1) * sb
            o_ref[rows, c * HEAD:(c + 1) * HEAD] = y.astype(o_ref.dtype)

    def mem_q(rows, acc):
        for c, y in enumerate(_chunk_norm(acc, gain_ref[...], MEM_HEAD_DIM)):
            o_ref[rows, c * MEM_HEAD_DIM:(c + 1) * MEM_HEAD_DIM] = y.astype(o_ref.dtype)

    def gate(rows, acc):
        z = acc + bias_ref[...]
        o_ref[rows, :] = (1.0 / (1.0 + jnp.exp(-z))).astype(o_ref.dtype)

    def plain(rows, acc):
        o_ref[rows, :] = acc.astype(o_ref.dtype)

    pl.when(is_rope)(lambda: for_row_blocks(rope))
    pl.when(is_mem)(lambda: for_row_blocks(mem_q))
    pl.when(is_gate)(lambda: for_row_blocks(gate))
    pl.when(jnp.logical_not(is_rope | is_mem | is_gate))(lambda: for_row_blocks(plain))


def _input_projection(h, w_in, gain_row, bias_row, cos_t, sa_t, sb_t, rope_tiles, mem_tile, gate_tile0):
    t, d = h.shape
    n_cols = w_in.shape[1]
    tm = min(PROJ_TM, t)
    tn = PROJ_TN
    row = lambda n, m: (m, 0)
    col = lambda n, m: (0, n)
    return pl.pallas_call(
        functools.partial(_proj_kernel, rope_tiles, mem_tile, gate_tile0),
        grid=(n_cols // tn, t // tm),
        in_specs=[
            pl.BlockSpec((tm, d), row),
            pl.BlockSpec((d, tn), col),
            pl.BlockSpec((1, tn), col),
            pl.BlockSpec((1, tn), col),
            pl.BlockSpec((tm, HEAD), row),
            pl.BlockSpec((tm, HEAD), row),
            pl.BlockSpec((tm, HEAD), row),
        ],
        out_specs=pl.BlockSpec((tm, tn), lambda n, m: (m, n)),
        out_shape=jax.ShapeDtypeStruct((t, n_cols), BF16),
        scratch_shapes=[pltpu.VMEM((d, tn), BF16)],
        compiler_params=_params(("arbitrary", "arbitrary")),
        name="input_projection",
    )(h, w_in, gain_row, bias_row, cos_t, sa_t, sb_t)


def _diff_attn_kernel(lam_init, bound_ref, q_ref, k_ref, v_ref, lam_ref, subln_ref, o_ref, m_sc, l_sc, acc_sc):
    qi = pl.program_id(2)
    tq = q_ref.shape[0]
    sub = min(DIFF_SUB, tq)
    tk = min(DIFF_TK, tq)
    per_tile = tq // tk
    bound = bound_ref[0]
    l_sc[...] = jnp.zeros(l_sc.shape, F32)
    acc_sc[...] = jnp.zeros(acc_sc.shape, F32)

    def row_plan(d):
        plan = []
        for r in range(tq // sub):
            if d is None or r * sub >= (d + 1) * tk:
                plan.append((r, tk, None))
            elif (r + 1) * sub > d * tk:
                plan.append((r, min(tk, (r + 1) * sub - d * tk), r * sub - d * tk))
        return plan

    def scores(start, c, r, nk, offset):
        s = lax.dot_general(q_ref[r * sub:(r + 1) * sub, c * HEAD:(c + 1) * HEAD],
                            k_ref[pl.ds(start, nk), c * HEAD:(c + 1) * HEAD],
                            (((1,), (1,)), ((), ())), preferred_element_type=F32)
        if offset is not None:
            row = lax.broadcasted_iota(jnp.int32, s.shape, 0) + offset
            col = lax.broadcasted_iota(jnp.int32, s.shape, 1)
            s = jnp.where(row >= col, s, NEG_INF)
        return s

    def fixed_step(start, plan):
        for c in range(2):
            for r, nk, offset in plan:
                rows = slice(r * sub, (r + 1) * sub)
                p = jnp.exp2(scores(start, c, r, nk, offset) - bound)
                l_sc[c, rows] += functools.reduce(
                    jnp.add, [p[:, i * LANES:(i + 1) * LANES] for i in range(nk // LANES)])
                acc_sc[c, rows] += jnp.dot(p.astype(v_ref.dtype), v_ref[pl.ds(start, nk), :],
                                           preferred_element_type=F32)

    def online_step(start, plan):
        for c in range(2):
            for r, nk, offset in plan:
                rows = slice(r * sub, (r + 1) * sub)
                s = scores(start, c, r, nk, offset)
                m_prev = m_sc[c, rows]
                m_new = jnp.maximum(m_prev, jnp.max(s, axis=1, keepdims=True))
                alpha = jnp.exp2(m_prev - m_new)
                p = jnp.exp2(s - jnp.tile(m_new, (1, nk // LANES)))
                l_sc[c, rows] = alpha * l_sc[c, rows] + jnp.sum(p, axis=1, keepdims=True)
                pv = jnp.dot(p.astype(v_ref.dtype), v_ref[pl.ds(start, nk), :], preferred_element_type=F32)
                acc_sc[c, rows] = jnp.tile(alpha, (1, acc_sc.shape[2] // LANES)) * acc_sc[c, rows] + pv
                m_sc[c, rows] = m_new

    def sweep(step):
        def body(j, carry):
            step(pl.multiple_of(j * tk, tk), row_plan(None))
            return carry

        lax.fori_loop(0, qi * per_tile, body, 0)
        for d in range(per_tile):
            step(pl.multiple_of((qi * per_tile + d) * tk, tk), row_plan(d))

    def finish(l0, l1):
        lp = lam_ref[...]
        lam = (jnp.exp(jnp.sum(lp[0:1] * lp[1:2], axis=1, keepdims=True))
               - jnp.exp(jnp.sum(lp[2:3] * lp[3:4], axis=1, keepdims=True)) + lam_init)
        o = acc_sc[0] / l0 - lam * (acc_sc[1] / l1)
        ms = jnp.mean(o * o, axis=-1, keepdims=True)
        o = o * lax.rsqrt(ms + NORM_EPS) * subln_ref[...] * (1.0 - lam_init)
        o_ref[...] = o.astype(o_ref.dtype)

    @pl.when(bound <= DIFF_FIXED_SHIFT_MAX)
    def _():
        sweep(fixed_step)
        finish(jnp.sum(l_sc[0], axis=1, keepdims=True), jnp.sum(l_sc[1], axis=1, keepdims=True))

    @pl.when(bound > DIFF_FIXED_SHIFT_MAX)
    def _():
        m_sc[...] = jnp.full(m_sc.shape, NEG_INF, F32)
        sweep(online_step)
        finish(l_sc[0][:, :1], l_sc[1][:, :1])


def _diff_attention(proj, batch, seq, q_blk0, k_blk0, v_blk0, score_bound, diff_lambda, subln, lam_init):
    tq = min(DIFF_TQ, seq)
    nq = seq // tq
    dv = DIFF_V_DIM
    return pl.pallas_call(
        functools.partial(_diff_attn_kernel, lam_init),
        grid_spec=pltpu.PrefetchScalarGridSpec(
            num_scalar_prefetch=1,
            grid=(batch, DIFF_HEADS, nq),
            in_specs=[
                pl.BlockSpec((tq, dv), lambda b, h, i, bound: (b * nq + i, q_blk0 + h)),
                pl.BlockSpec((seq, dv), lambda b, h, i, bound: (b, k_blk0 + h)),
                pl.BlockSpec((seq, dv), lambda b, h, i, bound: (b, v_blk0 + h)),
                pl.BlockSpec((4, HEAD), lambda b, h, i, bound: (0, 0)),
                pl.BlockSpec((1, dv), lambda b, h, i, bound: (0, 0)),
            ],
            out_specs=pl.BlockSpec((tq, dv), lambda b, h, i, bound: (b * nq + i, h)),
            scratch_shapes=[pltpu.VMEM((2, tq, LANES), F32), pltpu.VMEM((2, tq, LANES), F32),
                            pltpu.VMEM((2, tq, dv), F32)],
        ),
        out_shape=jax.ShapeDtypeStruct((batch * seq, DIFF_HEADS * dv), BF16),
        compiler_params=_params(("arbitrary", "arbitrary", "arbitrary")),
        name="diff_attention",
    )(score_bound, proj, proj, proj, diff_lambda, subln.reshape(1, dv))


def _dilated_multiplicity(tq):
    n_chunks = DIL_PAD // tq + 1
    q = np.arange(tq)[:, None]
    k = np.arange(tq)[None, :]
    out = np.zeros((n_chunks + 1, tq, tq), np.float32)
    for c in range(n_chunks):
        d = DIL_PAD - c * tq + q - k
        for window, dil in DIL_CONFIGS:
            out[c] += ((d >= 0) & (d <= window) & (d % dil == 0)).astype(np.float32)
    return out


def _dil_attn_kernel(bound_ref, q_ref, k_ref, v_ref, mult_ref, o_ref):
    qi = pl.program_id(2)
    tq = q_ref.shape[0]
    n_chunks = mult_ref.shape[0] - 1
    bound = bound_ref[0]
    heads = [slice(h * HEAD, (h + 1) * HEAD) for h in range(q_ref.shape[1] // HEAD)]

    def chunk(c, cols):
        blk = qi - (n_chunks - 1) + c
        start = pl.multiple_of(jnp.maximum(blk, 0) * tq, tq)
        w = mult_ref[jnp.where(blk >= 0, c, n_chunks)]
        s = lax.dot_general(q_ref[:, cols], k_ref[pl.ds(start, tq), cols], (((1,), (1,)), ((), ())),
                            preferred_element_type=F32)
        return start, w, s

    def pv(p, start, cols):
        return jnp.dot(p.astype(v_ref.dtype), v_ref[pl.ds(start, tq), cols], preferred_element_type=F32)

    @pl.when(bound <= DIFF_FIXED_SHIFT_MAX)
    def _():
        for cols in heads:
            den = jnp.zeros((tq, LANES), F32)
            acc = jnp.zeros((tq, HEAD), F32)
            for c in range(n_chunks):
                start, w, s = chunk(c, cols)
                p = w * jnp.exp2(s - bound)
                den += functools.reduce(jnp.add, [p[:, i * LANES:(i + 1) * LANES] for i in range(tq // LANES)])
                acc += pv(p, start, cols)
            o_ref[:, cols] = (acc / jnp.sum(den, axis=1, keepdims=True)).astype(o_ref.dtype)

    @pl.when(bound > DIFF_FIXED_SHIFT_MAX)
    def _():
        for cols in heads:
            chunks = [chunk(c, cols) for c in range(n_chunks)]
            scores = [jnp.where(w > 0.0, s, NEG_INF) for _, w, s in chunks]
            m = jnp.max(functools.reduce(jnp.maximum, scores), axis=1, keepdims=True)
            den = jnp.zeros((tq, 1), F32)
            acc = jnp.zeros((tq, HEAD), F32)
            for (start, w, _), s in zip(chunks, scores):
                p = w * jnp.exp2(s - m)
                den += jnp.sum(p, axis=1, keepdims=True)
                acc += pv(p, start, cols)
            o_ref[:, cols] = (acc / den).astype(o_ref.dtype)


def _dilated_attention(proj, batch, seq, q_blk0, k_blk0, v_blk0, score_bound):
    tq = DIL_TQ
    nq = seq // tq
    hps = DIL_HEADS_PER_STEP
    width = hps * HEAD
    assert q_blk0 % hps == 0 and k_blk0 % hps == 0 and v_blk0 % hps == 0
    mult = jnp.asarray(_dilated_multiplicity(tq))
    return pl.pallas_call(
        _dil_attn_kernel,
        grid_spec=pltpu.PrefetchScalarGridSpec(
            num_scalar_prefetch=1,
            grid=(batch, DIL_HEADS // hps, nq),
            in_specs=[
                pl.BlockSpec((tq, width), lambda b, h, i, bound: (b * nq + i, q_blk0 // hps + h)),
                pl.BlockSpec((seq, width), lambda b, h, i, bound: (b, k_blk0 // hps + h)),
                pl.BlockSpec((seq, width), lambda b, h, i, bound: (b, v_blk0 // hps + h)),
                pl.BlockSpec(mult.shape, lambda b, h, i, bound: (0, 0, 0)),
            ],
            out_specs=pl.BlockSpec((tq, width), lambda b, h, i, bound: (b * nq + i, h)),
        ),
        out_shape=jax.ShapeDtypeStruct((batch * seq, DIL_HEADS * HEAD), BF16),
        compiler_params=_params(("arbitrary", "arbitrary", "arbitrary")),
        name="dilated_attention",
    )(score_bound, proj, proj, proj, mult)


def _mem_kv_kernel(mem_ref, g_ref, w_ref, kn_ref, o_ref):
    x = mem_ref[...]
    ms = jnp.mean(x * x, axis=-1, keepdims=True)
    h = (x * lax.rsqrt(ms + NORM_EPS) * g_ref[...]).astype(BF16)
    kv = jnp.dot(h, w_ref[...], preferred_element_type=F32)
    half = kv.shape[1] // 2
    for c, y in enumerate(_chunk_norm(kv[:, :half], kn_ref[...], MEM_HEAD_DIM)):
        o_ref[:, c * MEM_HEAD_DIM:(c + 1) * MEM_HEAD_DIM] = y.astype(o_ref.dtype)
    o_ref[:, half:] = kv[:, half:].astype(o_ref.dtype)


def _mem_kv(mem2d, g_mem_norm, w_mem_kv_bf, k_gain_row, batch):
    rows, d = mem2d.shape
    mlen = rows // batch
    n = w_mem_kv_bf.shape[1]
    return pl.pallas_call(
        _mem_kv_kernel,
        grid=(batch,),
        in_specs=[
            pl.BlockSpec((mlen, d), lambda b: (b, 0)),
            pl.BlockSpec((1, d), lambda b: (0, 0)),
            pl.BlockSpec((d, n), lambda b: (0, 0)),
            pl.BlockSpec((1, n // 2), lambda b: (0, 0)),
        ],
        out_specs=pl.BlockSpec((mlen, n), lambda b: (b, 0)),
        out_shape=jax.ShapeDtypeStruct((rows, n), BF16),
        compiler_params=_params(("arbitrary",)),
        name="mem_kv",
    )(mem2d, g_mem_norm.reshape(1, d), w_mem_kv_bf, k_gain_row)


def _mem_attn_kernel(q_ref, kv_ref, o_ref):
    half = kv_ref.shape[1] // 2
    for h in range(MEM_HEADS):
        lo, hi = h * MEM_HEAD_DIM, (h + 1) * MEM_HEAD_DIM
        s = lax.dot_general(q_ref[:, lo:hi], kv_ref[:, lo:hi], (((1,), (1,)), ((), ())),
                            preferred_element_type=F32)
        e = jnp.exp(s - jnp.max(s, axis=1, keepdims=True))
        p = e / jnp.sum(e, axis=1, keepdims=True)
        o = jnp.dot(p.astype(kv_ref.dtype), kv_ref[:, half + lo:half + hi], preferred_element_type=F32)
        o_ref[:, lo:hi] = o.astype(o_ref.dtype)


def _memory_attention(proj, kv, batch, seq, q_blk):
    tm = min(ROW_TILE, seq)
    per_batch = seq // tm
    mlen = kv.shape[0] // batch
    width = MEM_HEADS * MEM_HEAD_DIM
    return pl.pallas_call(
        _mem_attn_kernel,
        grid=(batch * per_batch,),
        in_specs=[
            pl.BlockSpec((tm, width), lambda i: (i, q_blk)),
            pl.BlockSpec((mlen, 2 * width), lambda i: (i // per_batch, 0)),
        ],
        out_specs=pl.BlockSpec((tm, width), lambda i: (i, 0)),
        out_shape=jax.ShapeDtypeStruct((batch * seq, width), BF16),
        compiler_params=_params(("arbitrary",)),
        name="memory_attention",
    )(proj, kv)


def _merge_kernel(od_ref, ol_ref, om_ref, wd_ref, wl_ref, wm_ref, gd_ref, gl_ref, gm_ref, o_ref):
    acc = gd_ref[...].astype(F32) * jnp.dot(od_ref[...], wd_ref[...], preferred_element_type=F32)
    acc += gl_ref[...].astype(F32) * jnp.dot(ol_ref[...], wl_ref[...], preferred_element_type=F32)
    acc += gm_ref[...].astype(F32) * jnp.dot(om_ref[...], wm_ref[...], preferred_element_type=F32)
    o_ref[...] = acc.astype(o_ref.dtype)


def _branch_merge(o_diff, o_dil, o_mem, wd, wl, wm, proj, gate_tile0, d_model):
    t = o_diff.shape[0]
    tm = min(ROW_TILE, t)
    tn = PROJ_TN
    per_branch = d_model // tn
    row = lambda n, m: (m, 0)
    col = lambda n, m: (0, n)
    gate = lambda br: (lambda n, m: (m, gate_tile0 + br * per_branch + n))
    return pl.pallas_call(
        _merge_kernel,
        grid=(per_branch, t // tm),
        in_specs=[
            pl.BlockSpec((tm, o_diff.shape[1]), row),
            pl.BlockSpec((tm, o_dil.shape[1]), row),
            pl.BlockSpec((tm, o_mem.shape[1]), row),
            pl.BlockSpec((wd.shape[0], tn), col),
            pl.BlockSpec((wl.shape[0], tn), col),
            pl.BlockSpec((wm.shape[0], tn), col),
            pl.BlockSpec((tm, tn), gate(0)),
            pl.BlockSpec((tm, tn), gate(1)),
            pl.BlockSpec((tm, tn), gate(2)),
        ],
        out_specs=pl.BlockSpec((tm, tn), lambda n, m: (m, n)),
        out_shape=jax.ShapeDtypeStruct((t, d_model), BF16),
        compiler_params=_params(("arbitrary", "arbitrary")),
        name="branch_merge",
    )(o_diff, o_dil, o_mem, wd, wl, wm, proj, proj, proj)


def _split_bf16(x):
    hi = x.astype(BF16)
    return hi, (x - hi.astype(F32)).astype(BF16)


def _pack_bf16_pairs(x):
    n = x.shape[1] // 2
    bits = lax.bitcast_convert_type(x.astype(F32), jnp.uint32)
    return (bits[:, :n] >> 16) | (bits[:, n:] & jnp.uint32(0xFFFF0000))


def _unpack_bf16_pairs(u):
    lo = lax.bitcast_convert_type(u << 16, F32).astype(BF16)
    hi = lax.bitcast_convert_type(u & jnp.uint32(0xFFFF0000), F32).astype(BF16)
    return lo, hi


def _out_proj_kernel(mg_ref, w_ref, x_ref, g_ref, wr_hi_ref, wr_lo_ref, br_ref, x1_ref, h2_ref, lg_ref):
    for rows in _row_blocks(x_ref.shape[0]):
        x1 = x_ref[rows, :] + jnp.dot(mg_ref[rows, :], w_ref[...], preferred_element_type=F32)
        x1_ref[rows, :] = x1
        ms = jnp.mean(x1 * x1, axis=-1, keepdims=True)
        h2 = x1 * lax.rsqrt(ms + NORM_EPS) * g_ref[...]
        hi, lo = _split_bf16(h2)
        h2_ref[rows, :] = _pack_bf16_pairs(hi)
        lg = (jnp.dot(hi, wr_hi_ref[...], preferred_element_type=F32)
              + jnp.dot(lo, wr_hi_ref[...], preferred_element_type=F32)
              + jnp.dot(hi, wr_lo_ref[...], preferred_element_type=F32))
        lane = lax.broadcasted_iota(jnp.int32, lg.shape, 1)
        lg_ref[rows, :] = jnp.where(lane < N_EXPERTS, lg + br_ref[...], -jnp.inf)


def _output_projection(merged, w_out_bf, x2d, g_ffn, wr_hi, wr_lo, b_router_row):
    t, d = x2d.shape
    tm = min(ROW_TILE, t)
    row = lambda i: (i, 0)
    fixed = lambda i: (0, 0)
    return pl.pallas_call(
        _out_proj_kernel,
        grid=(t // tm,),
        in_specs=[
            pl.BlockSpec((tm, d), row),
            pl.BlockSpec((d, d), fixed),
            pl.BlockSpec((tm, d), row),
            pl.BlockSpec((1, d), fixed),
            pl.BlockSpec((d, LANES), fixed),
            pl.BlockSpec((d, LANES), fixed),
            pl.BlockSpec((1, LANES), fixed),
        ],
        out_specs=[pl.BlockSpec((tm, d), row), pl.BlockSpec((tm, d // 2), row), pl.BlockSpec((tm, LANES), row)],
        out_shape=[jax.ShapeDtypeStruct((t, d), F32), jax.ShapeDtypeStruct((t, d // 2), jnp.uint32),
                   jax.ShapeDtypeStruct((t, LANES), F32)],
        compiler_params=_params(("arbitrary",)),
        name="output_projection",
    )(merged, w_out_bf, x2d, g_ffn.reshape(1, d), wr_hi, wr_lo, b_router_row)


def _route_kernel(lg_ref, idx_ref, gate_ref, rank_ref, cnt_ref, carry_sc):
    i = pl.program_id(0)

    @pl.when(i == 0)
    def _():
        carry_sc[...] = jnp.zeros(carry_sc.shape, F32)

    work = lg_ref[...]
    tm = work.shape[0]
    lane = lax.broadcasted_iota(jnp.int32, work.shape, 1).astype(F32)
    vals, idxs = [], []
    member = jnp.zeros(work.shape, F32)
    for _ in range(TOP_K):
        mx = jnp.max(work, axis=1, keepdims=True)
        idx = jnp.min(jnp.where(work == mx, lane, float(LANES)), axis=1, keepdims=True)
        sel = lane == idx
        vals.append(mx)
        idxs.append(idx)
        member = jnp.where(sel, 1.0, member)
        work = jnp.where(sel, -jnp.inf, work)
    exps = [jnp.exp(v - vals[0]) for v in vals]
    z = exps[0] + exps[1] + exps[2] + exps[3]
    r = lax.broadcasted_iota(jnp.int32, (tm, tm), 0)
    c = lax.broadcasted_iota(jnp.int32, (tm, tm), 1)
    before = jnp.where(c < r, 1.0, 0.0).astype(BF16)
    cum = jnp.dot(before, member.astype(BF16), preferred_element_type=F32) + carry_sc[...]
    idx_out = jnp.zeros(work.shape, F32)
    gate_out = jnp.zeros(work.shape, F32)
    rank_out = jnp.zeros(work.shape, F32)
    for k in range(TOP_K):
        rank = jnp.sum(jnp.where(lane == idxs[k], cum, 0.0), axis=1, keepdims=True)
        idx_out = jnp.where(lane == float(k), idxs[k], idx_out)
        gate_out = jnp.where(lane == float(k), exps[k] / z, gate_out)
        rank_out = jnp.where(lane == float(k), rank, rank_out)
    idx_ref[...] = idx_out.astype(jnp.int32)
    gate_ref[...] = gate_out
    rank_ref[...] = rank_out.astype(jnp.int32)
    carry_sc[...] += jnp.sum(member, axis=0, keepdims=True)
    cnt_ref[...] = carry_sc[...].astype(jnp.int32)


def _route(logits):
    t = logits.shape[0]
    tm = min(ROW_TILE, t)
    row = lambda i: (i, 0)
    return pl.pallas_call(
        _route_kernel,
        grid=(t // tm,),
        in_specs=[pl.BlockSpec((tm, LANES), row)],
        out_specs=[pl.BlockSpec((tm, LANES), row), pl.BlockSpec((tm, LANES), row),
                   pl.BlockSpec((tm, LANES), row), pl.BlockSpec((1, LANES), lambda i: (0, 0))],
        out_shape=[jax.ShapeDtypeStruct((t, LANES), jnp.int32), jax.ShapeDtypeStruct((t, LANES), F32),
                   jax.ShapeDtypeStruct((t, LANES), jnp.int32), jax.ShapeDtypeStruct((1, LANES), jnp.int32)],
        scratch_shapes=[pltpu.VMEM((1, LANES), F32)],
        compiler_params=_params(("arbitrary",)),
        name="route",
    )(logits)


def _stream_expert_weights(sched, n_pass, copies, cast):
    be_ref, _, run_ref, next_ref, last_ref, nruns_ref = sched
    p = pl.program_id(0)
    m = pl.program_id(1)

    @pl.when((m == 0) | (be_ref[m] != be_ref[jnp.maximum(m - 1, 0)]))
    def _():
        @pl.when(p * nruns_ref[0] + run_ref[m] == 0)
        def _():
            for c in copies(be_ref[m], p):
                c.start()

        for c in copies(be_ref[m], p):
            c.wait()
        cast()
        is_last = last_ref[m] == 1

        @pl.when(jnp.logical_not(is_last) | (p + 1 < n_pass))
        def _():
            for c in copies(next_ref[m], jnp.where(is_last, p + 1, p)):
                c.start()


def _for_filled_row_blocks(filled, o_ref, compute):
    blocks = _row_blocks(o_ref.shape[0])
    size = blocks[0].stop
    for n in range(1, len(blocks) + 1):
        cond = filled > (n - 1) * size
        if n < len(blocks):
            cond &= filled <= n * size

        @pl.when(cond)
        def _(n=n):
            for rows in blocks[:n]:
                compute(rows)
            for rows in blocks[n:]:
                o_ref[rows, :] = jnp.zeros((size, o_ref.shape[1]), o_ref.dtype)


def _row_copy(src_hbm, row, dst, dst_row, sem):
    return pltpu.make_async_copy(src_hbm.at[pl.ds(row, 1)], dst.at[pl.ds(dst_row, 1)], sem)


def _dispatch_kernel(dest_ref, h_ref, init_hbm, o_hbm, sem):
    del init_hbm
    i = pl.program_id(0)
    rows = h_ref.shape[0]

    def copy(t, slot):
        return pltpu.make_async_copy(h_ref.at[pl.ds(t, 1)], o_hbm.at[pl.ds(slot, 1)], sem)

    def start(t, carry):
        for k in range(TOP_K):
            copy(t, dest_ref[(i * rows + t) * TOP_K + k]).start()
        return carry

    def wait(t, carry):
        for k in range(TOP_K):
            copy(t, 0).wait()
        return carry

    lax.fori_loop(0, rows, start, 0, unroll=DMA_LOOP_UNROLL)
    lax.fori_loop(0, rows, wait, 0, unroll=DMA_LOOP_UNROLL)


def _dispatch(h2_packed, dest_flat, n_slots):
    t, w = h2_packed.shape
    rows = min(COMBINE_ROWS, t)
    return pl.pallas_call(
        _dispatch_kernel,
        grid_spec=pltpu.PrefetchScalarGridSpec(
            num_scalar_prefetch=1,
            grid=(t // rows,),
            in_specs=[pl.BlockSpec((rows, w), lambda i, dest: (i, 0)), pl.BlockSpec(memory_space=pl.ANY)],
            out_specs=pl.BlockSpec(memory_space=pl.ANY),
            scratch_shapes=[pltpu.SemaphoreType.DMA(())],
        ),
        out_shape=jax.ShapeDtypeStruct((n_slots, w), h2_packed.dtype),
        input_output_aliases={2: 0},
        compiler_params=_params(("arbitrary",)),
        name="dispatch",
    )(dest_flat, h2_packed, jnp.zeros((n_slots, w), h2_packed.dtype))


def _gate_up_kernel(n_pass, be_ref, used_ref, run_ref, next_ref, last_ref, nruns_ref, filled_ref, x_ref, w_hbm,
                    bg_ref, bu_ref, o_ref, wbuf, wg_bf, wu_bf, sem):
    m = pl.program_id(1)
    tf = wg_bf.shape[1]
    d_ff = n_pass * tf

    def copies(e, p):
        gate_cols = pl.ds(pl.multiple_of(p * tf, tf), tf)
        up_cols = pl.ds(pl.multiple_of(d_ff + p * tf, tf), tf)
        return [pltpu.make_async_copy(w_hbm.at[e, :, gate_cols], wbuf.at[0], sem),
                pltpu.make_async_copy(w_hbm.at[e, :, up_cols], wbuf.at[1], sem)]

    def cast():
        wg_bf[...] = wbuf[0].astype(BF16)
        wu_bf[...] = wbuf[1].astype(BF16)

    @pl.when(m < used_ref[0])
    def _():
        _stream_expert_weights((be_ref, used_ref, run_ref, next_ref, last_ref, nruns_ref), n_pass, copies, cast)

        def compute(rows):
            lo, hi = _unpack_bf16_pairs(x_ref[rows, :])
            half = lo.shape[1]
            gate = (jnp.dot(lo, wg_bf[:half, :], preferred_element_type=F32)
                    + jnp.dot(hi, wg_bf[half:, :], preferred_element_type=F32) + bg_ref[...])
            up = (jnp.dot(lo, wu_bf[:half, :], preferred_element_type=F32)
                  + jnp.dot(hi, wu_bf[half:, :], preferred_element_type=F32) + bu_ref[...])
            gate = jnp.minimum(gate, SWIGLU_LIMIT)
            up = jnp.clip(up, -SWIGLU_LIMIT, SWIGLU_LIMIT)
            act = (up + 1.0) * gate * (1.0 / (1.0 + jnp.exp(-SWIGLU_ALPHA * gate)))
            o_ref[rows, :] = act.astype(o_ref.dtype)

        _for_filled_row_blocks(filled_ref[m], o_ref, compute)

    @pl.when(m >= used_ref[0])
    def _():
        o_ref[...] = jnp.zeros(o_ref.shape, o_ref.dtype)


def _gate_up(xs, w_gate_up, b_gate_up, sched):
    p = xs.shape[0]
    d = w_gate_up.shape[1]
    d_ff = w_gate_up.shape[2] // 2
    rows, tf = EXPERT_ROWS, FF_TILE
    nf = d_ff // tf
    blk = lambda m, used: jnp.minimum(m, used[0] - 1)
    return pl.pallas_call(
        functools.partial(_gate_up_kernel, nf),
        grid_spec=pltpu.PrefetchScalarGridSpec(
            num_scalar_prefetch=len(sched),
            grid=(nf, p // rows),
            in_specs=[
                pl.BlockSpec((rows, xs.shape[1]), lambda f, m, be, used, *_: (blk(m, used), 0)),
                pl.BlockSpec(memory_space=pl.ANY),
                pl.BlockSpec((None, 1, tf), lambda f, m, be, used, *_: (be[blk(m, used)], 0, f)),
                pl.BlockSpec((None, 1, tf), lambda f, m, be, used, *_: (be[blk(m, used)], 0, nf + f)),
            ],
            out_specs=pl.BlockSpec((rows, tf), lambda f, m, *_: (m, f)),
            scratch_shapes=[pltpu.VMEM((2, d, tf), F32), pltpu.VMEM((d, tf), BF16), pltpu.VMEM((d, tf), BF16),
                            pltpu.SemaphoreType.DMA(())],
        ),
        out_shape=jax.ShapeDtypeStruct((p, d_ff), BF16),
        compiler_params=_params(("arbitrary", "arbitrary")),
        name="expert_gate_up",
    )(*sched, xs, w_gate_up, b_gate_up, b_gate_up)


def _down_kernel(n_pass, be_ref, used_ref, run_ref, next_ref, last_ref, nruns_ref, filled_ref, a_ref, w_hbm, b_ref,
                 o_ref, wbuf, w_bf, sem):
    m = pl.program_id(1)
    tn = w_bf.shape[1]

    def copies(e, p):
        cols = pl.ds(pl.multiple_of(p * tn, tn), tn)
        return [pltpu.make_async_copy(w_hbm.at[e, :, cols], wbuf, sem)]

    def cast():
        w_bf[...] = wbuf[...].astype(BF16)

    @pl.when(m < used_ref[0])
    def _():
        _stream_expert_weights((be_ref, used_ref, run_ref, next_ref, last_ref, nruns_ref), n_pass, copies, cast)

        def compute(rows):
            o_ref[rows, :] = jnp.dot(a_ref[rows, :], w_bf[...], preferred_element_type=F32) + b_ref[...]

        _for_filled_row_blocks(filled_ref[m], o_ref, compute)

    @pl.when(m >= used_ref[0])
    def _():
        o_ref[...] = jnp.zeros(o_ref.shape, o_ref.dtype)


def _down(act, w_down, b_down, sched):
    p, d_ff = act.shape
    d = w_down.shape[2]
    rows, tn = EXPERT_ROWS, DOWN_TN
    blk = lambda m, used: jnp.minimum(m, used[0] - 1)
    return pl.pallas_call(
        functools.partial(_down_kernel, d // tn),
        grid_spec=pltpu.PrefetchScalarGridSpec(
            num_scalar_prefetch=len(sched),
            grid=(d // tn, p // rows),
            in_specs=[
                pl.BlockSpec((rows, d_ff), lambda n, m, be, used, *_: (blk(m, used), 0)),
                pl.BlockSpec(memory_space=pl.ANY),
                pl.BlockSpec((None, 1, tn), lambda n, m, be, used, *_: (be[blk(m, used)], 0, n)),
            ],
            out_specs=pl.BlockSpec((rows, tn), lambda n, m, *_: (m, n)),
            scratch_shapes=[pltpu.VMEM((d_ff, tn), F32), pltpu.VMEM((d_ff, tn), BF16),
                            pltpu.SemaphoreType.DMA(())],
        ),
        out_shape=jax.ShapeDtypeStruct((p, d), F32),
        compiler_params=_params(("arbitrary", "arbitrary")),
        name="expert_down",
    )(*sched, act, w_down, b_down)


def _combine_kernel(dest_ref, x_ref, g_ref, y_hbm, o_ref, buf, sem):
    i = pl.program_id(0)
    rows = x_ref.shape[0]
    slot = i % 2

    def gather(step, into):
        def start(t, carry):
            for k in range(TOP_K):
                _row_copy(y_hbm, dest_ref[(step * rows + t) * TOP_K + k], buf.at[into, k], t, sem.at[into]).start()
            return carry

        lax.fori_loop(0, rows, start, 0, unroll=DMA_LOOP_UNROLL)

    @pl.when(i == 0)
    def _():
        gather(0, 0)

    @pl.when(i + 1 < pl.num_programs(0))
    def _():
        gather(i + 1, 1 - slot)

    def wait(t, carry):
        for k in range(TOP_K):
            _row_copy(y_hbm, 0, buf.at[slot, k], t, sem.at[slot]).wait()
        return carry

    lax.fori_loop(0, rows, wait, 0, unroll=DMA_LOOP_UNROLL)
    g = g_ref[...]
    cur = buf.at[slot]
    moe = ((g[:, 0:1] * cur[0] + g[:, 1:2] * cur[1]) + (g[:, 2:3] * cur[2] + g[:, 3:4] * cur[3]))
    o_ref[...] = x_ref[...] + moe


def _combine(x1, gates_padded, y, dest_flat):
    t, d = x1.shape
    rows = min(COMBINE_ROWS, t)
    return pl.pallas_call(
        _combine_kernel,
        grid_spec=pltpu.PrefetchScalarGridSpec(
            num_scalar_prefetch=1,
            grid=(t // rows,),
            in_specs=[pl.BlockSpec((rows, d), lambda i, dest: (i, 0)),
                      pl.BlockSpec((rows, LANES), lambda i, dest: (i, 0)),
                      pl.BlockSpec(memory_space=pl.ANY)],
            out_specs=pl.BlockSpec((rows, d), lambda i, dest: (i, 0)),
            scratch_shapes=[pltpu.VMEM((2, TOP_K, rows, d), F32), pltpu.SemaphoreType.DMA((2,))],
        ),
        out_shape=jax.ShapeDtypeStruct((t, d), F32),
        compiler_params=_params(("arbitrary",)),
        name="combine",
    )(dest_flat, x1, gates_padded, y)


def _moe(x1, h2, logits, w_gate_up, b_gate_up, w_down, b_down):
    t, d = x1.shape
    n_exp = w_gate_up.shape[0]
    rows = EXPERT_ROWS
    idx_p, gate_p, rank_p, cnt_p = _route(logits)
    idx, rank = idx_p[:, :TOP_K], rank_p[:, :TOP_K]
    counts = cnt_p[0, :n_exp]

    padded = (counts + rows - 1) // rows * rows
    pend = jnp.cumsum(padded)
    pstart = pend - padded
    n_blocks = (t * TOP_K + n_exp * (rows - 1) + rows - 1) // rows
    dest_flat = (pstart[idx] + rank).astype(jnp.int32).reshape(-1)
    block_start = jnp.arange(n_blocks, dtype=jnp.int32) * rows
    block_e = jnp.minimum(jnp.sum(pend[None, :] <= block_start[:, None], axis=1), n_exp - 1).astype(jnp.int32)
    n_used = (pend[-1] // rows).astype(jnp.int32).reshape(1)
    has_rows = counts > 0
    expert_run = (jnp.cumsum(has_rows.astype(jnp.int32)) - 1).astype(jnp.int32)
    n_runs = jnp.sum(has_rows.astype(jnp.int32)).reshape(1)
    run_id = expert_run[block_e]
    is_last_run = (run_id == n_runs[0] - 1).astype(jnp.int32)
    following = jnp.where(is_last_run == 1, 0, run_id + 1)
    expert_ids = jnp.arange(n_exp, dtype=jnp.int32)
    next_e = jnp.sum(jnp.where(has_rows[None, :] & (expert_run[None, :] == following[:, None]),
                               expert_ids[None, :], 0), axis=1).astype(jnp.int32)
    filled = jnp.clip(counts[block_e] - (block_start - pstart[block_e]), 0, rows).astype(jnp.int32)
    sched = (block_e, n_used, run_id, next_e, is_last_run, n_runs, filled)

    xs = _dispatch(h2, dest_flat, n_blocks * rows)
    act = _gate_up(xs, w_gate_up, b_gate_up.reshape(n_exp, 1, -1), sched)
    y = _down(act, w_down, b_down.reshape(n_exp, 1, -1), sched)
    return _combine(x1, gate_p, y, dest_flat)


def _rope_tables(positions):
    half = ROT_DIM // 2
    inv_freq = ROPE_THETA ** (-jnp.arange(0, ROT_DIM, 2, dtype=F32) / ROT_DIM)
    ang = positions.reshape(-1).astype(F32)[:, None] * inv_freq
    cos, sin = jnp.cos(ang), jnp.sin(ang)
    t = ang.shape[0]
    cos_t = jnp.concatenate([cos, cos, jnp.ones((t, HEAD - ROT_DIM), F32)], axis=1)
    sa_t = jnp.concatenate([-sin, jnp.zeros((t, HEAD - half), F32)], axis=1)
    sb_t = jnp.concatenate([jnp.zeros((t, half), F32), sin, jnp.zeros((t, HEAD - ROT_DIM), F32)], axis=1)
    return cos_t, sa_t, sb_t


def _layer(x2d, mem2d, batch, seq, cos_t, sa_t, sb_t, layer_idx, g_mix_norm, w_in, b_gate, diff_q_norm,
           diff_k_norm, diff_lambda, diff_subln, dil_q_norm, dil_k_norm, g_mem_norm, w_mem_kv, mem_q_norm,
           mem_k_norm, w_branch_diff, w_branch_dil, w_branch_mem, w_out, g_ffn_norm, w_router, b_router,
           w_gate_up, b_gate_up, w_down, b_down):
    d = x2d.shape[1]
    diff_cols = DIFF_HEADS * 2 * HEAD
    dil_cols = DIL_HEADS * HEAD
    mem_cols = MEM_HEADS * MEM_HEAD_DIM
    off_dq, off_dk, off_dv = 0, diff_cols, 2 * diff_cols
    off_lq = 3 * diff_cols
    off_lk, off_lv = off_lq + dil_cols, off_lq + 2 * dil_cols
    off_mq = off_lq + 3 * dil_cols
    off_gate = off_mq + mem_cols
    n_cols = off_gate + 3 * d
    assert w_in.shape == (d, n_cols)
    tn = PROJ_TN
    tiles = lambda off, width: tuple(range(off // tn, (off + width) // tn))
    rope_tiles = (tiles(off_dq, diff_cols) + tiles(off_dk, diff_cols) + tiles(off_lq, dil_cols)
                  + tiles(off_lk, dil_cols))

    rep = lambda g, width: jnp.tile(g.astype(F32), width // g.shape[0])
    diff_q_gain = diff_q_norm.astype(F32) * (HEAD ** -0.5 * LOG2E)
    dil_q_gain = dil_q_norm.astype(F32) * (HEAD ** -0.5 * LOG2E)

    def score_bound(q_gain, k_gain):
        return (1.01 * HEAD * jnp.max(jnp.abs(q_gain)) * jnp.max(jnp.abs(k_gain.astype(F32)))).reshape(1)

    diff_bound = score_bound(diff_q_gain, diff_k_norm)
    dil_bound = score_bound(dil_q_gain, dil_k_norm)
    gain_row = jnp.concatenate([
        rep(diff_q_gain, diff_cols), rep(diff_k_norm, diff_cols), jnp.ones((diff_cols,), F32),
        rep(dil_q_gain, dil_cols), rep(dil_k_norm, dil_cols), jnp.ones((dil_cols,), F32),
        rep(mem_q_norm, mem_cols) * MEM_HEAD_DIM ** -0.5, jnp.ones((3 * d,), F32)]).reshape(1, n_cols)
    bias_row = jnp.concatenate([jnp.zeros((off_gate,), F32), b_gate.astype(F32)]).reshape(1, n_cols)

    h = _rmsnorm(x2d, g_mix_norm, BF16)
    proj = _input_projection(h, w_in, gain_row, bias_row, cos_t, sa_t, sb_t, rope_tiles, off_mq // tn,
                             off_gate // tn)

    lam_init = 0.8 - 0.6 * float(np.exp(-0.3 * layer_idx))
    o_diff = _diff_attention(proj, batch, seq, off_dq // DIFF_V_DIM, off_dk // DIFF_V_DIM, off_dv // DIFF_V_DIM,
                             diff_bound, diff_lambda.astype(F32), diff_subln, lam_init)
    o_dil = _dilated_attention(proj, batch, seq, off_lq // HEAD, off_lk // HEAD, off_lv // HEAD, dil_bound)
    kv = _mem_kv(mem2d, g_mem_norm, w_mem_kv.astype(BF16), rep(mem_k_norm, mem_cols).reshape(1, mem_cols), batch)
    o_mem = _memory_attention(proj, kv, batch, seq, off_mq // mem_cols)

    merged = _branch_merge(o_diff, o_dil, o_mem, w_branch_diff.astype(BF16), w_branch_dil.astype(BF16),
                           w_branch_mem.astype(BF16), proj, off_gate // tn, d)

    n_exp = w_router.shape[1]
    wr = jnp.zeros((d, LANES), F32).at[:, :n_exp].set(w_router)
    wr_hi, wr_lo = _split_bf16(wr)
    br_row = jnp.zeros((1, LANES), F32).at[0, :n_exp].set(b_router)
    x1, h2, logits = _output_projection(merged, w_out.astype(BF16), x2d, g_ffn_norm, wr_hi, wr_lo, br_row)
    return _moe(x1, h2, logits, w_gate_up, b_gate_up, w_down, b_down)


def kernel(x, mem, positions, g_mix_norm, w_in, b_gate, diff_q_norm, diff_k_norm, diff_lambda, diff_subln,
           dil_q_norm, dil_k_norm, g_mem_norm, w_mem_kv, mem_q_norm, mem_k_norm, w_branch_diff, w_branch_dil,
           w_branch_mem, w_out, g_ffn_norm, w_router, b_router, w_gate_up, b_gate_up, w_down, b_down):
    batch, seq, d = x.shape
    cos_t, sa_t, sb_t = _rope_tables(positions)
    x2d = x.reshape(batch * seq, d)
    mem2d = mem.reshape(-1, d)
    for l in range(g_mix_norm.shape[0]):
        x2d = _layer(x2d, mem2d, batch, seq, cos_t, sa_t, sb_t, l, g_mix_norm[l], w_in[l], b_gate[l],
                     diff_q_norm[l], diff_k_norm[l], diff_lambda[l], diff_subln[l], dil_q_norm[l],
                     dil_k_norm[l], g_mem_norm[l], w_mem_kv[l], mem_q_norm[l], mem_k_norm[l],
                     w_branch_diff[l], w_branch_dil[l], w_branch_mem[l], w_out[l], g_ffn_norm[l],
                     w_router[l], b_router[l], w_gate_up[l], b_gate_up[l], w_down[l], b_down[l])
    return x2d.reshape(batch, seq, d)
```

```python
import functools

import numpy as np
import jax
import jax.numpy as jnp
from jax import lax
from jax.experimental import pallas as pl
from jax.experimental.pallas import tpu as pltpu

F32 = jnp.float32
BF16 = jnp.bfloat16

NORM_EPS = 1e-6
NEG_INF = -1e30
ROPE_THETA = 500000.0
ROT_DIM = 32
HEAD = 128
DIFF_HEADS = 8
DIFF_V_DIM = 256
DIL_HEADS = 8
DIL_CONFIGS = ((128, 1), (512, 4), (2048, 16))
DIL_PAD = 2048
MEM_HEADS = 4
MEM_HEAD_DIM = 256
N_EXPERTS = 32
TOP_K = 4
SWIGLU_LIMIT = 7.0
SWIGLU_ALPHA = 1.702

LANES = 128
VMEM_LIMIT = 56 * 1024 * 1024

PROJ_TN = 1024
ROW_TILE = 512
DIFF_TQ = 1024
DIFF_TK = 1024
DIFF_SUB = 256
DIFF_FIXED_SHIFT_MAX = 40.0
LOG2E = 1.4426950408889634
DIL_TQ = 256
DIL_HEADS_PER_STEP = 2
EXPERT_ROWS = 512
FF_TILE = 1024
DOWN_TN = 2048
COMBINE_ROWS = 256
DMA_LOOP_UNROLL = 8
MATMUL_ROW_BLOCK = 256
PROJ_TM = 2048


def _params(sem, **kw):
    return pltpu.CompilerParams(dimension_semantics=sem, vmem_limit_bytes=VMEM_LIMIT, **kw)


def _rmsnorm_kernel(x_ref, g_ref, o_ref):
    x = x_ref[...]
    ms = jnp.mean(x * x, axis=-1, keepdims=True)
    o_ref[...] = (x * lax.rsqrt(ms + NORM_EPS) * g_ref[...]).astype(o_ref.dtype)


def _rmsnorm(x, g, out_dtype):
    t, d = x.shape
    tm = min(ROW_TILE, t)
    return pl.pallas_call(
        _rmsnorm_kernel,
        grid=(t // tm,),
        in_specs=[pl.BlockSpec((tm, d), lambda i: (i, 0)), pl.BlockSpec((1, d), lambda i: (0, 0))],
        out_specs=pl.BlockSpec((tm, d), lambda i: (i, 0)),
        out_shape=jax.ShapeDtypeStruct((t, d), out_dtype),
        compiler_params=_params(("arbitrary",)),
        name="rmsnorm",
    )(x, g.reshape(1, d))


def _row_blocks(rows):
    size = min(MATMUL_ROW_BLOCK, rows)
    return [slice(r, r + size) for r in range(0, rows, size)]


def _chunk_norm(x, gain, width):
    outs = []
    for c in range(x.shape[1] // width):
        xc = x[:, c * width:(c + 1) * width]
        ms = jnp.mean(xc * xc, axis=-1, keepdims=True)
        outs.append(xc * lax.rsqrt(ms + NORM_EPS) * gain[:, c * width:(c + 1) * width])
    return outs


def _proj_kernel(rope_tiles, mem_tile, gate_tile0, h_ref, w_ref, gain_ref, bias_ref, cos_ref, sa_ref,
                 sb_ref, o_ref, wbf_ref):
    n = pl.program_id(0)
    m = pl.program_id(1)

    @pl.when(m == 0)
    def _():
        wbf_ref[...] = w_ref[...].astype(BF16)

    is_rope = functools.reduce(jnp.logical_or, [n == t for t in rope_tiles])
    is_mem = n == mem_tile
    is_gate = n >= gate_tile0

    def for_row_blocks(epilogue):
        for rows in _row_blocks(h_ref.shape[0]):
            epilogue(rows, jnp.dot(h_ref[rows, :], wbf_ref[...], preferred_element_type=F32))

    def rope(rows, acc):
        cos, sa, sb = cos_ref[rows, :], sa_ref[rows, :], sb_ref[rows, :]
        for c, y in enumerate(_chunk_norm(acc, gain_ref[...], HEAD)):
            y = y * cos + pltpu.roll(y, HEAD - ROT_DIM // 2, 1) * sa + pltpu.roll(y, ROT_DIM // 2, 1) * sb
            o_ref[rows, c * HEAD:(c + 1) * HEAD] = y.astype(o_ref.dtype)

    def mem_q(rows, acc):
        for c, y in enumerate(_chunk_norm(acc, gain_ref[...], MEM_HEAD_DIM)):
            o_ref[rows, c * MEM_HEAD_DIM:(c + 1) * MEM_HEAD_DIM] = y.astype(o_ref.dtype)

    def gate(rows, acc):
        z = acc + bias_ref[...]
        o_ref[rows, :] = (1.0 / (1.0 + jnp.exp(-z))).astype(o_ref.dtype)

    def plain(rows, acc):
        o_ref[rows, :] = acc.astype(o_ref.dtype)

    pl.when(is_rope)(lambda: for_row_blocks(rope))
    pl.when(is_mem)(lambda: for_row_blocks(mem_q))
    pl.when(is_gate)(lambda: for_row_blocks(gate))
    pl.when(jnp.logical_not(is_rope | is_mem | is_gate))(lambda: for_row_blocks(plain))


def _input_projection(h, w_in, gain_row, bias_row, cos_t, sa_t, sb_t, rope_tiles, mem_tile, gate_tile0):
    t, d = h.shape
    n_cols = w_in.shape[1]
    tm = min(PROJ_TM, t)
    tn = PROJ_TN
    row = lambda n, m: (m, 0)
    col = lambda n, m: (0, n)
    return pl.pallas_call(
        functools.partial(_proj_kernel, rope_tiles, mem_tile, gate_tile0),
        grid=(n_cols // tn, t // tm),
        in_specs=[
            pl.BlockSpec((tm, d), row),
            pl.BlockSpec((d, tn), col),
            pl.BlockSpec((1, tn), col),
            pl.BlockSpec((1, tn), col),
            pl.BlockSpec((tm, HEAD), row),
            pl.BlockSpec((tm, HEAD), row),
            pl.BlockSpec((tm, HEAD), row),
        ],
        out_specs=pl.BlockSpec((tm, tn), lambda n, m: (m, n)),
        out_shape=jax.ShapeDtypeStruct((t, n_cols), BF16),
        scratch_shapes=[pltpu.VMEM((d, tn), BF16)],
        compiler_params=_params(("arbitrary", "arbitrary")),
        name="input_projection",
    )(h, w_in, gain_row, bias_row, cos_t, sa_t, sb_t)


def _diff_attn_kernel(lam_init, bound_ref, q_ref, k_ref, v_ref, lam_ref, subln_ref, o_ref, m_sc, l_sc, acc_sc):
    qi = pl.program_id(2)
    tq = q_ref.shape[0]
    sub = min(DIFF_SUB, tq)
    tk = min(DIFF_TK, tq)
    per_tile = tq // tk
    bound = bound_ref[0]
    l_sc[...] = jnp.zeros(l_sc.shape, F32)
    acc_sc[...] = jnp.zeros(acc_sc.shape, F32)

    def row_plan(d):
        plan = []
        for r in range(tq // sub):
            if d is None or r * sub >= (d + 1) * tk:
                plan.append((r, tk, None))
            elif (r + 1) * sub > d * tk:
                plan.append((r, min(tk, (r + 1) * sub - d * tk), r * sub - d * tk))
        return plan

    def scores(start, c, r, nk, offset):
        s = lax.dot_general(q_ref[r * sub:(r + 1) * sub, c * HEAD:(c + 1) * HEAD],
                            k_ref[pl.ds(start, nk), c * HEAD:(c + 1) * HEAD],
                            (((1,), (1,)), ((), ())), preferred_element_type=F32)
        if offset is not None:
            row = lax.broadcasted_iota(jnp.int32, s.shape, 0) + offset
            col = lax.broadcasted_iota(jnp.int32, s.shape, 1)
            s = jnp.where(row >= col, s, NEG_INF)
        return s

    def fixed_step(start, plan):
        for c in range(2):
            for r, nk, offset in plan:
                rows = slice(r * sub, (r + 1) * sub)
                p = jnp.exp2(scores(start, c, r, nk, offset) - bound)
                l_sc[c, rows] += functools.reduce(
                    jnp.add, [p[:, i * LANES:(i + 1) * LANES] for i in range(nk // LANES)])
                acc_sc[c, rows] += jnp.dot(p.astype(v_ref.dtype), v_ref[pl.ds(start, nk), :],
                                           preferred_element_type=F32)

    def online_step(start, plan):
        for c in range(2):
            for r, nk, offset in plan:
                rows = slice(r * sub, (r + 1) * sub)
                s = scores(start, c, r, nk, offset)
                m_prev = m_sc[c, rows]
                m_new = jnp.maximum(m_prev, jnp.max(s, axis=1, keepdims=True))
                alpha = jnp.exp2(m_prev - m_new)
                p = jnp.exp2(s - jnp.tile(m_new, (1, nk // LANES)))
                l_sc[c, rows] = alpha * l_sc[c, rows] + jnp.sum(p, axis=1, keepdims=True)
                pv = jnp.dot(p.astype(v_ref.dtype), v_ref[pl.ds(start, nk), :], preferred_element_type=F32)
                acc_sc[c, rows] = jnp.tile(alpha, (1, acc_sc.shape[2] // LANES)) * acc_sc[c, rows] + pv
                m_sc[c, rows] = m_new

    def sweep(step):
        def body(j, carry):
            step(pl.multiple_of(j * tk, tk), row_plan(None))
            return carry

        lax.fori_loop(0, qi * per_tile, body, 0)
        for d in range(per_tile):
            step(pl.multiple_of((qi * per_tile + d) * tk, tk), row_plan(d))

    def finish(l0, l1):
        lp = lam_ref[...]
        lam = (jnp.exp(jnp.sum(lp[0:1] * lp[1:2], axis=1, keepdims=True))
               - jnp.exp(jnp.sum(lp[2:3] * lp[3:4], axis=1, keepdims=True)) + lam_init)
        o = acc_sc[0] / l0 - lam * (acc_sc[1] / l1)
        ms = jnp.mean(o * o, axis=-1, keepdims=True)
        o = o * lax.rsqrt(ms + NORM_EPS) * subln_ref[...] * (1.0 - lam_init)
        o_ref[...] = o.astype(o_ref.dtype)

    @pl.when(bound <= DIFF_FIXED_SHIFT_MAX)
    def _():
        sweep(fixed_step)
        finish(jnp.sum(l_sc[0], axis=1, keepdims=True), jnp.sum(l_sc[1], axis=1, keepdims=True))

    @pl.when(bound > DIFF_FIXED_SHIFT_MAX)
    def _():
        m_sc[...] = jnp.full(m_sc.shape, NEG_INF, F32)
        sweep(online_step)
        finish(l_sc[0][:, :1], l_sc[1][:, :1])


def _diff_attention(proj, batch, seq, q_blk0, k_blk0, v_blk0, score_bound, diff_lambda, subln, lam_init):
    tq = min(DIFF_TQ, seq)
    assert seq % tq == 0
    nq = seq // tq
    dv = DIFF_V_DIM
    return pl.pallas_call(
        functools.partial(_diff_attn_kernel, lam_init),
        grid_spec=pltpu.PrefetchScalarGridSpec(
            num_scalar_prefetch=1,
            grid=(batch, DIFF_HEADS, nq),
            in_specs=[
                pl.BlockSpec((tq, dv), lambda b, h, i, bound: (b * nq + i, q_blk0 + h)),
                pl.BlockSpec((seq, dv), lambda b, h, i, bound: (b, k_blk0 + h)),
                pl.BlockSpec((seq, dv), lambda b, h, i, bound: (b, v_blk0 + h)),
                pl.BlockSpec((4, HEAD), lambda b, h, i, bound: (0, 0)),
                pl.BlockSpec((1, dv), lambda b, h, i, bound: (0, 0)),
            ],
            out_specs=pl.BlockSpec((tq, dv), lambda b, h, i, bound: (b * nq + i, h)),
            scratch_shapes=[pltpu.VMEM((2, tq, LANES), F32), pltpu.VMEM((2, tq, LANES), F32),
                            pltpu.VMEM((2, tq, dv), F32)],
        ),
        out_shape=jax.ShapeDtypeStruct((batch * seq, DIFF_HEADS * dv), BF16),
        compiler_params=_params(("arbitrary", "arbitrary", "arbitrary")),
        name="diff_attention",
    )(score_bound, proj, proj, proj, diff_lambda, subln.reshape(1, dv))


def _dilated_multiplicity(tq):
    n_chunks = DIL_PAD // tq + 1
    q = np.arange(tq)[:, None]
    k = np.arange(tq)[None, :]
    out = np.zeros((n_chunks + 1, tq, tq), np.float32)
    for c in range(n_chunks):
        d = DIL_PAD - c * tq + q - k
        for window, dil in DIL_CONFIGS:
            out[c] += ((d >= 0) & (d <= window) & (d % dil == 0)).astype(np.float32)
    return out


def _dil_attn_kernel(bound_ref, q_ref, k_ref, v_ref, mult_ref, o_ref):
    qi = pl.program_id(2)
    tq = q_ref.shape[0]
    n_chunks = mult_ref.shape[0] - 1
    bound = bound_ref[0]
    heads = [slice(h * HEAD, (h + 1) * HEAD) for h in range(q_ref.shape[1] // HEAD)]

    def chunk(c, cols):
        blk = qi - (n_chunks - 1) + c
        start = pl.multiple_of(jnp.maximum(blk, 0) * tq, tq)
        w = mult_ref[jnp.where(blk >= 0, c, n_chunks)]
        s = lax.dot_general(q_ref[:, cols], k_ref[pl.ds(start, tq), cols], (((1,), (1,)), ((), ())),
                            preferred_element_type=F32)
        return start, w, s

    def pv(p, start, cols):
        return jnp.dot(p.astype(v_ref.dtype), v_ref[pl.ds(start, tq), cols], preferred_element_type=F32)

    @pl.when(bound <= DIFF_FIXED_SHIFT_MAX)
    def _():
        for cols in heads:
            den = jnp.zeros((tq, LANES), F32)
            acc = jnp.zeros((tq, HEAD), F32)
            for c in range(n_chunks):
                start, w, s = chunk(c, cols)
                p = w * jnp.exp2(s - bound)
                den += functools.reduce(jnp.add, [p[:, i * LANES:(i + 1) * LANES] for i in range(tq // LANES)])
                acc += pv(p, start, cols)
            o_ref[:, cols] = (acc / jnp.sum(den, axis=1, keepdims=True)).astype(o_ref.dtype)

    @pl.when(bound > DIFF_FIXED_SHIFT_MAX)
    def _():
        for cols in heads:
            chunks = [chunk(c, cols) for c in range(n_chunks)]
            scores = [jnp.where(w > 0.0, s, NEG_INF) for _, w, s in chunks]
            m = jnp.max(functools.reduce(jnp.maximum, scores), axis=1, keepdims=True)
            den = jnp.zeros((tq, 1), F32)
            acc = jnp.zeros((tq, HEAD), F32)
            for (start, w, _), s in zip(chunks, scores):
                p = w * jnp.exp2(s - m)
                den += jnp.sum(p, axis=1, keepdims=True)
                acc += pv(p, start, cols)
            o_ref[:, cols] = (acc / den).astype(o_ref.dtype)


def _dilated_attention(proj, batch, seq, q_blk0, k_blk0, v_blk0, score_bound):
    tq = DIL_TQ
    assert seq % tq == 0
    nq = seq // tq
    hps = DIL_HEADS_PER_STEP
    width = hps * HEAD
    assert q_blk0 % hps == 0 and k_blk0 % hps == 0 and v_blk0 % hps == 0
    mult = jnp.asarray(_dilated_multiplicity(tq))
    return pl.pallas_call(
        _dil_attn_kernel,
        grid_spec=pltpu.PrefetchScalarGridSpec(
            num_scalar_prefetch=1,
            grid=(batch, DIL_HEADS // hps, nq),
            in_specs=[
                pl.BlockSpec((tq, width), lambda b, h, i, bound: (b * nq + i, q_blk0 // hps + h)),
                pl.BlockSpec((seq, width), lambda b, h, i, bound: (b, k_blk0 // hps + h)),
                pl.BlockSpec((seq, width), lambda b, h, i, bound: (b, v_blk0 // hps + h)),
                pl.BlockSpec(mult.shape, lambda b, h, i, bound: (0, 0, 0)),
            ],
            out_specs=pl.BlockSpec((tq, width), lambda b, h, i, bound: (b * nq + i, h)),
        ),
        out_shape=jax.ShapeDtypeStruct((batch * seq, DIL_HEADS * HEAD), BF16),
        compiler_params=_params(("arbitrary", "arbitrary", "arbitrary")),
        name="dilated_attention",
    )(score_bound, proj, proj, proj, mult)


def _mem_kv_kernel(mem_ref, g_ref, w_ref, kn_ref, o_ref):
    x = mem_ref[...]
    ms = jnp.mean(x * x, axis=-1, keepdims=True)
    h = (x * lax.rsqrt(ms + NORM_EPS) * g_ref[...]).astype(BF16)
    kv = jnp.dot(h, w_ref[...], preferred_element_type=F32)
    half = kv.shape[1] // 2
    for c, y in enumerate(_chunk_norm(kv[:, :half], kn_ref[...], MEM_HEAD_DIM)):
        o_ref[:, c * MEM_HEAD_DIM:(c + 1) * MEM_HEAD_DIM] = y.astype(o_ref.dtype)
    o_ref[:, half:] = kv[:, half:].astype(o_ref.dtype)


def _mem_kv(mem2d, g_mem_norm, w_mem_kv_bf, k_gain_row, batch):
    rows, d = mem2d.shape
    mlen = rows // batch
    n = w_mem_kv_bf.shape[1]
    return pl.pallas_call(
        _mem_kv_kernel,
        grid=(batch,),
        in_specs=[
            pl.BlockSpec((mlen, d), lambda b: (b, 0)),
            pl.BlockSpec((1, d), lambda b: (0, 0)),
            pl.BlockSpec((d, n), lambda b: (0, 0)),
            pl.BlockSpec((1, n // 2), lambda b: (0, 0)),
        ],
        out_specs=pl.BlockSpec((mlen, n), lambda b: (b, 0)),
        out_shape=jax.ShapeDtypeStruct((rows, n), BF16),
        compiler_params=_params(("arbitrary",)),
        name="mem_kv",
    )(mem2d, g_mem_norm.reshape(1, d), w_mem_kv_bf, k_gain_row)


def _mem_attn_kernel(q_ref, kv_ref, o_ref):
    half = kv_ref.shape[1] // 2
    for h in range(MEM_HEADS):
        lo, hi = h * MEM_HEAD_DIM, (h + 1) * MEM_HEAD_DIM
        s = lax.dot_general(q_ref[:, lo:hi], kv_ref[:, lo:hi], (((1,), (1,)), ((), ())),
                            preferred_element_type=F32)
        e = jnp.exp(s - jnp.max(s, axis=1, keepdims=True))
        p = e / jnp.sum(e, axis=1, keepdims=True)
        o = jnp.dot(p.astype(kv_ref.dtype), kv_ref[:, half + lo:half + hi], preferred_element_type=F32)
        o_ref[:, lo:hi] = o.astype(o_ref.dtype)


def _memory_attention(proj, kv, batch, seq, q_blk):
    tm = min(ROW_TILE, seq)
    per_batch = seq // tm
    mlen = kv.shape[0] // batch
    width = MEM_HEADS * MEM_HEAD_DIM
    return pl.pallas_call(
        _mem_attn_kernel,
        grid=(batch * per_batch,),
        in_specs=[
            pl.BlockSpec((tm, width), lambda i: (i, q_blk)),
            pl.BlockSpec((mlen, 2 * width), lambda i: (i // per_batch, 0)),
        ],
        out_specs=pl.BlockSpec((tm, width), lambda i: (i, 0)),
        out_shape=jax.ShapeDtypeStruct((batch * seq, width), BF16),
        compiler_params=_params(("arbitrary",)),
        name="memory_attention",
    )(proj, kv)


def _merge_kernel(od_ref, ol_ref, om_ref, wd_ref, wl_ref, wm_ref, gd_ref, gl_ref, gm_ref, o_ref):
    acc = gd_ref[...].astype(F32) * jnp.dot(od_ref[...], wd_ref[...], preferred_element_type=F32)
    acc += gl_ref[...].astype(F32) * jnp.dot(ol_ref[...], wl_ref[...], preferred_element_type=F32)
    acc += gm_ref[...].astype(F32) * jnp.dot(om_ref[...], wm_ref[...], preferred_element_type=F32)
    o_ref[...] = acc.astype(o_ref.dtype)


def _branch_merge(o_diff, o_dil, o_mem, wd, wl, wm, proj, gate_tile0, d_model):
    t = o_diff.shape[0]
    tm = min(ROW_TILE, t)
    tn = PROJ_TN
    per_branch = d_model // tn
    row = lambda n, m: (m, 0)
    col = lambda n, m: (0, n)
    gate = lambda br: (lambda n, m: (m, gate_tile0 + br * per_branch + n))
    return pl.pallas_call(
        _merge_kernel,
        grid=(per_branch, t // tm),
        in_specs=[
            pl.BlockSpec((tm, o_diff.shape[1]), row),
            pl.BlockSpec((tm, o_dil.shape[1]), row),
            pl.BlockSpec((tm, o_mem.shape[1]), row),
            pl.BlockSpec((wd.shape[0], tn), col),
            pl.BlockSpec((wl.shape[0], tn), col),
            pl.BlockSpec((wm.shape[0], tn), col),
            pl.BlockSpec((tm, tn), gate(0)),
            pl.BlockSpec((tm, tn), gate(1)),
            pl.BlockSpec((tm, tn), gate(2)),
        ],
        out_specs=pl.BlockSpec((tm, tn), lambda n, m: (m, n)),
        out_shape=jax.ShapeDtypeStruct((t, d_model), BF16),
        compiler_params=_params(("arbitrary", "arbitrary")),
        name="branch_merge",
    )(o_diff, o_dil, o_mem, wd, wl, wm, proj, proj, proj)


def _split_bf16(x):
    hi = x.astype(BF16)
    return hi, (x - hi.astype(F32)).astype(BF16)


def _pack_bf16_pairs(x):
    n = x.shape[1] // 2
    bits = lax.bitcast_convert_type(x.astype(F32), jnp.uint32)
    return (bits[:, :n] >> 16) | (bits[:, n:] & jnp.uint32(0xFFFF0000))


def _unpack_bf16_pairs(u):
    lo = lax.bitcast_convert_type(u << 16, F32).astype(BF16)
    hi = lax.bitcast_convert_type(u & jnp.uint32(0xFFFF0000), F32).astype(BF16)
    return lo, hi


def _out_proj_kernel(mg_ref, w_ref, x_ref, g_ref, wr_hi_ref, wr_lo_ref, br_ref, x1_ref, h2_ref, lg_ref):
    x1 = x_ref[...] + jnp.dot(mg_ref[...], w_ref[...], preferred_element_type=F32)
    x1_ref[...] = x1
    ms = jnp.mean(x1 * x1, axis=-1, keepdims=True)
    h2 = x1 * lax.rsqrt(ms + NORM_EPS) * g_ref[...]
    hi, lo = _split_bf16(h2)
    h2_ref[...] = _pack_bf16_pairs(hi)
    lg = (jnp.dot(hi, wr_hi_ref[...], preferred_element_type=F32)
          + jnp.dot(lo, wr_hi_ref[...], preferred_element_type=F32)
          + jnp.dot(hi, wr_lo_ref[...], preferred_element_type=F32))
    lane = lax.broadcasted_iota(jnp.int32, lg.shape, 1)
    lg_ref[...] = jnp.where(lane < N_EXPERTS, lg + br_ref[...], -jnp.inf)


def _output_projection(merged, w_out_bf, x2d, g_ffn, wr_hi, wr_lo, b_router_row):
    t, d = x2d.shape
    tm = min(ROW_TILE, t)
    row = lambda i: (i, 0)
    fixed = lambda i: (0, 0)
    return pl.pallas_call(
        _out_proj_kernel,
        grid=(t // tm,),
        in_specs=[
            pl.BlockSpec((tm, d), row),
            pl.BlockSpec((d, d), fixed),
            pl.BlockSpec((tm, d), row),
            pl.BlockSpec((1, d), fixed),
            pl.BlockSpec((d, LANES), fixed),
            pl.BlockSpec((d, LANES), fixed),
            pl.BlockSpec((1, LANES), fixed),
        ],
        out_specs=[pl.BlockSpec((tm, d), row), pl.BlockSpec((tm, d // 2), row), pl.BlockSpec((tm, LANES), row)],
        out_shape=[jax.ShapeDtypeStruct((t, d), F32), jax.ShapeDtypeStruct((t, d // 2), jnp.uint32),
                   jax.ShapeDtypeStruct((t, LANES), F32)],
        compiler_params=_params(("arbitrary",)),
        name="output_projection",
    )(merged, w_out_bf, x2d, g_ffn.reshape(1, d), wr_hi, wr_lo, b_router_row)


def _route_kernel(lg_ref, idx_ref, gate_ref, rank_ref, cnt_ref, carry_sc):
    i = pl.program_id(0)

    @pl.when(i == 0)
    def _():
        carry_sc[...] = jnp.zeros(carry_sc.shape, F32)

    work = lg_ref[...]
    tm = work.shape[0]
    lane = lax.broadcasted_iota(jnp.int32, work.shape, 1).astype(F32)
    vals, idxs = [], []
    member = jnp.zeros(work.shape, F32)
    for _ in range(TOP_K):
        mx = jnp.max(work, axis=1, keepdims=True)
        idx = jnp.min(jnp.where(work == mx, lane, float(LANES)), axis=1, keepdims=True)
        sel = lane == idx
        vals.append(mx)
        idxs.append(idx)
        member = jnp.where(sel, 1.0, member)
        work = jnp.where(sel, -jnp.inf, work)
    exps = [jnp.exp(v - vals[0]) for v in vals]
    z = exps[0] + exps[1] + exps[2] + exps[3]
    r = lax.broadcasted_iota(jnp.int32, (tm, tm), 0)
    c = lax.broadcasted_iota(jnp.int32, (tm, tm), 1)
    before = jnp.where(c < r, 1.0, 0.0).astype(BF16)
    cum = jnp.dot(before, member.astype(BF16), preferred_element_type=F32) + carry_sc[...]
    idx_out = jnp.zeros(work.shape, F32)
    gate_out = jnp.zeros(work.shape, F32)
    rank_out = jnp.zeros(work.shape, F32)
    for k in range(TOP_K):
        rank = jnp.sum(jnp.where(lane == idxs[k], cum, 0.0), axis=1, keepdims=True)
        idx_out = jnp.where(lane == float(k), idxs[k], idx_out)
        gate_out = jnp.where(lane == float(k), exps[k] / z, gate_out)
        rank_out = jnp.where(lane == float(k), rank, rank_out)
    idx_ref[...] = idx_out.astype(jnp.int32)
    gate_ref[...] = gate_out
    rank_ref[...] = rank_out.astype(jnp.int32)
    carry_sc[...] += jnp.sum(member, axis=0, keepdims=True)
    cnt_ref[...] = carry_sc[...].astype(jnp.int32)


def _route(logits):
    t = logits.shape[0]
    tm = min(ROW_TILE, t)
    row = lambda i: (i, 0)
    return pl.pallas_call(
        _route_kernel,
        grid=(t // tm,),
        in_specs=[pl.BlockSpec((tm, LANES), row)],
        out_specs=[pl.BlockSpec((tm, LANES), row), pl.BlockSpec((tm, LANES), row),
                   pl.BlockSpec((tm, LANES), row), pl.BlockSpec((1, LANES), lambda i: (0, 0))],
        out_shape=[jax.ShapeDtypeStruct((t, LANES), jnp.int32), jax.ShapeDtypeStruct((t, LANES), F32),
                   jax.ShapeDtypeStruct((t, LANES), jnp.int32), jax.ShapeDtypeStruct((1, LANES), jnp.int32)],
        scratch_shapes=[pltpu.VMEM((1, LANES), F32)],
        compiler_params=_params(("arbitrary",)),
        name="route",
    )(logits)


def _with_streamed_weights(sched, n_pass, copies, cast, compute):
    be_ref, used_ref, run_ref, next_ref, last_ref, nruns_ref = sched
    p = pl.program_id(0)
    m = pl.program_id(1)
    tile = p * nruns_ref[0] + run_ref[m]
    cur = tile % 2
    first = (m == 0) | (be_ref[m] != be_ref[jnp.maximum(m - 1, 0)])
    run_end = (m == used_ref[0] - 1) | (be_ref[m] != be_ref[jnp.minimum(m + 1, be_ref.shape[0] - 1)])
    is_last = last_ref[m] == 1
    has_next = jnp.logical_not(is_last) | (p + 1 < n_pass)
    next_pass = jnp.where(is_last, p + 1, p)

    @pl.when(first & (tile == 0))
    def _():
        for c in copies(be_ref[m], p):
            c.start()
        for c in copies(be_ref[m], p):
            c.wait()
        cast(0)

    @pl.when(first & has_next)
    def _():
        for c in copies(next_ref[m], next_pass):
            c.start()

    @pl.when(run_end & has_next)
    def _():
        for c in copies(next_ref[m], next_pass):
            c.wait()
        compute(cur, lambda: cast(1 - cur))

    @pl.when(jnp.logical_not(run_end & has_next))
    def _():
        compute(cur, None)


def _for_filled_row_blocks(filled, o_ref, compute, also=None):
    blocks = _row_blocks(o_ref.shape[0])
    size = blocks[0].stop
    for n in range(1, len(blocks) + 1):
        cond = filled > (n - 1) * size
        if n < len(blocks):
            cond &= filled <= n * size

        @pl.when(cond)
        def _(n=n):
            for rows in blocks[:n]:
                compute(rows)
            if also is not None:
                also()
            for rows in blocks[n:]:
                o_ref[rows, :] = jnp.zeros((size, o_ref.shape[1]), o_ref.dtype)


def _row_copy(src_hbm, row, dst, dst_row, sem):
    return pltpu.make_async_copy(src_hbm.at[pl.ds(row, 1)], dst.at[pl.ds(dst_row, 1)], sem)


def _dispatch_kernel(dest_ref, last_ref, used_ref, h_ref, o_hbm, zeros, zero_sem, sem):
    i = pl.program_id(0)
    rows = h_ref.shape[0]
    blk = zeros.shape[0]
    n_blocks = o_hbm.shape[0] // blk

    @pl.when(i == 0)
    def _():
        zeros[...] = jnp.zeros(zeros.shape, zeros.dtype)

        def clear(b):
            return pltpu.make_async_copy(zeros, o_hbm.at[pl.ds(pl.multiple_of(b * blk, blk), blk)], zero_sem)

        def each_target(act):
            def expert(e, carry):
                @pl.when(last_ref[e] >= 0)
                def _():
                    act(clear(last_ref[e]))
                return carry

            def tail(b, carry):
                act(clear(b))
                return carry

            lax.fori_loop(0, last_ref.shape[0], expert, 0)
            lax.fori_loop(used_ref[0], n_blocks, tail, 0)

        each_target(lambda c: c.start())
        each_target(lambda c: c.wait())

    def copy(t, slot):
        return pltpu.make_async_copy(h_ref.at[pl.ds(t, 1)], o_hbm.at[pl.ds(slot, 1)], sem)

    def start(t, carry):
        for k in range(TOP_K):
            copy(t, dest_ref[(i * rows + t) * TOP_K + k]).start()
        return carry

    def wait(t, carry):
        for k in range(TOP_K):
            copy(t, 0).wait()
        return carry

    lax.fori_loop(0, rows, start, 0, unroll=DMA_LOOP_UNROLL)
    lax.fori_loop(0, rows, wait, 0, unroll=DMA_LOOP_UNROLL)


def _dispatch(h2_packed, dest_flat, last_block, n_used, n_slots):
    t, w = h2_packed.shape
    rows = min(COMBINE_ROWS, t)
    return pl.pallas_call(
        _dispatch_kernel,
        grid_spec=pltpu.PrefetchScalarGridSpec(
            num_scalar_prefetch=3,
            grid=(t // rows,),
            in_specs=[pl.BlockSpec((rows, w), lambda i, *_: (i, 0))],
            out_specs=pl.BlockSpec(memory_space=pl.ANY),
            scratch_shapes=[pltpu.VMEM((EXPERT_ROWS, w), h2_packed.dtype), pltpu.SemaphoreType.DMA(()),
                            pltpu.SemaphoreType.DMA(())],
        ),
        out_shape=jax.ShapeDtypeStruct((n_slots, w), h2_packed.dtype),
        compiler_params=_params(("arbitrary",)),
        name="dispatch",
    )(dest_flat, last_block, n_used, h2_packed)


def _gate_up_kernel(n_pass, be_ref, used_ref, run_ref, next_ref, last_ref, nruns_ref, filled_ref, x_ref, w_hbm,
                    bg_ref, bu_ref, o_ref, wbuf, wg_bf, wu_bf, sem):
    m = pl.program_id(1)
    tf = wg_bf.shape[2]
    d_ff = n_pass * tf

    def copies(e, p):
        gate_cols = pl.ds(pl.multiple_of(p * tf, tf), tf)
        up_cols = pl.ds(pl.multiple_of(d_ff + p * tf, tf), tf)
        return [pltpu.make_async_copy(w_hbm.at[e, :, gate_cols], wbuf.at[0], sem),
                pltpu.make_async_copy(w_hbm.at[e, :, up_cols], wbuf.at[1], sem)]

    def cast(slot):
        wg_bf[slot] = wbuf[0].astype(BF16)
        wu_bf[slot] = wbuf[1].astype(BF16)

    def compute(slot, also):
        def block(rows):
            lo, hi = _unpack_bf16_pairs(x_ref[rows, :])
            half = lo.shape[1]
            gate = (jnp.dot(lo, wg_bf[slot, :half, :], preferred_element_type=F32)
                    + jnp.dot(hi, wg_bf[slot, half:, :], preferred_element_type=F32) + bg_ref[...])
            up = (jnp.dot(lo, wu_bf[slot, :half, :], preferred_element_type=F32)
                  + jnp.dot(hi, wu_bf[slot, half:, :], preferred_element_type=F32) + bu_ref[...])
            gate = jnp.minimum(gate, SWIGLU_LIMIT)
            up = jnp.clip(up, -SWIGLU_LIMIT, SWIGLU_LIMIT)
            act = (up + 1.0) * gate * (1.0 / (1.0 + jnp.exp(-SWIGLU_ALPHA * gate)))
            o_ref[rows, :] = act.astype(o_ref.dtype)

        _for_filled_row_blocks(filled_ref[m], o_ref, block, also)

    @pl.when(m < used_ref[0])
    def _():
        _with_streamed_weights((be_ref, used_ref, run_ref, next_ref, last_ref, nruns_ref), n_pass, copies, cast,
                               compute)

    @pl.when(m >= used_ref[0])
    def _():
        o_ref[...] = jnp.zeros(o_ref.shape, o_ref.dtype)


def _gate_up(xs, w_gate_up, b_gate_up, sched):
    p = xs.shape[0]
    d = w_gate_up.shape[1]
    d_ff = w_gate_up.shape[2] // 2
    rows, tf = EXPERT_ROWS, FF_TILE
    nf = d_ff // tf
    blk = lambda m, used: jnp.minimum(m, used[0] - 1)
    return pl.pallas_call(
        functools.partial(_gate_up_kernel, nf),
        grid_spec=pltpu.PrefetchScalarGridSpec(
            num_scalar_prefetch=len(sched),
            grid=(nf, p // rows),
            in_specs=[
                pl.BlockSpec((rows, xs.shape[1]), lambda f, m, be, used, *_: (blk(m, used), 0)),
                pl.BlockSpec(memory_space=pl.ANY),
                pl.BlockSpec((None, 1, tf), lambda f, m, be, used, *_: (be[blk(m, used)], 0, f)),
                pl.BlockSpec((None, 1, tf), lambda f, m, be, used, *_: (be[blk(m, used)], 0, nf + f)),
            ],
            out_specs=pl.BlockSpec((rows, tf), lambda f, m, *_: (m, f)),
            scratch_shapes=[pltpu.VMEM((2, d, tf), F32), pltpu.VMEM((2, d, tf), BF16),
                            pltpu.VMEM((2, d, tf), BF16), pltpu.SemaphoreType.DMA(())],
        ),
        out_shape=jax.ShapeDtypeStruct((p, d_ff), BF16),
        compiler_params=_params(("arbitrary", "arbitrary")),
        name="expert_gate_up",
    )(*sched, xs, w_gate_up, b_gate_up, b_gate_up)


def _down_kernel(n_pass, be_ref, used_ref, run_ref, next_ref, last_ref, nruns_ref, filled_ref, a_ref, w_hbm, b_ref,
                 o_ref, wbuf, w_bf, sem):
    m = pl.program_id(1)
    tn = w_bf.shape[2]

    def copies(e, p):
        cols = pl.ds(pl.multiple_of(p * tn, tn), tn)
        return [pltpu.make_async_copy(w_hbm.at[e, :, cols], wbuf, sem)]

    def cast(slot):
        w_bf[slot] = wbuf[...].astype(BF16)

    def compute(slot, also):
        def block(rows):
            o_ref[rows, :] = jnp.dot(a_ref[rows, :], w_bf[slot], preferred_element_type=F32) + b_ref[...]

        _for_filled_row_blocks(filled_ref[m], o_ref, block, also)

    @pl.when(m < used_ref[0])
    def _():
        _with_streamed_weights((be_ref, used_ref, run_ref, next_ref, last_ref, nruns_ref), n_pass, copies, cast,
                               compute)

    @pl.when(m >= used_ref[0])
    def _():
        o_ref[...] = jnp.zeros(o_ref.shape, o_ref.dtype)


def _down(act, w_down, b_down, sched):
    p, d_ff = act.shape
    d = w_down.shape[2]
    rows, tn = EXPERT_ROWS, DOWN_TN
    blk = lambda m, used: jnp.minimum(m, used[0] - 1)
    return pl.pallas_call(
        functools.partial(_down_kernel, d // tn),
        grid_spec=pltpu.PrefetchScalarGridSpec(
            num_scalar_prefetch=len(sched),
            grid=(d // tn, p // rows),
            in_specs=[
                pl.BlockSpec((rows, d_ff), lambda n, m, be, used, *_: (blk(m, used), 0)),
                pl.BlockSpec(memory_space=pl.ANY),
                pl.BlockSpec((None, 1, tn), lambda n, m, be, used, *_: (be[blk(m, used)], 0, n)),
            ],
            out_specs=pl.BlockSpec((rows, tn), lambda n, m, *_: (m, n)),
            scratch_shapes=[pltpu.VMEM((d_ff, tn), F32), pltpu.VMEM((2, d_ff, tn), BF16),
                            pltpu.SemaphoreType.DMA(())],
        ),
        out_shape=jax.ShapeDtypeStruct((p, d), F32),
        compiler_params=_params(("arbitrary", "arbitrary")),
        name="expert_down",
    )(*sched, act, w_down, b_down)


def _combine_kernel(dest_ref, x_ref, g_ref, y_hbm, o_ref, buf, sem):
    i = pl.program_id(0)
    rows = x_ref.shape[0]
    slot = i % 2

    def gather(step, into):
        def start(t, carry):
            for k in range(TOP_K):
                _row_copy(y_hbm, dest_ref[(step * rows + t) * TOP_K + k], buf.at[into, k], t, sem.at[into]).start()
            return carry

        lax.fori_loop(0, rows, start, 0, unroll=DMA_LOOP_UNROLL)

    @pl.when(i == 0)
    def _():
        gather(0, 0)

    @pl.when(i + 1 < pl.num_programs(0))
    def _():
        gather(i + 1, 1 - slot)

    def wait(t, carry):
        for k in range(TOP_K):
            _row_copy(y_hbm, 0, buf.at[slot, k], t, sem.at[slot]).wait()
        return carry

    lax.fori_loop(0, rows, wait, 0, unroll=DMA_LOOP_UNROLL)
    g = g_ref[...]
    cur = buf.at[slot]
    moe = ((g[:, 0:1] * cur[0] + g[:, 1:2] * cur[1]) + (g[:, 2:3] * cur[2] + g[:, 3:4] * cur[3]))
    o_ref[...] = x_ref[...] + moe


def _combine(x1, gates_padded, y, dest_flat):
    t, d = x1.shape
    rows = min(COMBINE_ROWS, t)
    return pl.pallas_call(
        _combine_kernel,
        grid_spec=pltpu.PrefetchScalarGridSpec(
            num_scalar_prefetch=1,
            grid=(t // rows,),
            in_specs=[pl.BlockSpec((rows, d), lambda i, dest: (i, 0)),
                      pl.BlockSpec((rows, LANES), lambda i, dest: (i, 0)),
                      pl.BlockSpec(memory_space=pl.ANY)],
            out_specs=pl.BlockSpec((rows, d), lambda i, dest: (i, 0)),
            scratch_shapes=[pltpu.VMEM((2, TOP_K, rows, d), F32), pltpu.SemaphoreType.DMA((2,))],
        ),
        out_shape=jax.ShapeDtypeStruct((t, d), F32),
        compiler_params=_params(("arbitrary",)),
        name="combine",
    )(dest_flat, x1, gates_padded, y)


def _moe(x1, h2, logits, w_gate_up, b_gate_up, w_down, b_down):
    t, d = x1.shape
    n_exp = w_gate_up.shape[0]
    rows = EXPERT_ROWS
    idx_p, gate_p, rank_p, cnt_p = _route(logits)
    idx, rank = idx_p[:, :TOP_K], rank_p[:, :TOP_K]
    counts = cnt_p[0, :n_exp]

    padded = (counts + rows - 1) // rows * rows
    pend = jnp.cumsum(padded)
    pstart = pend - padded
    n_blocks = (t * TOP_K + n_exp * (rows - 1) + rows - 1) // rows
    dest_flat = (pstart[idx] + rank).astype(jnp.int32).reshape(-1)
    block_start = jnp.arange(n_blocks, dtype=jnp.int32) * rows
    block_e = jnp.minimum(jnp.sum(pend[None, :] <= block_start[:, None], axis=1), n_exp - 1).astype(jnp.int32)
    n_used = (pend[-1] // rows).astype(jnp.int32).reshape(1)
    has_rows = counts > 0
    expert_run = (jnp.cumsum(has_rows.astype(jnp.int32)) - 1).astype(jnp.int32)
    n_runs = jnp.sum(has_rows.astype(jnp.int32)).reshape(1)
    run_id = expert_run[block_e]
    is_last_run = (run_id == n_runs[0] - 1).astype(jnp.int32)
    following = jnp.where(is_last_run == 1, 0, run_id + 1)
    expert_ids = jnp.arange(n_exp, dtype=jnp.int32)
    next_e = jnp.sum(jnp.where(has_rows[None, :] & (expert_run[None, :] == following[:, None]),
                               expert_ids[None, :], 0), axis=1).astype(jnp.int32)
    filled = jnp.clip(counts[block_e] - (block_start - pstart[block_e]), 0, rows).astype(jnp.int32)
    sched = (block_e, n_used, run_id, next_e, is_last_run, n_runs, filled)

    last_block = jnp.where(has_rows, pend // rows - 1, -1).astype(jnp.int32)
    xs = _dispatch(h2, dest_flat, last_block, n_used, n_blocks * rows)
    act = _gate_up(xs, w_gate_up, b_gate_up.reshape(n_exp, 1, -1), sched)
    y = _down(act, w_down, b_down.reshape(n_exp, 1, -1), sched)
    return _combine(x1, gate_p, y, dest_flat)


def _rope_tables(positions):
    half = ROT_DIM // 2
    inv_freq = ROPE_THETA ** (-jnp.arange(0, ROT_DIM, 2, dtype=F32) / ROT_DIM)
    ang = positions.reshape(-1).astype(F32)[:, None] * inv_freq
    cos, sin = jnp.cos(ang), jnp.sin(ang)
    t = ang.shape[0]
    cos_t = jnp.concatenate([cos, cos, jnp.ones((t, HEAD - ROT_DIM), F32)], axis=1)
    sa_t = jnp.concatenate([-sin, jnp.zeros((t, HEAD - half), F32)], axis=1)
    sb_t = jnp.concatenate([jnp.zeros((t, half), F32), sin, jnp.zeros((t, HEAD - ROT_DIM), F32)], axis=1)
    return cos_t, sa_t, sb_t


def _layer(x2d, mem2d, batch, seq, cos_t, sa_t, sb_t, layer_idx, g_mix_norm, w_in, b_gate, diff_q_norm,
           diff_k_norm, diff_lambda, diff_subln, dil_q_norm, dil_k_norm, g_mem_norm, w_mem_kv, mem_q_norm,
           mem_k_norm, w_branch_diff, w_branch_dil, w_branch_mem, w_out, g_ffn_norm, w_router, b_router,
           w_gate_up, b_gate_up, w_down, b_down):
    d = x2d.shape[1]
    diff_cols = DIFF_HEADS * 2 * HEAD
    dil_cols = DIL_HEADS * HEAD
    mem_cols = MEM_HEADS * MEM_HEAD_DIM
    off_dq, off_dk, off_dv = 0, diff_cols, 2 * diff_cols
    off_lq = 3 * diff_cols
    off_lk, off_lv = off_lq + dil_cols, off_lq + 2 * dil_cols
    off_mq = off_lq + 3 * dil_cols
    off_gate = off_mq + mem_cols
    n_cols = off_gate + 3 * d
    assert w_in.shape == (d, n_cols)
    tn = PROJ_TN
    tiles = lambda off, width: tuple(range(off // tn, (off + width) // tn))
    rope_tiles = (tiles(off_dq, diff_cols) + tiles(off_dk, diff_cols) + tiles(off_lq, dil_cols)
                  + tiles(off_lk, dil_cols))

    rep = lambda g, width: jnp.tile(g.astype(F32), width // g.shape[0])
    diff_q_gain = diff_q_norm.astype(F32) * (HEAD ** -0.5 * LOG2E)
    dil_q_gain = dil_q_norm.astype(F32) * (HEAD ** -0.5 * LOG2E)

    def score_bound(q_gain, k_gain):
        return (1.01 * HEAD * jnp.max(jnp.abs(q_gain)) * jnp.max(jnp.abs(k_gain.astype(F32)))).reshape(1)

    diff_bound = score_bound(diff_q_gain, diff_k_norm)
    dil_bound = score_bound(dil_q_gain, dil_k_norm)
    gain_row = jnp.concatenate([
        rep(diff_q_gain, diff_cols), rep(diff_k_norm, diff_cols), jnp.ones((diff_cols,), F32),
        rep(dil_q_gain, dil_cols), rep(dil_k_norm, dil_cols), jnp.ones((dil_cols,), F32),
        rep(mem_q_norm, mem_cols) * MEM_HEAD_DIM ** -0.5, jnp.ones((3 * d,), F32)]).reshape(1, n_cols)
    bias_row = jnp.concatenate([jnp.zeros((off_gate,), F32), b_gate.astype(F32)]).reshape(1, n_cols)

    h = _rmsnorm(x2d, g_mix_norm, BF16)
    proj = _input_projection(h, w_in, gain_row, bias_row, cos_t, sa_t, sb_t, rope_tiles, off_mq // tn,
                             off_gate // tn)

    lam_init = 0.8 - 0.6 * float(np.exp(-0.3 * layer_idx))
    o_diff = _diff_attention(proj, batch, seq, off_dq // DIFF_V_DIM, off_dk // DIFF_V_DIM, off_dv // DIFF_V_DIM,
                             diff_bound, diff_lambda.astype(F32), diff_subln, lam_init)
    o_dil = _dilated_attention(proj, batch, seq, off_lq // HEAD, off_lk // HEAD, off_lv // HEAD, dil_bound)
    kv = _mem_kv(mem2d, g_mem_norm, w_mem_kv.astype(BF16), rep(mem_k_norm, mem_cols).reshape(1, mem_cols), batch)
    o_mem = _memory_attention(proj, kv, batch, seq, off_mq // mem_cols)

    merged = _branch_merge(o_diff, o_dil, o_mem, w_branch_diff.astype(BF16), w_branch_dil.astype(BF16),
                           w_branch_mem.astype(BF16), proj, off_gate // tn, d)

    n_exp = w_router.shape[1]
    wr = jnp.zeros((d, LANES), F32).at[:, :n_exp].set(w_router)
    wr_hi, wr_lo = _split_bf16(wr)
    br_row = jnp.zeros((1, LANES), F32).at[0, :n_exp].set(b_router)
    x1, h2, logits = _output_projection(merged, w_out.astype(BF16), x2d, g_ffn_norm, wr_hi, wr_lo, br_row)
    return _moe(x1, h2, logits, w_gate_up, b_gate_up, w_down, b_down)


def kernel(x, mem, positions, g_mix_norm, w_in, b_gate, diff_q_norm, diff_k_norm, diff_lambda, diff_subln,
           dil_q_norm, dil_k_norm, g_mem_norm, w_mem_kv, mem_q_norm, mem_k_norm, w_branch_diff, w_branch_dil,
           w_branch_mem, w_out, g_ffn_norm, w_router, b_router, w_gate_up, b_gate_up, w_down, b_down):
    batch, seq, d = x.shape
    cos_t, sa_t, sb_t = _rope_tables(positions)
    x2d = x.reshape(batch * seq, d)
    mem2d = mem.reshape(-1, d)
    for l in range(g_mix_norm.shape[0]):
        x2d = _layer(x2d, mem2d, batch, seq, cos_t, sa_t, sb_t, l, g_mix_norm[l], w_in[l], b_gate[l],
                     diff_q_norm[l], diff_k_norm[l], diff_lambda[l], diff_subln[l], dil_q_norm[l],
                     dil_k_norm[l], g_mem_norm[l], w_mem_kv[l], mem_q_norm[l], mem_k_norm[l],
                     w_branch_diff[l], w_branch_dil[l], w_branch_mem[l], w_out[l], g_ffn_norm[l],
                     w_router[l], b_router[l], w_gate_up[l], b_gate_up[l], w_down[l], b_down[l])
    return x2d.reshape(batch, seq, d)
```

```python
import functools

import numpy as np
import jax
import jax.numpy as jnp
from jax import lax
from jax.experimental import pallas as pl
from jax.experimental.pallas import tpu as pltpu

F32 = jnp.float32
BF16 = jnp.bfloat16

NORM_EPS = 1e-6
NEG_INF = -1e30
ROPE_THETA = 500000.0
ROT_DIM = 32
HEAD = 128
DIFF_HEADS = 8
DIFF_V_DIM = 256
DIL_HEADS = 8
DIL_CONFIGS = ((128, 1), (512, 4), (2048, 16))
DIL_PAD = 2048
MEM_HEADS = 4
MEM_HEAD_DIM = 256
N_EXPERTS = 32
TOP_K = 4
SWIGLU_LIMIT = 7.0
SWIGLU_ALPHA = 1.702

LANES = 128
VMEM_LIMIT = 56 * 1024 * 1024

PROJ_TN = 1024
ROW_TILE = 512
DIFF_TQ = 1024
DIFF_TK = 1024
DIFF_SUB = 256
DIFF_FIXED_SHIFT_MAX = 40.0
LOG2E = 1.4426950408889634
DIL_TQ = 256
DIL_HEADS_PER_STEP = 2
EXPERT_ROWS = 512
FF_TILE = 1024
DOWN_TN = 2048
COMBINE_ROWS = 256
DMA_LOOP_UNROLL = 16
MATMUL_ROW_BLOCK = 256
PROJ_TM = 2048


def _params(sem, **kw):
    return pltpu.CompilerParams(dimension_semantics=sem, vmem_limit_bytes=VMEM_LIMIT, **kw)


def _rmsnorm_kernel(x_ref, g_ref, o_ref):
    x = x_ref[...]
    ms = jnp.mean(x * x, axis=-1, keepdims=True)
    o_ref[...] = (x * lax.rsqrt(ms + NORM_EPS) * g_ref[...]).astype(o_ref.dtype)


def _rmsnorm(x, g, out_dtype):
    t, d = x.shape
    tm = min(ROW_TILE, t)
    return pl.pallas_call(
        _rmsnorm_kernel,
        grid=(t // tm,),
        in_specs=[pl.BlockSpec((tm, d), lambda i: (i, 0)), pl.BlockSpec((1, d), lambda i: (0, 0))],
        out_specs=pl.BlockSpec((tm, d), lambda i: (i, 0)),
        out_shape=jax.ShapeDtypeStruct((t, d), out_dtype),
        compiler_params=_params(("arbitrary",)),
        name="rmsnorm",
    )(x, g.reshape(1, d))


def _row_blocks(rows):
    size = min(MATMUL_ROW_BLOCK, rows)
    return [slice(r, r + size) for r in range(0, rows, size)]


def _chunk_norm(x, gain, width):
    outs = []
    for c in range(x.shape[1] // width):
        xc = x[:, c * width:(c + 1) * width]
        ms = jnp.mean(xc * xc, axis=-1, keepdims=True)
        outs.append(xc * lax.rsqrt(ms + NORM_EPS) * gain[:, c * width:(c + 1) * width])
    return outs


def _proj_kernel(rope_tiles, mem_tile, gate_tile0, h_ref, w_ref, gain_ref, bias_ref, cos_ref, sa_ref,
                 sb_ref, o_ref, wbf_ref):
    n = pl.program_id(0)
    m = pl.program_id(1)

    @pl.when(m == 0)
    def _():
        wbf_ref[...] = w_ref[...].astype(BF16)

    is_rope = functools.reduce(jnp.logical_or, [n == t for t in rope_tiles])
    is_mem = n == mem_tile
    is_gate = n >= gate_tile0

    def for_row_blocks(epilogue):
        for rows in _row_blocks(h_ref.shape[0]):
            epilogue(rows, jnp.dot(h_ref[rows, :], wbf_ref[...], preferred_element_type=F32))

    def rope(rows, acc):
        cos, sa, sb = cos_ref[rows, :], sa_ref[rows, :], sb_ref[rows, :]
        for c, y in enumerate(_chunk_norm(acc, gain_ref[...], HEAD)):
            y = y * cos + pltpu.roll(y, HEAD - ROT_DIM // 2, 1) * sa + pltpu.roll(y, ROT_DIM // 2, 1) * sb
            o_ref[rows, c * HEAD:(c + 1) * HEAD] = y.astype(o_ref.dtype)

    def mem_q(rows, acc):
        for c, y in enumerate(_chunk_norm(acc, gain_ref[...], MEM_HEAD_DIM)):
            o_ref[rows, c * MEM_HEAD_DIM:(c + 1) * MEM_HEAD_DIM] = y.astype(o_ref.dtype)

    def gate(rows, acc):
        z = acc + bias_ref[...]
        o_ref[rows, :] = (1.0 / (1.0 + jnp.exp(-z))).astype(o_ref.dtype)

    def plain(rows, acc):
        o_ref[rows, :] = acc.astype(o_ref.dtype)

    pl.when(is_rope)(lambda: for_row_blocks(rope))
    pl.when(is_mem)(lambda: for_row_blocks(mem_q))
    pl.when(is_gate)(lambda: for_row_blocks(gate))
    pl.when(jnp.logical_not(is_rope | is_mem | is_gate))(lambda: for_row_blocks(plain))


def _input_projection(h, w_in, gain_row, bias_row, cos_t, sa_t, sb_t, rope_tiles, mem_tile, gate_tile0):
    t, d = h.shape
    n_cols = w_in.shape[1]
    tm = min(PROJ_TM, t)
    tn = PROJ_TN
    row = lambda n, m: (m, 0)
    col = lambda n, m: (0, n)
    return pl.pallas_call(
        functools.partial(_proj_kernel, rope_tiles, mem_tile, gate_tile0),
        grid=(n_cols // tn, t // tm),
        in_specs=[
            pl.BlockSpec((tm, d), row),
            pl.BlockSpec((d, tn), col),
            pl.BlockSpec((1, tn), col),
            pl.BlockSpec((1, tn), col),
            pl.BlockSpec((tm, HEAD), row),
            pl.BlockSpec((tm, HEAD), row),
            pl.BlockSpec((tm, HEAD), row),
        ],
        out_specs=pl.BlockSpec((tm, tn), lambda n, m: (m, n)),
        out_shape=jax.ShapeDtypeStruct((t, n_cols), BF16),
        scratch_shapes=[pltpu.VMEM((d, tn), BF16)],
        compiler_params=_params(("arbitrary", "arbitrary")),
        name="input_projection",
    )(h, w_in, gain_row, bias_row, cos_t, sa_t, sb_t)


def _diff_attn_kernel(lam_init, bound_ref, q_ref, k_ref, v_ref, lam_ref, subln_ref, o_ref, m_sc, l_sc, acc_sc):
    qi = pl.program_id(2)
    tq = q_ref.shape[0]
    sub = min(DIFF_SUB, tq)
    tk = min(DIFF_TK, tq)
    per_tile = tq // tk
    bound = bound_ref[0]
    l_sc[...] = jnp.zeros(l_sc.shape, F32)
    acc_sc[...] = jnp.zeros(acc_sc.shape, F32)

    def row_plan(d):
        plan = []
        for r in range(tq // sub):
            if d is None or r * sub >= (d + 1) * tk:
                plan.append((r, tk, None))
            elif (r + 1) * sub > d * tk:
                plan.append((r, min(tk, (r + 1) * sub - d * tk), r * sub - d * tk))
        return plan

    def scores(start, c, r, nk, offset):
        s = lax.dot_general(q_ref[r * sub:(r + 1) * sub, c * HEAD:(c + 1) * HEAD],
                            k_ref[pl.ds(start, nk), c * HEAD:(c + 1) * HEAD],
                            (((1,), (1,)), ((), ())), preferred_element_type=F32)
        if offset is not None:
            row = lax.broadcasted_iota(jnp.int32, s.shape, 0) + offset
            col = lax.broadcasted_iota(jnp.int32, s.shape, 1)
            s = jnp.where(row >= col, s, NEG_INF)
        return s

    def fixed_step(start, plan):
        for c in range(2):
            for r, nk, offset in plan:
                rows = slice(r * sub, (r + 1) * sub)
                p = jnp.exp2(scores(start, c, r, nk, offset) - bound)
                l_sc[c, rows] += functools.reduce(
                    jnp.add, [p[:, i * LANES:(i + 1) * LANES] for i in range(nk // LANES)])
                acc_sc[c, rows] += jnp.dot(p.astype(v_ref.dtype), v_ref[pl.ds(start, nk), :],
                                           preferred_element_type=F32)

    def online_step(start, plan):
        for c in range(2):
            for r, nk, offset in plan:
                rows = slice(r * sub, (r + 1) * sub)
                s = scores(start, c, r, nk, offset)
                m_prev = m_sc[c, rows]
                m_new = jnp.maximum(m_prev, jnp.max(s, axis=1, keepdims=True))
                alpha = jnp.exp2(m_prev - m_new)
                p = jnp.exp2(s - jnp.tile(m_new, (1, nk // LANES)))
                l_sc[c, rows] = alpha * l_sc[c, rows] + jnp.sum(p, axis=1, keepdims=True)
                pv = jnp.dot(p.astype(v_ref.dtype), v_ref[pl.ds(start, nk), :], preferred_element_type=F32)
                acc_sc[c, rows] = jnp.tile(alpha, (1, acc_sc.shape[2] // LANES)) * acc_sc[c, rows] + pv
                m_sc[c, rows] = m_new

    def sweep(step):
        def body(j, carry):
            step(pl.multiple_of(j * tk, tk), row_plan(None))
            return carry

        lax.fori_loop(0, qi * per_tile, body, 0)
        for d in range(per_tile):
            step(pl.multiple_of((qi * per_tile + d) * tk, tk), row_plan(d))

    def finish(l0, l1):
        lp = lam_ref[...]
        lam = (jnp.exp(jnp.sum(lp[0:1] * lp[1:2], axis=1, keepdims=True))
               - jnp.exp(jnp.sum(lp[2:3] * lp[3:4], axis=1, keepdims=True)) + lam_init)
        o = acc_sc[0] / l0 - lam * (acc_sc[1] / l1)
        ms = jnp.mean(o * o, axis=-1, keepdims=True)
        o = o * lax.rsqrt(ms + NORM_EPS) * subln_ref[...] * (1.0 - lam_init)
        o_ref[...] = o.astype(o_ref.dtype)

    @pl.when(bound <= DIFF_FIXED_SHIFT_MAX)
    def _():
        sweep(fixed_step)
        finish(jnp.sum(l_sc[0], axis=1, keepdims=True), jnp.sum(l_sc[1], axis=1, keepdims=True))

    @pl.when(bound > DIFF_FIXED_SHIFT_MAX)
    def _():
        m_sc[...] = jnp.full(m_sc.shape, NEG_INF, F32)
        sweep(online_step)
        finish(l_sc[0][:, :1], l_sc[1][:, :1])


def _diff_attention(proj, batch, seq, q_blk0, k_blk0, v_blk0, score_bound, diff_lambda, subln, lam_init):
    tq = min(DIFF_TQ, seq)
    assert seq % tq == 0
    nq = seq // tq
    dv = DIFF_V_DIM
    return pl.pallas_call(
        functools.partial(_diff_attn_kernel, lam_init),
        grid_spec=pltpu.PrefetchScalarGridSpec(
            num_scalar_prefetch=1,
            grid=(batch, DIFF_HEADS, nq),
            in_specs=[
                pl.BlockSpec((tq, dv), lambda b, h, i, bound: (b * nq + i, q_blk0 + h)),
                pl.BlockSpec((seq, dv), lambda b, h, i, bound: (b, k_blk0 + h)),
                pl.BlockSpec((seq, dv), lambda b, h, i, bound: (b, v_blk0 + h)),
                pl.BlockSpec((4, HEAD), lambda b, h, i, bound: (0, 0)),
                pl.BlockSpec((1, dv), lambda b, h, i, bound: (0, 0)),
            ],
            out_specs=pl.BlockSpec((tq, dv), lambda b, h, i, bound: (b * nq + i, h)),
            scratch_shapes=[pltpu.VMEM((2, tq, LANES), F32), pltpu.VMEM((2, tq, LANES), F32),
                            pltpu.VMEM((2, tq, dv), F32)],
        ),
        out_shape=jax.ShapeDtypeStruct((batch * seq, DIFF_HEADS * dv), BF16),
        compiler_params=_params(("arbitrary", "arbitrary", "arbitrary")),
        name="diff_attention",
    )(score_bound, proj, proj, proj, diff_lambda, subln.reshape(1, dv))


def _dilated_multiplicity(tq):
    n_chunks = DIL_PAD // tq + 1
    q = np.arange(tq)[:, None]
    k = np.arange(tq)[None, :]
    out = np.zeros((n_chunks + 1, tq, tq), np.float32)
    for c in range(n_chunks):
        d = DIL_PAD - c * tq + q - k
        for window, dil in DIL_CONFIGS:
            out[c] += ((d >= 0) & (d <= window) & (d % dil == 0)).astype(np.float32)
    return out


def _dil_attn_kernel(bound_ref, q_ref, k_ref, v_ref, mult_ref, o_ref):
    qi = pl.program_id(2)
    tq = q_ref.shape[0]
    n_chunks = mult_ref.shape[0] - 1
    bound = bound_ref[0]
    heads = [slice(h * HEAD, (h + 1) * HEAD) for h in range(q_ref.shape[1] // HEAD)]

    def chunk(c, cols):
        blk = qi - (n_chunks - 1) + c
        start = pl.multiple_of(jnp.maximum(blk, 0) * tq, tq)
        w = mult_ref[jnp.where(blk >= 0, c, n_chunks)]
        s = lax.dot_general(q_ref[:, cols], k_ref[pl.ds(start, tq), cols], (((1,), (1,)), ((), ())),
                            preferred_element_type=F32)
        return start, w, s

    def pv(p, start, cols):
        return jnp.dot(p.astype(v_ref.dtype), v_ref[pl.ds(start, tq), cols], preferred_element_type=F32)

    @pl.when(bound <= DIFF_FIXED_SHIFT_MAX)
    def _():
        for cols in heads:
            den = jnp.zeros((tq, LANES), F32)
            acc = jnp.zeros((tq, HEAD), F32)
            for c in range(n_chunks):
                start, w, s = chunk(c, cols)
                p = w * jnp.exp2(s - bound)
                den += functools.reduce(jnp.add, [p[:, i * LANES:(i + 1) * LANES] for i in range(tq // LANES)])
                acc += pv(p, start, cols)
            o_ref[:, cols] = (acc / jnp.sum(den, axis=1, keepdims=True)).astype(o_ref.dtype)

    @pl.when(bound > DIFF_FIXED_SHIFT_MAX)
    def _():
        for cols in heads:
            chunks = [chunk(c, cols) for c in range(n_chunks)]
            scores = [jnp.where(w > 0.0, s, NEG_INF) for _, w, s in chunks]
            m = jnp.max(functools.reduce(jnp.maximum, scores), axis=1, keepdims=True)
            den = jnp.zeros((tq, 1), F32)
            acc = jnp.zeros((tq, HEAD), F32)
            for (start, w, _), s in zip(chunks, scores):
                p = w * jnp.exp2(s - m)
                den += jnp.sum(p, axis=1, keepdims=True)
                acc += pv(p, start, cols)
            o_ref[:, cols] = (acc / den).astype(o_ref.dtype)


def _dilated_attention(proj, batch, seq, q_blk0, k_blk0, v_blk0, score_bound):
    tq = DIL_TQ
    assert seq % tq == 0
    nq = seq // tq
    hps = DIL_HEADS_PER_STEP
    width = hps * HEAD
    assert q_blk0 % hps == 0 and k_blk0 % hps == 0 and v_blk0 % hps == 0
    mult = jnp.asarray(_dilated_multiplicity(tq))
    return pl.pallas_call(
        _dil_attn_kernel,
        grid_spec=pltpu.PrefetchScalarGridSpec(
            num_scalar_prefetch=1,
            grid=(batch, DIL_HEADS // hps, nq),
            in_specs=[
                pl.BlockSpec((tq, width), lambda b, h, i, bound: (b * nq + i, q_blk0 // hps + h)),
                pl.BlockSpec((seq, width), lambda b, h, i, bound: (b, k_blk0 // hps + h)),
                pl.BlockSpec((seq, width), lambda b, h, i, bound: (b, v_blk0 // hps + h)),
                pl.BlockSpec(mult.shape, lambda b, h, i, bound: (0, 0, 0)),
            ],
            out_specs=pl.BlockSpec((tq, width), lambda b, h, i, bound: (b * nq + i, h)),
        ),
        out_shape=jax.ShapeDtypeStruct((batch * seq, DIL_HEADS * HEAD), BF16),
        compiler_params=_params(("arbitrary", "arbitrary", "arbitrary")),
        name="dilated_attention",
    )(score_bound, proj, proj, proj, mult)


def _mem_kv_kernel(mem_ref, g_ref, w_ref, kn_ref, o_ref):
    x = mem_ref[...]
    ms = jnp.mean(x * x, axis=-1, keepdims=True)
    h = (x * lax.rsqrt(ms + NORM_EPS) * g_ref[...]).astype(BF16)
    kv = jnp.dot(h, w_ref[...], preferred_element_type=F32)
    half = kv.shape[1] // 2
    for c, y in enumerate(_chunk_norm(kv[:, :half], kn_ref[...], MEM_HEAD_DIM)):
        o_ref[:, c * MEM_HEAD_DIM:(c + 1) * MEM_HEAD_DIM] = y.astype(o_ref.dtype)
    o_ref[:, half:] = kv[:, half:].astype(o_ref.dtype)


def _mem_kv(mem2d, g_mem_norm, w_mem_kv_bf, k_gain_row, batch):
    rows, d = mem2d.shape
    mlen = rows // batch
    n = w_mem_kv_bf.shape[1]
    return pl.pallas_call(
        _mem_kv_kernel,
        grid=(batch,),
        in_specs=[
            pl.BlockSpec((mlen, d), lambda b: (b, 0)),
            pl.BlockSpec((1, d), lambda b: (0, 0)),
            pl.BlockSpec((d, n), lambda b: (0, 0)),
            pl.BlockSpec((1, n // 2), lambda b: (0, 0)),
        ],
        out_specs=pl.BlockSpec((mlen, n), lambda b: (b, 0)),
        out_shape=jax.ShapeDtypeStruct((rows, n), BF16),
        compiler_params=_params(("arbitrary",)),
        name="mem_kv",
    )(mem2d, g_mem_norm.reshape(1, d), w_mem_kv_bf, k_gain_row)


def _mem_attn_kernel(q_ref, kv_ref, o_ref):
    half = kv_ref.shape[1] // 2
    for h in range(MEM_HEADS):
        lo, hi = h * MEM_HEAD_DIM, (h + 1) * MEM_HEAD_DIM
        s = lax.dot_general(q_ref[:, lo:hi], kv_ref[:, lo:hi], (((1,), (1,)), ((), ())),
                            preferred_element_type=F32)
        e = jnp.exp(s - jnp.max(s, axis=1, keepdims=True))
        p = e / jnp.sum(e, axis=1, keepdims=True)
        o = jnp.dot(p.astype(kv_ref.dtype), kv_ref[:, half + lo:half + hi], preferred_element_type=F32)
        o_ref[:, lo:hi] = o.astype(o_ref.dtype)


def _memory_attention(proj, kv, batch, seq, q_blk):
    tm = min(ROW_TILE, seq)
    per_batch = seq // tm
    mlen = kv.shape[0] // batch
    width = MEM_HEADS * MEM_HEAD_DIM
    return pl.pallas_call(
        _mem_attn_kernel,
        grid=(batch * per_batch,),
        in_specs=[
            pl.BlockSpec((tm, width), lambda i: (i, q_blk)),
            pl.BlockSpec((mlen, 2 * width), lambda i: (i // per_batch, 0)),
        ],
        out_specs=pl.BlockSpec((tm, width), lambda i: (i, 0)),
        out_shape=jax.ShapeDtypeStruct((batch * seq, width), BF16),
        compiler_params=_params(("arbitrary",)),
        name="memory_attention",
    )(proj, kv)


def _merge_kernel(od_ref, ol_ref, om_ref, wd_ref, wl_ref, wm_ref, gd_ref, gl_ref, gm_ref, o_ref):
    acc = gd_ref[...].astype(F32) * jnp.dot(od_ref[...], wd_ref[...], preferred_element_type=F32)
    acc += gl_ref[...].astype(F32) * jnp.dot(ol_ref[...], wl_ref[...], preferred_element_type=F32)
    acc += gm_ref[...].astype(F32) * jnp.dot(om_ref[...], wm_ref[...], preferred_element_type=F32)
    o_ref[...] = acc.astype(o_ref.dtype)


def _branch_merge(o_diff, o_dil, o_mem, wd, wl, wm, proj, gate_tile0, d_model):
    t = o_diff.shape[0]
    tm = min(ROW_TILE, t)
    tn = PROJ_TN
    per_branch = d_model // tn
    row = lambda n, m: (m, 0)
    col = lambda n, m: (0, n)
    gate = lambda br: (lambda n, m: (m, gate_tile0 + br * per_branch + n))
    return pl.pallas_call(
        _merge_kernel,
        grid=(per_branch, t // tm),
        in_specs=[
            pl.BlockSpec((tm, o_diff.shape[1]), row),
            pl.BlockSpec((tm, o_dil.shape[1]), row),
            pl.BlockSpec((tm, o_mem.shape[1]), row),
            pl.BlockSpec((wd.shape[0], tn), col),
            pl.BlockSpec((wl.shape[0], tn), col),
            pl.BlockSpec((wm.shape[0], tn), col),
            pl.BlockSpec((tm, tn), gate(0)),
            pl.BlockSpec((tm, tn), gate(1)),
            pl.BlockSpec((tm, tn), gate(2)),
        ],
        out_specs=pl.BlockSpec((tm, tn), lambda n, m: (m, n)),
        out_shape=jax.ShapeDtypeStruct((t, d_model), BF16),
        compiler_params=_params(("arbitrary", "arbitrary")),
        name="branch_merge",
    )(o_diff, o_dil, o_mem, wd, wl, wm, proj, proj, proj)


def _split_bf16(x):
    hi = x.astype(BF16)
    return hi, (x - hi.astype(F32)).astype(BF16)


def _pack_bf16_pairs(x):
    n = x.shape[1] // 2
    bits = lax.bitcast_convert_type(x.astype(F32), jnp.uint32)
    return (bits[:, :n] >> 16) | (bits[:, n:] & jnp.uint32(0xFFFF0000))


def _unpack_bf16_pairs(u):
    lo = lax.bitcast_convert_type(u << 16, F32).astype(BF16)
    hi = lax.bitcast_convert_type(u & jnp.uint32(0xFFFF0000), F32).astype(BF16)
    return lo, hi


def _out_proj_kernel(mg_ref, w_ref, x_ref, g_ref, wr_hi_ref, wr_lo_ref, br_ref, x1_ref, h2_ref, lg_ref):
    x1 = x_ref[...] + jnp.dot(mg_ref[...], w_ref[...], preferred_element_type=F32)
    x1_ref[...] = x1
    ms = jnp.mean(x1 * x1, axis=-1, keepdims=True)
    h2 = x1 * lax.rsqrt(ms + NORM_EPS) * g_ref[...]
    hi, lo = _split_bf16(h2)
    h2_ref[...] = _pack_bf16_pairs(hi)
    lg = (jnp.dot(hi, wr_hi_ref[...], preferred_element_type=F32)
          + jnp.dot(lo, wr_hi_ref[...], preferred_element_type=F32)
          + jnp.dot(hi, wr_lo_ref[...], preferred_element_type=F32))
    lane = lax.broadcasted_iota(jnp.int32, lg.shape, 1)
    lg_ref[...] = jnp.where(lane < N_EXPERTS, lg + br_ref[...], -jnp.inf)


def _output_projection(merged, w_out_bf, x2d, g_ffn, wr_hi, wr_lo, b_router_row):
    t, d = x2d.shape
    tm = min(ROW_TILE, t)
    row = lambda i: (i, 0)
    fixed = lambda i: (0, 0)
    return pl.pallas_call(
        _out_proj_kernel,
        grid=(t // tm,),
        in_specs=[
            pl.BlockSpec((tm, d), row),
            pl.BlockSpec((d, d), fixed),
            pl.BlockSpec((tm, d), row),
            pl.BlockSpec((1, d), fixed),
            pl.BlockSpec((d, LANES), fixed),
            pl.BlockSpec((d, LANES), fixed),
            pl.BlockSpec((1, LANES), fixed),
        ],
        out_specs=[pl.BlockSpec((tm, d), row), pl.BlockSpec((tm, d // 2), row), pl.BlockSpec((tm, LANES), row)],
        out_shape=[jax.ShapeDtypeStruct((t, d), F32), jax.ShapeDtypeStruct((t, d // 2), jnp.uint32),
                   jax.ShapeDtypeStruct((t, LANES), F32)],
        compiler_params=_params(("arbitrary",)),
        name="output_projection",
    )(merged, w_out_bf, x2d, g_ffn.reshape(1, d), wr_hi, wr_lo, b_router_row)


def _route_kernel(lg_ref, idx_ref, gate_ref, rank_ref, cnt_ref, carry_sc):
    i = pl.program_id(0)

    @pl.when(i == 0)
    def _():
        carry_sc[...] = jnp.zeros(carry_sc.shape, F32)

    work = lg_ref[...]
    tm = work.shape[0]
    lane = lax.broadcasted_iota(jnp.int32, work.shape, 1).astype(F32)
    vals, idxs = [], []
    member = jnp.zeros(work.shape, F32)
    for _ in range(TOP_K):
        mx = jnp.max(work, axis=1, keepdims=True)
        idx = jnp.min(jnp.where(work == mx, lane, float(LANES)), axis=1, keepdims=True)
        sel = lane == idx
        vals.append(mx)
        idxs.append(idx)
        member = jnp.where(sel, 1.0, member)
        work = jnp.where(sel, -jnp.inf, work)
    exps = [jnp.exp(v - vals[0]) for v in vals]
    z = exps[0] + exps[1] + exps[2] + exps[3]
    r = lax.broadcasted_iota(jnp.int32, (tm, tm), 0)
    c = lax.broadcasted_iota(jnp.int32, (tm, tm), 1)
    before = jnp.where(c < r, 1.0, 0.0).astype(BF16)
    cum = jnp.dot(before, member.astype(BF16), preferred_element_type=F32) + carry_sc[...]
    idx_out = jnp.zeros(work.shape, F32)
    gate_out = jnp.zeros(work.shape, F32)
    rank_out = jnp.zeros(work.shape, F32)
    for k in range(TOP_K):
        rank = jnp.sum(jnp.where(lane == idxs[k], cum, 0.0), axis=1, keepdims=True)
        idx_out = jnp.where(lane == float(k), idxs[k], idx_out)
        gate_out = jnp.where(lane == float(k), exps[k] / z, gate_out)
        rank_out = jnp.where(lane == float(k), rank, rank_out)
    idx_ref[...] = idx_out.astype(jnp.int32)
    gate_ref[...] = gate_out
    rank_ref[...] = rank_out.astype(jnp.int32)
    carry_sc[...] += jnp.sum(member, axis=0, keepdims=True)
    cnt_ref[...] = carry_sc[...].astype(jnp.int32)


def _route(logits):
    t = logits.shape[0]
    tm = min(ROW_TILE, t)
    row = lambda i: (i, 0)
    return pl.pallas_call(
        _route_kernel,
        grid=(t // tm,),
        in_specs=[pl.BlockSpec((tm, LANES), row)],
        out_specs=[pl.BlockSpec((tm, LANES), row), pl.BlockSpec((tm, LANES), row),
                   pl.BlockSpec((tm, LANES), row), pl.BlockSpec((1, LANES), lambda i: (0, 0))],
        out_shape=[jax.ShapeDtypeStruct((t, LANES), jnp.int32), jax.ShapeDtypeStruct((t, LANES), F32),
                   jax.ShapeDtypeStruct((t, LANES), jnp.int32), jax.ShapeDtypeStruct((1, LANES), jnp.int32)],
        scratch_shapes=[pltpu.VMEM((1, LANES), F32)],
        compiler_params=_params(("arbitrary",)),
        name="route",
    )(logits)


def _stream_expert_weights(sched, n_pass, copies, cast):
    be_ref, _, run_ref, next_ref, last_ref, nruns_ref = sched
    p = pl.program_id(0)
    m = pl.program_id(1)

    @pl.when((m == 0) | (be_ref[m] != be_ref[jnp.maximum(m - 1, 0)]))
    def _():
        @pl.when(p * nruns_ref[0] + run_ref[m] == 0)
        def _():
            for c in copies(be_ref[m], p):
                c.start()

        for c in copies(be_ref[m], p):
            c.wait()
        cast()
        is_last = last_ref[m] == 1

        @pl.when(jnp.logical_not(is_last) | (p + 1 < n_pass))
        def _():
            for c in copies(next_ref[m], jnp.where(is_last, p + 1, p)):
                c.start()


def _for_filled_row_blocks(filled, o_ref, compute):
    blocks = _row_blocks(o_ref.shape[0])
    size = blocks[0].stop
    for n in range(1, len(blocks) + 1):
        cond = filled > (n - 1) * size
        if n < len(blocks):
            cond &= filled <= n * size

        @pl.when(cond)
        def _(n=n):
            for rows in blocks[:n]:
                compute(rows)
            for rows in blocks[n:]:
                o_ref[rows, :] = jnp.zeros((size, o_ref.shape[1]), o_ref.dtype)


def _row_copy(src_hbm, row, dst, dst_row, sem):
    return pltpu.make_async_copy(src_hbm.at[pl.ds(row, 1)], dst.at[pl.ds(dst_row, 1)], sem)


def _dispatch_kernel(dest_ref, last_ref, used_ref, h_ref, o_hbm, zeros, zero_sem, sem):
    i = pl.program_id(0)
    rows = h_ref.shape[0]
    blk = zeros.shape[0]
    n_blocks = o_hbm.shape[0] // blk

    @pl.when(i == 0)
    def _():
        zeros[...] = jnp.zeros(zeros.shape, zeros.dtype)

        def clear(b):
            return pltpu.make_async_copy(zeros, o_hbm.at[pl.ds(pl.multiple_of(b * blk, blk), blk)], zero_sem)

        def each_target(act):
            def expert(e, carry):
                @pl.when(last_ref[e] >= 0)
                def _():
                    act(clear(last_ref[e]))
                return carry

            def tail(b, carry):
                act(clear(b))
                return carry

            lax.fori_loop(0, last_ref.shape[0], expert, 0)
            lax.fori_loop(used_ref[0], n_blocks, tail, 0)

        each_target(lambda c: c.start())
        each_target(lambda c: c.wait())

    def copy(t, slot):
        return pltpu.make_async_copy(h_ref.at[pl.ds(t, 1)], o_hbm.at[pl.ds(slot, 1)], sem)

    def start(t, carry):
        for k in range(TOP_K):
            copy(t, dest_ref[(i * rows + t) * TOP_K + k]).start()
        return carry

    def wait(t, carry):
        for k in range(TOP_K):
            copy(t, 0).wait()
        return carry

    lax.fori_loop(0, rows, start, 0, unroll=DMA_LOOP_UNROLL)
    lax.fori_loop(0, rows, wait, 0, unroll=DMA_LOOP_UNROLL)


def _dispatch(h2_packed, dest_flat, last_block, n_used, n_slots):
    t, w = h2_packed.shape
    rows = min(COMBINE_ROWS, t)
    return pl.pallas_call(
        _dispatch_kernel,
        grid_spec=pltpu.PrefetchScalarGridSpec(
            num_scalar_prefetch=3,
            grid=(t // rows,),
            in_specs=[pl.BlockSpec((rows, w), lambda i, *_: (i, 0))],
            out_specs=pl.BlockSpec(memory_space=pl.ANY),
            scratch_shapes=[pltpu.VMEM((EXPERT_ROWS, w), h2_packed.dtype), pltpu.SemaphoreType.DMA(()),
                            pltpu.SemaphoreType.DMA(())],
        ),
        out_shape=jax.ShapeDtypeStruct((n_slots, w), h2_packed.dtype),
        compiler_params=_params(("arbitrary",)),
        name="dispatch",
    )(dest_flat, last_block, n_used, h2_packed)


def _gate_up_kernel(n_pass, be_ref, used_ref, run_ref, next_ref, last_ref, nruns_ref, filled_ref, x_ref, w_hbm,
                    bg_ref, bu_ref, o_ref, wbuf, wg_bf, wu_bf, sem):
    m = pl.program_id(1)
    tf = wg_bf.shape[1]
    d_ff = n_pass * tf

    def copies(e, p):
        gate_cols = pl.ds(pl.multiple_of(p * tf, tf), tf)
        up_cols = pl.ds(pl.multiple_of(d_ff + p * tf, tf), tf)
        return [pltpu.make_async_copy(w_hbm.at[e, :, gate_cols], wbuf.at[0], sem),
                pltpu.make_async_copy(w_hbm.at[e, :, up_cols], wbuf.at[1], sem)]

    def cast():
        wg_bf[...] = wbuf[0].astype(BF16)
        wu_bf[...] = wbuf[1].astype(BF16)

    @pl.when(m < used_ref[0])
    def _():
        _stream_expert_weights((be_ref, used_ref, run_ref, next_ref, last_ref, nruns_ref), n_pass, copies, cast)

        def compute(rows):
            lo, hi = _unpack_bf16_pairs(x_ref[rows, :])
            half = lo.shape[1]
            gate = (jnp.dot(lo, wg_bf[:half, :], preferred_element_type=F32)
                    + jnp.dot(hi, wg_bf[half:, :], preferred_element_type=F32) + bg_ref[...])
            up = (jnp.dot(lo, wu_bf[:half, :], preferred_element_type=F32)
                  + jnp.dot(hi, wu_bf[half:, :], preferred_element_type=F32) + bu_ref[...])
            gate = jnp.minimum(gate, SWIGLU_LIMIT)
            up = jnp.clip(up, -SWIGLU_LIMIT, SWIGLU_LIMIT)
            act = (up + 1.0) * gate * (1.0 / (1.0 + jnp.exp(-SWIGLU_ALPHA * gate)))
            o_ref[rows, :] = act.astype(o_ref.dtype)

        _for_filled_row_blocks(filled_ref[m], o_ref, compute)

    @pl.when(m >= used_ref[0])
    def _():
        o_ref[...] = jnp.zeros(o_ref.shape, o_ref.dtype)


def _gate_up(xs, w_gate_up, b_gate_up, sched):
    p = xs.shape[0]
    d = w_gate_up.shape[1]
    d_ff = w_gate_up.shape[2] // 2
    rows, tf = EXPERT_ROWS, FF_TILE
    nf = d_ff // tf
    blk = lambda m, used: jnp.minimum(m, used[0] - 1)
    return pl.pallas_call(
        functools.partial(_gate_up_kernel, nf),
        grid_spec=pltpu.PrefetchScalarGridSpec(
            num_scalar_prefetch=len(sched),
            grid=(nf, p // rows),
            in_specs=[
                pl.BlockSpec((rows, xs.shape[1]), lambda f, m, be, used, *_: (blk(m, used), 0)),
                pl.BlockSpec(memory_space=pl.ANY),
                pl.BlockSpec((None, 1, tf), lambda f, m, be, used, *_: (be[blk(m, used)], 0, f)),
                pl.BlockSpec((None, 1, tf), lambda f, m, be, used, *_: (be[blk(m, used)], 0, nf + f)),
            ],
            out_specs=pl.BlockSpec((rows, tf), lambda f, m, *_: (m, f)),
            scratch_shapes=[pltpu.VMEM((2, d, tf), F32), pltpu.VMEM((d, tf), BF16), pltpu.VMEM((d, tf), BF16),
                            pltpu.SemaphoreType.DMA(())],
        ),
        out_shape=jax.ShapeDtypeStruct((p, d_ff), BF16),
        compiler_params=_params(("arbitrary", "arbitrary")),
        name="expert_gate_up",
    )(*sched, xs, w_gate_up, b_gate_up, b_gate_up)


def _down_kernel(n_pass, be_ref, used_ref, run_ref, next_ref, last_ref, nruns_ref, filled_ref, a_ref, w_hbm, b_ref,
                 o_ref, wbuf, w_bf, sem):
    m = pl.program_id(1)
    tn = w_bf.shape[1]

    def copies(e, p):
        cols = pl.ds(pl.multiple_of(p * tn, tn), tn)
        return [pltpu.make_async_copy(w_hbm.at[e, :, cols], wbuf, sem)]

    def cast():
        w_bf[...] = wbuf[...].astype(BF16)

    @pl.when(m < used_ref[0])
    def _():
        _stream_expert_weights((be_ref, used_ref, run_ref, next_ref, last_ref, nruns_ref), n_pass, copies, cast)

        def compute(rows):
            o_ref[rows, :] = jnp.dot(a_ref[rows, :], w_bf[...], preferred_element_type=F32) + b_ref[...]

        _for_filled_row_blocks(filled_ref[m], o_ref, compute)

    @pl.when(m >= used_ref[0])
    def _():
        o_ref[...] = jnp.zeros(o_ref.shape, o_ref.dtype)


def _down(act, w_down, b_down, sched):
    p, d_ff = act.shape
    d = w_down.shape[2]
    rows, tn = EXPERT_ROWS, DOWN_TN
    blk = lambda m, used: jnp.minimum(m, used[0] - 1)
    return pl.pallas_call(
        functools.partial(_down_kernel, d // tn),
        grid_spec=pltpu.PrefetchScalarGridSpec(
            num_scalar_prefetch=len(sched),
            grid=(d // tn, p // rows),
            in_specs=[
                pl.BlockSpec((rows, d_ff), lambda n, m, be, used, *_: (blk(m, used), 0)),
                pl.BlockSpec(memory_space=pl.ANY),
                pl.BlockSpec((None, 1, tn), lambda n, m, be, used, *_: (be[blk(m, used)], 0, n)),
            ],
            out_specs=pl.BlockSpec((rows, tn), lambda n, m, *_: (m, n)),
            scratch_shapes=[pltpu.VMEM((d_ff, tn), F32), pltpu.VMEM((d_ff, tn), BF16),
                            pltpu.SemaphoreType.DMA(())],
        ),
        out_shape=jax.ShapeDtypeStruct((p, d), F32),
        compiler_params=_params(("arbitrary", "arbitrary")),
        name="expert_down",
    )(*sched, act, w_down, b_down)


def _combine_kernel(dest_ref, x_ref, g_ref, y_hbm, o_ref, buf, sem):
    i = pl.program_id(0)
    rows = x_ref.shape[0]
    slot = i % 2

    def gather(step, into):
        def start(t, carry):
            for k in range(TOP_K):
                _row_copy(y_hbm, dest_ref[(step * rows + t) * TOP_K + k], buf.at[into, k], t, sem.at[into]).start()
            return carry

        lax.fori_loop(0, rows, start, 0, unroll=DMA_LOOP_UNROLL)

    @pl.when(i == 0)
    def _():
        gather(0, 0)

    @pl.when(i + 1 < pl.num_programs(0))
    def _():
        gather(i + 1, 1 - slot)

    def wait(t, carry):
        for k in range(TOP_K):
            _row_copy(y_hbm, 0, buf.at[slot, k], t, sem.at[slot]).wait()
        return carry

    lax.fori_loop(0, rows, wait, 0, unroll=DMA_LOOP_UNROLL)
    g = g_ref[...]
    cur = buf.at[slot]
    moe = ((g[:, 0:1] * cur[0] + g[:, 1:2] * cur[1]) + (g[:, 2:3] * cur[2] + g[:, 3:4] * cur[3]))
    o_ref[...] = x_ref[...] + moe


def _combine(x1, gates_padded, y, dest_flat):
    t, d = x1.shape
    rows = min(COMBINE_ROWS, t)
    return pl.pallas_call(
        _combine_kernel,
        grid_spec=pltpu.PrefetchScalarGridSpec(
            num_scalar_prefetch=1,
            grid=(t // rows,),
            in_specs=[pl.BlockSpec((rows, d), lambda i, dest: (i, 0)),
                      pl.BlockSpec((rows, LANES), lambda i, dest: (i, 0)),
                      pl.BlockSpec(memory_space=pl.ANY)],
            out_specs=pl.BlockSpec((rows, d), lambda i, dest: (i, 0)),
            scratch_shapes=[pltpu.VMEM((2, TOP_K, rows, d), F32), pltpu.SemaphoreType.DMA((2,))],
        ),
        out_shape=jax.ShapeDtypeStruct((t, d), F32),
        compiler_params=_params(("arbitrary",)),
        name="combine",
    )(dest_flat, x1, gates_padded, y)


def _moe(x1, h2, logits, w_gate_up, b_gate_up, w_down, b_down):
    t, d = x1.shape
    n_exp = w_gate_up.shape[0]
    rows = EXPERT_ROWS
    idx_p, gate_p, rank_p, cnt_p = _route(logits)
    idx, rank = idx_p[:, :TOP_K], rank_p[:, :TOP_K]
    counts = cnt_p[0, :n_exp]

    padded = (counts + rows - 1) // rows * rows
    pend = jnp.cumsum(padded)
    pstart = pend - padded
    n_blocks = (t * TOP_K + n_exp * (rows - 1) + rows - 1) // rows
    dest_flat = (pstart[idx] + rank).astype(jnp.int32).reshape(-1)
    block_start = jnp.arange(n_blocks, dtype=jnp.int32) * rows
    block_e = jnp.minimum(jnp.sum(pend[None, :] <= block_start[:, None], axis=1), n_exp - 1).astype(jnp.int32)
    n_used = (pend[-1] // rows).astype(jnp.int32).reshape(1)
    has_rows = counts > 0
    expert_run = (jnp.cumsum(has_rows.astype(jnp.int32)) - 1).astype(jnp.int32)
    n_runs = jnp.sum(has_rows.astype(jnp.int32)).reshape(1)
    run_id = expert_run[block_e]
    is_last_run = (run_id == n_runs[0] - 1).astype(jnp.int32)
    following = jnp.where(is_last_run == 1, 0, run_id + 1)
    expert_ids = jnp.arange(n_exp, dtype=jnp.int32)
    next_e = jnp.sum(jnp.where(has_rows[None, :] & (expert_run[None, :] == following[:, None]),
                               expert_ids[None, :], 0), axis=1).astype(jnp.int32)
    filled = jnp.clip(counts[block_e] - (block_start - pstart[block_e]), 0, rows).astype(jnp.int32)
    sched = (block_e, n_used, run_id, next_e, is_last_run, n_runs, filled)

    last_block = jnp.where(has_rows, pend // rows - 1, -1).astype(jnp.int32)
    xs = _dispatch(h2, dest_flat, last_block, n_used, n_blocks * rows)
    act = _gate_up(xs, w_gate_up, b_gate_up.reshape(n_exp, 1, -1), sched)
    y = _down(act, w_down, b_down.reshape(n_exp, 1, -1), sched)
    return _combine(x1, gate_p, y, dest_flat)


def _rope_tables(positions):
    half = ROT_DIM // 2
    inv_freq = ROPE_THETA ** (-jnp.arange(0, ROT_DIM, 2, dtype=F32) / ROT_DIM)
    ang = positions.reshape(-1).astype(F32)[:, None] * inv_freq
    cos, sin = jnp.cos(ang), jnp.sin(ang)
    t = ang.shape[0]
    cos_t = jnp.concatenate([cos, cos, jnp.ones((t, HEAD - ROT_DIM), F32)], axis=1)
    sa_t = jnp.concatenate([-sin, jnp.zeros((t, HEAD - half), F32)], axis=1)
    sb_t = jnp.concatenate([jnp.zeros((t, half), F32), sin, jnp.zeros((t, HEAD - ROT_DIM), F32)], axis=1)
    return cos_t, sa_t, sb_t


def _layer(x2d, mem2d, batch, seq, cos_t, sa_t, sb_t, layer_idx, g_mix_norm, w_in, b_gate, diff_q_norm,
           diff_k_norm, diff_lambda, diff_subln, dil_q_norm, dil_k_norm, g_mem_norm, w_mem_kv, mem_q_norm,
           mem_k_norm, w_branch_diff, w_branch_dil, w_branch_mem, w_out, g_ffn_norm, w_router, b_router,
           w_gate_up, b_gate_up, w_down, b_down):
    d = x2d.shape[1]
    diff_cols = DIFF_HEADS * 2 * HEAD
    dil_cols = DIL_HEADS * HEAD
    mem_cols = MEM_HEADS * MEM_HEAD_DIM
    off_dq, off_dk, off_dv = 0, diff_cols, 2 * diff_cols
    off_lq = 3 * diff_cols
    off_lk, off_lv = off_lq + dil_cols, off_lq + 2 * dil_cols
    off_mq = off_lq + 3 * dil_cols
    off_gate = off_mq + mem_cols
    n_cols = off_gate + 3 * d
    assert w_in.shape == (d, n_cols)
    tn = PROJ_TN
    tiles = lambda off, width: tuple(range(off // tn, (off + width) // tn))
    rope_tiles = (tiles(off_dq, diff_cols) + tiles(off_dk, diff_cols) + tiles(off_lq, dil_cols)
                  + tiles(off_lk, dil_cols))

    rep = lambda g, width: jnp.tile(g.astype(F32), width // g.shape[0])
    diff_q_gain = diff_q_norm.astype(F32) * (HEAD ** -0.5 * LOG2E)
    dil_q_gain = dil_q_norm.astype(F32) * (HEAD ** -0.5 * LOG2E)

    def score_bound(q_gain, k_gain):
        return (1.01 * HEAD * jnp.max(jnp.abs(q_gain)) * jnp.max(jnp.abs(k_gain.astype(F32)))).reshape(1)

    diff_bound = score_bound(diff_q_gain, diff_k_norm)
    dil_bound = score_bound(dil_q_gain, dil_k_norm)
    gain_row = jnp.concatenate([
        rep(diff_q_gain, diff_cols), rep(diff_k_norm, diff_cols), jnp.ones((diff_cols,), F32),
        rep(dil_q_gain, dil_cols), rep(dil_k_norm, dil_cols), jnp.ones((dil_cols,), F32),
        rep(mem_q_norm, mem_cols) * MEM_HEAD_DIM ** -0.5, jnp.ones((3 * d,), F32)]).reshape(1, n_cols)
    bias_row = jnp.concatenate([jnp.zeros((off_gate,), F32), b_gate.astype(F32)]).reshape(1, n_cols)

    h = _rmsnorm(x2d, g_mix_norm, BF16)
    proj = _input_projection(h, w_in, gain_row, bias_row, cos_t, sa_t, sb_t, rope_tiles, off_mq // tn,
                             off_gate // tn)

    lam_init = 0.8 - 0.6 * float(np.exp(-0.3 * layer_idx))
    o_diff = _diff_attention(proj, batch, seq, off_dq // DIFF_V_DIM, off_dk // DIFF_V_DIM, off_dv // DIFF_V_DIM,
                             diff_bound, diff_lambda.astype(F32), diff_subln, lam_init)
    o_dil = _dilated_attention(proj, batch, seq, off_lq // HEAD, off_lk // HEAD, off_lv // HEAD, dil_bound)
    kv = _mem_kv(mem2d, g_mem_norm, w_mem_kv.astype(BF16), rep(mem_k_norm, mem_cols).reshape(1, mem_cols), batch)
    o_mem = _memory_attention(proj, kv, batch, seq, off_mq // mem_cols)

    merged = _branch_merge(o_diff, o_dil, o_mem, w_branch_diff.astype(BF16), w_branch_dil.astype(BF16),
                           w_branch_mem.astype(BF16), proj, off_gate // tn, d)

    n_exp = w_router.shape[1]
    wr = jnp.zeros((d, LANES), F32).at[:, :n_exp].set(w_router)
    wr_hi, wr_lo = _split_bf16(wr)
    br_row = jnp.zeros((1, LANES), F32).at[0, :n_exp].set(b_router)
    x1, h2, logits = _output_projection(merged, w_out.astype(BF16), x2d, g_ffn_norm, wr_hi, wr_lo, br_row)
    return _moe(x1, h2, logits, w_gate_up, b_gate_up, w_down, b_down)


def kernel(x, mem, positions, g_mix_norm, w_in, b_gate, diff_q_norm, diff_k_norm, diff_lambda, diff_subln,
           dil_q_norm, dil_k_norm, g_mem_norm, w_mem_kv, mem_q_norm, mem_k_norm, w_branch_diff, w_branch_dil,
           w_branch_mem, w_out, g_ffn_norm, w_router, b_router, w_gate_up, b_gate_up, w_down, b_down):
    batch, seq, d = x.shape
    cos_t, sa_t, sb_t = _rope_tables(positions)
    x2d = x.reshape(batch * seq, d)
    mem2d = mem.reshape(-1, d)
    for l in range(g_mix_norm.shape[0]):
        x2d = _layer(x2d, mem2d, batch, seq, cos_t, sa_t, sb_t, l, g_mix_norm[l], w_in[l], b_gate[l],
                     diff_q_norm[l], diff_k_norm[l], diff_lambda[l], diff_subln[l], dil_q_norm[l],
                     dil_k_norm[l], g_mem_norm[l], w_mem_kv[l], mem_q_norm[l], mem_k_norm[l],
                     w_branch_diff[l], w_branch_dil[l], w_branch_mem[l], w_out[l], g_ffn_norm[l],
                     w_router[l], b_router[l], w_gate_up[l], b_gate_up[l], w_down[l], b_down[l])
    return x2d.reshape(batch, seq, d)
```

```python
import functools

import numpy as np
import jax
import jax.numpy as jnp
from jax import lax
from jax.experimental import pallas as pl
from jax.experimental.pallas import tpu as pltpu

F32 = jnp.float32
BF16 = jnp.bfloat16

NORM_EPS = 1e-6
NEG_INF = -1e30
ROPE_THETA = 500000.0
ROT_DIM = 32
HEAD = 128
DIFF_HEADS = 8
DIFF_V_DIM = 256
DIL_HEADS = 8
DIL_CONFIGS = ((128, 1), (512, 4), (2048, 16))
DIL_PAD = 2048
MEM_HEADS = 4
MEM_HEAD_DIM = 256
N_EXPERTS = 32
TOP_K = 4
SWIGLU_LIMIT = 7.0
SWIGLU_ALPHA = 1.702

LANES = 128
VMEM_LIMIT = 56 * 1024 * 1024

PROJ_TN = 1024
ROW_TILE = 512
DIFF_TQ = 2048
DIFF_TK = 1024
DIFF_SUB = 256
DIFF_FIXED_SHIFT_MAX = 40.0
LOG2E = 1.4426950408889634
DIL_TQ = 256
DIL_HEADS_PER_STEP = 2
EXPERT_ROWS = 512
FF_TILE = 1024
DOWN_TN = 2048
COMBINE_ROWS = 256
DMA_LOOP_UNROLL = 16
MATMUL_ROW_BLOCK = 256
PROJ_TM = 2048


def _params(sem, **kw):
    return pltpu.CompilerParams(dimension_semantics=sem, vmem_limit_bytes=VMEM_LIMIT, **kw)


def _rmsnorm_kernel(x_ref, g_ref, o_ref):
    x = x_ref[...]
    ms = jnp.mean(x * x, axis=-1, keepdims=True)
    o_ref[...] = (x * lax.rsqrt(ms + NORM_EPS) * g_ref[...]).astype(o_ref.dtype)


def _rmsnorm(x, g, out_dtype):
    t, d = x.shape
    tm = min(ROW_TILE, t)
    return pl.pallas_call(
        _rmsnorm_kernel,
        grid=(t // tm,),
        in_specs=[pl.BlockSpec((tm, d), lambda i: (i, 0)), pl.BlockSpec((1, d), lambda i: (0, 0))],
        out_specs=pl.BlockSpec((tm, d), lambda i: (i, 0)),
        out_shape=jax.ShapeDtypeStruct((t, d), out_dtype),
        compiler_params=_params(("arbitrary",)),
        name="rmsnorm",
    )(x, g.reshape(1, d))


def _row_blocks(rows):
    size = min(MATMUL_ROW_BLOCK, rows)
    return [slice(r, r + size) for r in range(0, rows, size)]


def _chunk_norm(x, gain, width):
    outs = []
    for c in range(x.shape[1] // width):
        xc = x[:, c * width:(c + 1) * width]
        ms = jnp.mean(xc * xc, axis=-1, keepdims=True)
        outs.append(xc * lax.rsqrt(ms + NORM_EPS) * gain[:, c * width:(c + 1) * width])
    return outs


def _proj_kernel(rope_tiles, mem_tile, gate_tile0, h_ref, w_ref, gain_ref, bias_ref, cos_ref, sa_ref,
                 sb_ref, o_ref, wbf_ref):
    n = pl.program_id(0)
    m = pl.program_id(1)

    @pl.when(m == 0)
    def _():
        wbf_ref[...] = w_ref[...].astype(BF16)

    is_rope = functools.reduce(jnp.logical_or, [n == t for t in rope_tiles])
    is_mem = n == mem_tile
    is_gate = n >= gate_tile0

    def for_row_blocks(epilogue):
        for rows in _row_blocks(h_ref.shape[0]):
            epilogue(rows, jnp.dot(h_ref[rows, :], wbf_ref[...], preferred_element_type=F32))

    def rope(rows, acc):
        cos, sa, sb = cos_ref[rows, :], sa_ref[rows, :], sb_ref[rows, :]
        for c, y in enumerate(_chunk_norm(acc, gain_ref[...], HEAD)):
            y = y * cos + pltpu.roll(y, HEAD - ROT_DIM // 2, 1) * sa + pltpu.roll(y, ROT_DIM // 2, 1) * sb
            o_ref[rows, c * HEAD:(c + 1) * HEAD] = y.astype(o_ref.dtype)

    def mem_q(rows, acc):
        for c, y in enumerate(_chunk_norm(acc, gain_ref[...], MEM_HEAD_DIM)):
            o_ref[rows, c * MEM_HEAD_DIM:(c + 1) * MEM_HEAD_DIM] = y.astype(o_ref.dtype)

    def gate(rows, acc):
        z = acc + bias_ref[...]
        o_ref[rows, :] = (1.0 / (1.0 + jnp.exp(-z))).astype(o_ref.dtype)

    def plain(rows, acc):
        o_ref[rows, :] = acc.astype(o_ref.dtype)

    pl.when(is_rope)(lambda: for_row_blocks(rope))
    pl.when(is_mem)(lambda: for_row_blocks(mem_q))
    pl.when(is_gate)(lambda: for_row_blocks(gate))
    pl.when(jnp.logical_not(is_rope | is_mem | is_gate))(lambda: for_row_blocks(plain))


def _input_projection(h, w_in, gain_row, bias_row, cos_t, sa_t, sb_t, rope_tiles, mem_tile, gate_tile0):
    t, d = h.shape
    n_cols = w_in.shape[1]
    tm = min(PROJ_TM, t)
    tn = PROJ_TN
    row = lambda n, m: (m, 0)
    col = lambda n, m: (0, n)
    return pl.pallas_call(
        functools.partial(_proj_kernel, rope_tiles, mem_tile, gate_tile0),
        grid=(n_cols // tn, t // tm),
        in_specs=[
            pl.BlockSpec((tm, d), row),
            pl.BlockSpec((d, tn), col),
            pl.BlockSpec((1, tn), col),
            pl.BlockSpec((1, tn), col),
            pl.BlockSpec((tm, HEAD), row),
            pl.BlockSpec((tm, HEAD), row),
            pl.BlockSpec((tm, HEAD), row),
        ],
        out_specs=pl.BlockSpec((tm, tn), lambda n, m: (m, n)),
        out_shape=jax.ShapeDtypeStruct((t, n_cols), BF16),
        scratch_shapes=[pltpu.VMEM((d, tn), BF16)],
        compiler_params=_params(("arbitrary", "arbitrary")),
        name="input_projection",
    )(h, w_in, gain_row, bias_row, cos_t, sa_t, sb_t)


def _diff_attn_kernel(lam_init, bound_ref, q_ref, k_ref, v_ref, lam_ref, subln_ref, o_ref, m_sc, l_sc, acc_sc):
    qi = pl.program_id(2)
    tq = q_ref.shape[0]
    sub = min(DIFF_SUB, tq)
    tk = min(DIFF_TK, tq)
    per_tile = tq // tk
    bound = bound_ref[0]
    l_sc[...] = jnp.zeros(l_sc.shape, F32)
    acc_sc[...] = jnp.zeros(acc_sc.shape, F32)

    def row_plan(d):
        plan = []
        for r in range(tq // sub):
            if d is None or r * sub >= (d + 1) * tk:
                plan.append((r, tk, None))
            elif (r + 1) * sub > d * tk:
                plan.append((r, min(tk, (r + 1) * sub - d * tk), r * sub - d * tk))
        return plan

    def scores(start, c, r, nk, offset):
        s = lax.dot_general(q_ref[r * sub:(r + 1) * sub, c * HEAD:(c + 1) * HEAD],
                            k_ref[pl.ds(start, nk), c * HEAD:(c + 1) * HEAD],
                            (((1,), (1,)), ((), ())), preferred_element_type=F32)
        if offset is not None:
            row = lax.broadcasted_iota(jnp.int32, s.shape, 0) + offset
            col = lax.broadcasted_iota(jnp.int32, s.shape, 1)
            s = jnp.where(row >= col, s, NEG_INF)
        return s

    def fixed_step(start, plan):
        for c in range(2):
            for r, nk, offset in plan:
                rows = slice(r * sub, (r + 1) * sub)
                p = jnp.exp2(scores(start, c, r, nk, offset) - bound)
                l_sc[c, rows] += functools.reduce(
                    jnp.add, [p[:, i * LANES:(i + 1) * LANES] for i in range(nk // LANES)])
                acc_sc[c, rows] += jnp.dot(p.astype(v_ref.dtype), v_ref[pl.ds(start, nk), :],
                                           preferred_element_type=F32)

    def online_step(start, plan):
        for c in range(2):
            for r, nk, offset in plan:
                rows = slice(r * sub, (r + 1) * sub)
                s = scores(start, c, r, nk, offset)
                m_prev = m_sc[c, rows]
                m_new = jnp.maximum(m_prev, jnp.max(s, axis=1, keepdims=True))
                alpha = jnp.exp2(m_prev - m_new)
                p = jnp.exp2(s - jnp.tile(m_new, (1, nk // LANES)))
                l_sc[c, rows] = alpha * l_sc[c, rows] + jnp.sum(p, axis=1, keepdims=True)
                pv = jnp.dot(p.astype(v_ref.dtype), v_ref[pl.ds(start, nk), :], preferred_element_type=F32)
                acc_sc[c, rows] = jnp.tile(alpha, (1, acc_sc.shape[2] // LANES)) * acc_sc[c, rows] + pv
                m_sc[c, rows] = m_new

    def sweep(step):
        def body(j, carry):
            step(pl.multiple_of(j * tk, tk), row_plan(None))
            return carry

        lax.fori_loop(0, qi * per_tile, body, 0)
        for d in range(per_tile):
            step(pl.multiple_of((qi * per_tile + d) * tk, tk), row_plan(d))

    def finish(l0, l1):
        lp = lam_ref[...]
        lam = (jnp.exp(jnp.sum(lp[0:1] * lp[1:2], axis=1, keepdims=True))
               - jnp.exp(jnp.sum(lp[2:3] * lp[3:4], axis=1, keepdims=True)) + lam_init)
        o = acc_sc[0] / l0 - lam * (acc_sc[1] / l1)
        ms = jnp.mean(o * o, axis=-1, keepdims=True)
        o = o * lax.rsqrt(ms + NORM_EPS) * subln_ref[...] * (1.0 - lam_init)
        o_ref[...] = o.astype(o_ref.dtype)

    @pl.when(bound <= DIFF_FIXED_SHIFT_MAX)
    def _():
        sweep(fixed_step)
        finish(jnp.sum(l_sc[0], axis=1, keepdims=True), jnp.sum(l_sc[1], axis=1, keepdims=True))

    @pl.when(bound > DIFF_FIXED_SHIFT_MAX)
    def _():
        m_sc[...] = jnp.full(m_sc.shape, NEG_INF, F32)
        sweep(online_step)
        finish(l_sc[0][:, :1], l_sc[1][:, :1])


def _diff_attention(proj, batch, seq, q_blk0, k_blk0, v_blk0, score_bound, diff_lambda, subln, lam_init):
    tq = min(DIFF_TQ, seq)
    assert seq % tq == 0
    nq = seq // tq
    dv = DIFF_V_DIM
    return pl.pallas_call(
        functools.partial(_diff_attn_kernel, lam_init),
        grid_spec=pltpu.PrefetchScalarGridSpec(
            num_scalar_prefetch=1,
            grid=(batch, DIFF_HEADS, nq),
            in_specs=[
                pl.BlockSpec((tq, dv), lambda b, h, i, bound: (b * nq + i, q_blk0 + h)),
                pl.BlockSpec((seq, dv), lambda b, h, i, bound: (b, k_blk0 + h)),
                pl.BlockSpec((seq, dv), lambda b, h, i, bound: (b, v_blk0 + h)),
                pl.BlockSpec((4, HEAD), lambda b, h, i, bound: (0, 0)),
                pl.BlockSpec((1, dv), lambda b, h, i, bound: (0, 0)),
            ],
            out_specs=pl.BlockSpec((tq, dv), lambda b, h, i, bound: (b * nq + i, h)),
            scratch_shapes=[pltpu.VMEM((2, tq, LANES), F32), pltpu.VMEM((2, tq, LANES), F32),
                            pltpu.VMEM((2, tq, dv), F32)],
        ),
        out_shape=jax.ShapeDtypeStruct((batch * seq, DIFF_HEADS * dv), BF16),
        compiler_params=_params(("arbitrary", "arbitrary", "arbitrary")),
        name="diff_attention",
    )(score_bound, proj, proj, proj, diff_lambda, subln.reshape(1, dv))


def _dilated_multiplicity(tq):
    n_chunks = DIL_PAD // tq + 1
    q = np.arange(tq)[:, None]
    k = np.arange(tq)[None, :]
    out = np.zeros((n_chunks + 1, tq, tq), np.float32)
    for c in range(n_chunks):
        d = DIL_PAD - c * tq + q - k
        for window, dil in DIL_CONFIGS:
            out[c] += ((d >= 0) & (d <= window) & (d % dil == 0)).astype(np.float32)
    return out


def _dil_attn_kernel(bound_ref, q_ref, k_ref, v_ref, mult_ref, o_ref):
    qi = pl.program_id(2)
    tq = q_ref.shape[0]
    n_chunks = mult_ref.shape[0] - 1
    bound = bound_ref[0]
    heads = [slice(h * HEAD, (h + 1) * HEAD) for h in range(q_ref.shape[1] // HEAD)]

    def chunk(c, cols):
        blk = qi - (n_chunks - 1) + c
        start = pl.multiple_of(jnp.maximum(blk, 0) * tq, tq)
        w = mult_ref[jnp.where(blk >= 0, c, n_chunks)]
        s = lax.dot_general(q_ref[:, cols], k_ref[pl.ds(start, tq), cols], (((1,), (1,)), ((), ())),
                            preferred_element_type=F32)
        return start, w, s

    def pv(p, start, cols):
        return jnp.dot(p.astype(v_ref.dtype), v_ref[pl.ds(start, tq), cols], preferred_element_type=F32)

    @pl.when(bound <= DIFF_FIXED_SHIFT_MAX)
    def _():
        for cols in heads:
            den = jnp.zeros((tq, LANES), F32)
            acc = jnp.zeros((tq, HEAD), F32)
            for c in range(n_chunks):
                start, w, s = chunk(c, cols)
                p = w * jnp.exp2(s - bound)
                den += functools.reduce(jnp.add, [p[:, i * LANES:(i + 1) * LANES] for i in range(tq // LANES)])
                acc += pv(p, start, cols)
            o_ref[:, cols] = (acc / jnp.sum(den, axis=1, keepdims=True)).astype(o_ref.dtype)

    @pl.when(bound > DIFF_FIXED_SHIFT_MAX)
    def _():
        for cols in heads:
            chunks = [chunk(c, cols) for c in range(n_chunks)]
            scores = [jnp.where(w > 0.0, s, NEG_INF) for _, w, s in chunks]
            m = jnp.max(functools.reduce(jnp.maximum, scores), axis=1, keepdims=True)
            den = jnp.zeros((tq, 1), F32)
            acc = jnp.zeros((tq, HEAD), F32)
            for (start, w, _), s in zip(chunks, scores):
                p = w * jnp.exp2(s - m)
                den += jnp.sum(p, axis=1, keepdims=True)
                acc += pv(p, start, cols)
            o_ref[:, cols] = (acc / den).astype(o_ref.dtype)


def _dilated_attention(proj, batch, seq, q_blk0, k_blk0, v_blk0, score_bound):
    tq = DIL_TQ
    assert seq % tq == 0
    nq = seq // tq
    hps = DIL_HEADS_PER_STEP
    width = hps * HEAD
    assert q_blk0 % hps == 0 and k_blk0 % hps == 0 and v_blk0 % hps == 0
    mult = jnp.asarray(_dilated_multiplicity(tq))
    return pl.pallas_call(
        _dil_attn_kernel,
        grid_spec=pltpu.PrefetchScalarGridSpec(
            num_scalar_prefetch=1,
            grid=(batch, DIL_HEADS // hps, nq),
            in_specs=[
                pl.BlockSpec((tq, width), lambda b, h, i, bound: (b * nq + i, q_blk0 // hps + h)),
                pl.BlockSpec((seq, width), lambda b, h, i, bound: (b, k_blk0 // hps + h)),
                pl.BlockSpec((seq, width), lambda b, h, i, bound: (b, v_blk0 // hps + h)),
                pl.BlockSpec(mult.shape, lambda b, h, i, bound: (0, 0, 0)),
            ],
            out_specs=pl.BlockSpec((tq, width), lambda b, h, i, bound: (b * nq + i, h)),
        ),
        out_shape=jax.ShapeDtypeStruct((batch * seq, DIL_HEADS * HEAD), BF16),
        compiler_params=_params(("arbitrary", "arbitrary", "arbitrary")),
        name="dilated_attention",
    )(score_bound, proj, proj, proj, mult)


def _mem_kv_kernel(mem_ref, g_ref, w_ref, kn_ref, o_ref):
    x = mem_ref[...]
    ms = jnp.mean(x * x, axis=-1, keepdims=True)
    h = (x * lax.rsqrt(ms + NORM_EPS) * g_ref[...]).astype(BF16)
    kv = jnp.dot(h, w_ref[...], preferred_element_type=F32)
    half = kv.shape[1] // 2
    for c, y in enumerate(_chunk_norm(kv[:, :half], kn_ref[...], MEM_HEAD_DIM)):
        o_ref[:, c * MEM_HEAD_DIM:(c + 1) * MEM_HEAD_DIM] = y.astype(o_ref.dtype)
    o_ref[:, half:] = kv[:, half:].astype(o_ref.dtype)


def _mem_kv(mem2d, g_mem_norm, w_mem_kv_bf, k_gain_row, batch):
    rows, d = mem2d.shape
    mlen = rows // batch
    n = w_mem_kv_bf.shape[1]
    return pl.pallas_call(
        _mem_kv_kernel,
        grid=(batch,),
        in_specs=[
            pl.BlockSpec((mlen, d), lambda b: (b, 0)),
            pl.BlockSpec((1, d), lambda b: (0, 0)),
            pl.BlockSpec((d, n), lambda b: (0, 0)),
            pl.BlockSpec((1, n // 2), lambda b: (0, 0)),
        ],
        out_specs=pl.BlockSpec((mlen, n), lambda b: (b, 0)),
        out_shape=jax.ShapeDtypeStruct((rows, n), BF16),
        compiler_params=_params(("arbitrary",)),
        name="mem_kv",
    )(mem2d, g_mem_norm.reshape(1, d), w_mem_kv_bf, k_gain_row)


def _mem_attn_kernel(q_ref, kv_ref, o_ref):
    half = kv_ref.shape[1] // 2
    for h in range(MEM_HEADS):
        lo, hi = h * MEM_HEAD_DIM, (h + 1) * MEM_HEAD_DIM
        s = lax.dot_general(q_ref[:, lo:hi], kv_ref[:, lo:hi], (((1,), (1,)), ((), ())),
                            preferred_element_type=F32)
        e = jnp.exp(s - jnp.max(s, axis=1, keepdims=True))
        p = e / jnp.sum(e, axis=1, keepdims=True)
        o = jnp.dot(p.astype(kv_ref.dtype), kv_ref[:, half + lo:half + hi], preferred_element_type=F32)
        o_ref[:, lo:hi] = o.astype(o_ref.dtype)


def _memory_attention(proj, kv, batch, seq, q_blk):
    tm = min(ROW_TILE, seq)
    per_batch = seq // tm
    mlen = kv.shape[0] // batch
    width = MEM_HEADS * MEM_HEAD_DIM
    return pl.pallas_call(
        _mem_attn_kernel,
        grid=(batch * per_batch,),
        in_specs=[
            pl.BlockSpec((tm, width), lambda i: (i, q_blk)),
            pl.BlockSpec((mlen, 2 * width), lambda i: (i // per_batch, 0)),
        ],
        out_specs=pl.BlockSpec((tm, width), lambda i: (i, 0)),
        out_shape=jax.ShapeDtypeStruct((batch * seq, width), BF16),
        compiler_params=_params(("arbitrary",)),
        name="memory_attention",
    )(proj, kv)


def _merge_kernel(od_ref, ol_ref, om_ref, wd_ref, wl_ref, wm_ref, gd_ref, gl_ref, gm_ref, o_ref):
    acc = gd_ref[...].astype(F32) * jnp.dot(od_ref[...], wd_ref[...], preferred_element_type=F32)
    acc += gl_ref[...].astype(F32) * jnp.dot(ol_ref[...], wl_ref[...], preferred_element_type=F32)
    acc += gm_ref[...].astype(F32) * jnp.dot(om_ref[...], wm_ref[...], preferred_element_type=F32)
    o_ref[...] = acc.astype(o_ref.dtype)


def _branch_merge(o_diff, o_dil, o_mem, wd, wl, wm, proj, gate_tile0, d_model):
    t = o_diff.shape[0]
    tm = min(ROW_TILE, t)
    tn = PROJ_TN
    per_branch = d_model // tn
    row = lambda n, m: (m, 0)
    col = lambda n, m: (0, n)
    gate = lambda br: (lambda n, m: (m, gate_tile0 + br * per_branch + n))
    return pl.pallas_call(
        _merge_kernel,
        grid=(per_branch, t // tm),
        in_specs=[
            pl.BlockSpec((tm, o_diff.shape[1]), row),
            pl.BlockSpec((tm, o_dil.shape[1]), row),
            pl.BlockSpec((tm, o_mem.shape[1]), row),
            pl.BlockSpec((wd.shape[0], tn), col),
            pl.BlockSpec((wl.shape[0], tn), col),
            pl.BlockSpec((wm.shape[0], tn), col),
            pl.BlockSpec((tm, tn), gate(0)),
            pl.BlockSpec((tm, tn), gate(1)),
            pl.BlockSpec((tm, tn), gate(2)),
        ],
        out_specs=pl.BlockSpec((tm, tn), lambda n, m: (m, n)),
        out_shape=jax.ShapeDtypeStruct((t, d_model), BF16),
        compiler_params=_params(("arbitrary", "arbitrary")),
        name="branch_merge",
    )(o_diff, o_dil, o_mem, wd, wl, wm, proj, proj, proj)


def _split_bf16(x):
    hi = x.astype(BF16)
    return hi, (x - hi.astype(F32)).astype(BF16)


def _pack_bf16_pairs(x):
    n = x.shape[1] // 2
    bits = lax.bitcast_convert_type(x.astype(F32), jnp.uint32)
    return (bits[:, :n] >> 16) | (bits[:, n:] & jnp.uint32(0xFFFF0000))


def _unpack_bf16_pairs(u):
    lo = lax.bitcast_convert_type(u << 16, F32).astype(BF16)
    hi = lax.bitcast_convert_type(u & jnp.uint32(0xFFFF0000), F32).astype(BF16)
    return lo, hi


def _out_proj_kernel(mg_ref, w_ref, x_ref, g_ref, wr_hi_ref, wr_lo_ref, br_ref, x1_ref, h2_ref, lg_ref):
    x1 = x_ref[...] + jnp.dot(mg_ref[...], w_ref[...], preferred_element_type=F32)
    x1_ref[...] = x1
    ms = jnp.mean(x1 * x1, axis=-1, keepdims=True)
    h2 = x1 * lax.rsqrt(ms + NORM_EPS) * g_ref[...]
    hi, lo = _split_bf16(h2)
    h2_ref[...] = _pack_bf16_pairs(hi)
    lg = (jnp.dot(hi, wr_hi_ref[...], preferred_element_type=F32)
          + jnp.dot(lo, wr_hi_ref[...], preferred_element_type=F32)
          + jnp.dot(hi, wr_lo_ref[...], preferred_element_type=F32))
    lane = lax.broadcasted_iota(jnp.int32, lg.shape, 1)
    lg_ref[...] = jnp.where(lane < N_EXPERTS, lg + br_ref[...], -jnp.inf)


def _output_projection(merged, w_out_bf, x2d, g_ffn, wr_hi, wr_lo, b_router_row):
    t, d = x2d.shape
    tm = min(ROW_TILE, t)
    row = lambda i: (i, 0)
    fixed = lambda i: (0, 0)
    return pl.pallas_call(
        _out_proj_kernel,
        grid=(t // tm,),
        in_specs=[
            pl.BlockSpec((tm, d), row),
            pl.BlockSpec((d, d), fixed),
            pl.BlockSpec((tm, d), row),
            pl.BlockSpec((1, d), fixed),
            pl.BlockSpec((d, LANES), fixed),
            pl.BlockSpec((d, LANES), fixed),
            pl.BlockSpec((1, LANES), fixed),
        ],
        out_specs=[pl.BlockSpec((tm, d), row), pl.BlockSpec((tm, d // 2), row), pl.BlockSpec((tm, LANES), row)],
        out_shape=[jax.ShapeDtypeStruct((t, d), F32), jax.ShapeDtypeStruct((t, d // 2), jnp.uint32),
                   jax.ShapeDtypeStruct((t, LANES), F32)],
        compiler_params=_params(("arbitrary",)),
        name="output_projection",
    )(merged, w_out_bf, x2d, g_ffn.reshape(1, d), wr_hi, wr_lo, b_router_row)


def _route_kernel(lg_ref, idx_ref, gate_ref, rank_ref, cnt_ref, carry_sc):
    i = pl.program_id(0)

    @pl.when(i == 0)
    def _():
        carry_sc[...] = jnp.zeros(carry_sc.shape, F32)

    work = lg_ref[...]
    tm = work.shape[0]
    lane = lax.broadcasted_iota(jnp.int32, work.shape, 1).astype(F32)
    vals, idxs = [], []
    member = jnp.zeros(work.shape, F32)
    for _ in range(TOP_K):
        mx = jnp.max(work, axis=1, keepdims=True)
        idx = jnp.min(jnp.where(work == mx, lane, float(LANES)), axis=1, keepdims=True)
        sel = lane == idx
        vals.append(mx)
        idxs.append(idx)
        member = jnp.where(sel, 1.0, member)
        work = jnp.where(sel, -jnp.inf, work)
    exps = [jnp.exp(v - vals[0]) for v in vals]
    z = exps[0] + exps[1] + exps[2] + exps[3]
    r = lax.broadcasted_iota(jnp.int32, (tm, tm), 0)
    c = lax.broadcasted_iota(jnp.int32, (tm, tm), 1)
    before = jnp.where(c < r, 1.0, 0.0).astype(BF16)
    cum = jnp.dot(before, member.astype(BF16), preferred_element_type=F32) + carry_sc[...]
    idx_out = jnp.zeros(work.shape, F32)
    gate_out = jnp.zeros(work.shape, F32)
    rank_out = jnp.zeros(work.shape, F32)
    for k in range(TOP_K):
        rank = jnp.sum(jnp.where(lane == idxs[k], cum, 0.0), axis=1, keepdims=True)
        idx_out = jnp.where(lane == float(k), idxs[k], idx_out)
        gate_out = jnp.where(lane == float(k), exps[k] / z, gate_out)
        rank_out = jnp.where(lane == float(k), rank, rank_out)
    idx_ref[...] = idx_out.astype(jnp.int32)
    gate_ref[...] = gate_out
    rank_ref[...] = rank_out.astype(jnp.int32)
    carry_sc[...] += jnp.sum(member, axis=0, keepdims=True)
    cnt_ref[...] = carry_sc[...].astype(jnp.int32)


def _route(logits):
    t = logits.shape[0]
    tm = min(ROW_TILE, t)
    row = lambda i: (i, 0)
    return pl.pallas_call(
        _route_kernel,
        grid=(t // tm,),
        in_specs=[pl.BlockSpec((tm, LANES), row)],
        out_specs=[pl.BlockSpec((tm, LANES), row), pl.BlockSpec((tm, LANES), row),
                   pl.BlockSpec((tm, LANES), row), pl.BlockSpec((1, LANES), lambda i: (0, 0))],
        out_shape=[jax.ShapeDtypeStruct((t, LANES), jnp.int32), jax.ShapeDtypeStruct((t, LANES), F32),
                   jax.ShapeDtypeStruct((t, LANES), jnp.int32), jax.ShapeDtypeStruct((1, LANES), jnp.int32)],
        scratch_shapes=[pltpu.VMEM((1, LANES), F32)],
        compiler_params=_params(("arbitrary",)),
        name="route",
    )(logits)


def _stream_expert_weights(sched, n_pass, copies, cast):
    be_ref, _, run_ref, next_ref, last_ref, nruns_ref = sched
    p = pl.program_id(0)
    m = pl.program_id(1)

    @pl.when((m == 0) | (be_ref[m] != be_ref[jnp.maximum(m - 1, 0)]))
    def _():
        @pl.when(p * nruns_ref[0] + run_ref[m] == 0)
        def _():
            for c in copies(be_ref[m], p):
                c.start()

        for c in copies(be_ref[m], p):
            c.wait()
        cast()
        is_last = last_ref[m] == 1

        @pl.when(jnp.logical_not(is_last) | (p + 1 < n_pass))
        def _():
            for c in copies(next_ref[m], jnp.where(is_last, p + 1, p)):
                c.start()


def _for_filled_row_blocks(filled, o_ref, compute):
    blocks = _row_blocks(o_ref.shape[0])
    size = blocks[0].stop
    for n in range(1, len(blocks) + 1):
        cond = filled > (n - 1) * size
        if n < len(blocks):
            cond &= filled <= n * size

        @pl.when(cond)
        def _(n=n):
            for rows in blocks[:n]:
                compute(rows)
            for rows in blocks[n:]:
                o_ref[rows, :] = jnp.zeros((size, o_ref.shape[1]), o_ref.dtype)


def _row_copy(src_hbm, row, dst, dst_row, sem):
    return pltpu.make_async_copy(src_hbm.at[pl.ds(row, 1)], dst.at[pl.ds(dst_row, 1)], sem)


def _dispatch_kernel(dest_ref, last_ref, used_ref, h_ref, o_hbm, zeros, zero_sem, sem):
    i = pl.program_id(0)
    rows = h_ref.shape[0]
    blk = zeros.shape[0]
    n_blocks = o_hbm.shape[0] // blk

    @pl.when(i == 0)
    def _():
        zeros[...] = jnp.zeros(zeros.shape, zeros.dtype)

        def clear(b):
            return pltpu.make_async_copy(zeros, o_hbm.at[pl.ds(pl.multiple_of(b * blk, blk), blk)], zero_sem)

        def each_target(act):
            def expert(e, carry):
                @pl.when(last_ref[e] >= 0)
                def _():
                    act(clear(last_ref[e]))
                return carry

            def tail(b, carry):
                act(clear(b))
                return carry

            lax.fori_loop(0, last_ref.shape[0], expert, 0)
            lax.fori_loop(used_ref[0], n_blocks, tail, 0)

        each_target(lambda c: c.start())
        each_target(lambda c: c.wait())

    def copy(t, slot):
        return pltpu.make_async_copy(h_ref.at[pl.ds(t, 1)], o_hbm.at[pl.ds(slot, 1)], sem)

    def start(t, carry):
        for k in range(TOP_K):
            copy(t, dest_ref[(i * rows + t) * TOP_K + k]).start()
        return carry

    def wait(t, carry):
        for k in range(TOP_K):
            copy(t, 0).wait()
        return carry

    lax.fori_loop(0, rows, start, 0, unroll=DMA_LOOP_UNROLL)
    lax.fori_loop(0, rows, wait, 0, unroll=DMA_LOOP_UNROLL)


def _dispatch(h2_packed, dest_flat, last_block, n_used, n_slots):
    t, w = h2_packed.shape
    rows = min(COMBINE_ROWS, t)
    return pl.pallas_call(
        _dispatch_kernel,
        grid_spec=pltpu.PrefetchScalarGridSpec(
            num_scalar_prefetch=3,
            grid=(t // rows,),
            in_specs=[pl.BlockSpec((rows, w), lambda i, *_: (i, 0))],
            out_specs=pl.BlockSpec(memory_space=pl.ANY),
            scratch_shapes=[pltpu.VMEM((EXPERT_ROWS, w), h2_packed.dtype), pltpu.SemaphoreType.DMA(()),
                            pltpu.SemaphoreType.DMA(())],
        ),
        out_shape=jax.ShapeDtypeStruct((n_slots, w), h2_packed.dtype),
        compiler_params=_params(("arbitrary",)),
        name="dispatch",
    )(dest_flat, last_block, n_used, h2_packed)


def _gate_up_kernel(n_pass, be_ref, used_ref, run_ref, next_ref, last_ref, nruns_ref, filled_ref, x_ref, w_hbm,
                    bg_ref, bu_ref, o_ref, wbuf, wg_bf, wu_bf, sem):
    m = pl.program_id(1)
    tf = wg_bf.shape[1]
    d_ff = n_pass * tf

    def copies(e, p):
        gate_cols = pl.ds(pl.multiple_of(p * tf, tf), tf)
        up_cols = pl.ds(pl.multiple_of(d_ff + p * tf, tf), tf)
        return [pltpu.make_async_copy(w_hbm.at[e, :, gate_cols], wbuf.at[0], sem),
                pltpu.make_async_copy(w_hbm.at[e, :, up_cols], wbuf.at[1], sem)]

    def cast():
        wg_bf[...] = wbuf[0].astype(BF16)
        wu_bf[...] = wbuf[1].astype(BF16)

    @pl.when(m < used_ref[0])
    def _():
        _stream_expert_weights((be_ref, used_ref, run_ref, next_ref, last_ref, nruns_ref), n_pass, copies, cast)

        def compute(rows):
            lo, hi = _unpack_bf16_pairs(x_ref[rows, :])
            half = lo.shape[1]
            gate = (jnp.dot(lo, wg_bf[:half, :], preferred_element_type=F32)
                    + jnp.dot(hi, wg_bf[half:, :], preferred_element_type=F32) + bg_ref[...])
            up = (jnp.dot(lo, wu_bf[:half, :], preferred_element_type=F32)
                  + jnp.dot(hi, wu_bf[half:, :], preferred_element_type=F32) + bu_ref[...])
            gate = jnp.minimum(gate, SWIGLU_LIMIT)
            up = jnp.clip(up, -SWIGLU_LIMIT, SWIGLU_LIMIT)
            act = (up + 1.0) * gate * (1.0 / (1.0 + jnp.exp(-SWIGLU_ALPHA * gate)))
            o_ref[rows, :] = act.astype(o_ref.dtype)

        _for_filled_row_blocks(filled_ref[m], o_ref, compute)

    @pl.when(m >= used_ref[0])
    def _():
        o_ref[...] = jnp.zeros(o_ref.shape, o_ref.dtype)


def _gate_up(xs, w_gate_up, b_gate_up, sched):
    p = xs.shape[0]
    d = w_gate_up.shape[1]
    d_ff = w_gate_up.shape[2] // 2
    rows, tf = EXPERT_ROWS, FF_TILE
    nf = d_ff // tf
    blk = lambda m, used: jnp.minimum(m, used[0] - 1)
    return pl.pallas_call(
        functools.partial(_gate_up_kernel, nf),
        grid_spec=pltpu.PrefetchScalarGridSpec(
            num_scalar_prefetch=len(sched),
            grid=(nf, p // rows),
            in_specs=[
                pl.BlockSpec((rows, xs.shape[1]), lambda f, m, be, used, *_: (blk(m, used), 0)),
                pl.BlockSpec(memory_space=pl.ANY),
                pl.BlockSpec((None, 1, tf), lambda f, m, be, used, *_: (be[blk(m, used)], 0, f)),
                pl.BlockSpec((None, 1, tf), lambda f, m, be, used, *_: (be[blk(m, used)], 0, nf + f)),
            ],
            out_specs=pl.BlockSpec((rows, tf), lambda f, m, *_: (m, f)),
            scratch_shapes=[pltpu.VMEM((2, d, tf), F32), pltpu.VMEM((d, tf), BF16), pltpu.VMEM((d, tf), BF16),
                            pltpu.SemaphoreType.DMA(())],
        ),
        out_shape=jax.ShapeDtypeStruct((p, d_ff), BF16),
        compiler_params=_params(("arbitrary", "arbitrary")),
        name="expert_gate_up",
    )(*sched, xs, w_gate_up, b_gate_up, b_gate_up)


def _down_kernel(n_pass, be_ref, used_ref, run_ref, next_ref, last_ref, nruns_ref, filled_ref, a_ref, w_hbm, b_ref,
                 o_ref, wbuf, w_bf, sem):
    m = pl.program_id(1)
    tn = w_bf.shape[1]

    def copies(e, p):
        cols = pl.ds(pl.multiple_of(p * tn, tn), tn)
        return [pltpu.make_async_copy(w_hbm.at[e, :, cols], wbuf, sem)]

    def cast():
        w_bf[...] = wbuf[...].astype(BF16)

    @pl.when(m < used_ref[0])
    def _():
        _stream_expert_weights((be_ref, used_ref, run_ref, next_ref, last_ref, nruns_ref), n_pass, copies, cast)

        def compute(rows):
            o_ref[rows, :] = jnp.dot(a_ref[rows, :], w_bf[...], preferred_element_type=F32) + b_ref[...]

        _for_filled_row_blocks(filled_ref[m], o_ref, compute)

    @pl.when(m >= used_ref[0])
    def _():
        o_ref[...] = jnp.zeros(o_ref.shape, o_ref.dtype)


def _down(act, w_down, b_down, sched):
    p, d_ff = act.shape
    d = w_down.shape[2]
    rows, tn = EXPERT_ROWS, DOWN_TN
    blk = lambda m, used: jnp.minimum(m, used[0] - 1)
    return pl.pallas_call(
        functools.partial(_down_kernel, d // tn),
        grid_spec=pltpu.PrefetchScalarGridSpec(
            num_scalar_prefetch=len(sched),
            grid=(d // tn, p // rows),
            in_specs=[
                pl.BlockSpec((rows, d_ff), lambda n, m, be, used, *_: (blk(m, used), 0)),
                pl.BlockSpec(memory_space=pl.ANY),
                pl.BlockSpec((None, 1, tn), lambda n, m, be, used, *_: (be[blk(m, used)], 0, n)),
            ],
            out_specs=pl.BlockSpec((rows, tn), lambda n, m, *_: (m, n)),
            scratch_shapes=[pltpu.VMEM((d_ff, tn), F32), pltpu.VMEM((d_ff, tn), BF16),
                            pltpu.SemaphoreType.DMA(())],
        ),
        out_shape=jax.ShapeDtypeStruct((p, d), F32),
        compiler_params=_params(("arbitrary", "arbitrary")),
        name="expert_down",
    )(*sched, act, w_down, b_down)


def _combine_kernel(dest_ref, x_ref, g_ref, y_hbm, o_ref, buf, sem):
    i = pl.program_id(0)
    rows = x_ref.shape[0]
    slot = i % 2

    def gather(step, into):
        def start(t, carry):
            for k in range(TOP_K):
                _row_copy(y_hbm, dest_ref[(step * rows + t) * TOP_K + k], buf.at[into, k], t, sem.at[into]).start()
            return carry

        lax.fori_loop(0, rows, start, 0, unroll=DMA_LOOP_UNROLL)

    @pl.when(i == 0)
    def _():
        gather(0, 0)

    @pl.when(i + 1 < pl.num_programs(0))
    def _():
        gather(i + 1, 1 - slot)

    def wait(t, carry):
        for k in range(TOP_K):
            _row_copy(y_hbm, 0, buf.at[slot, k], t, sem.at[slot]).wait()
        return carry

    lax.fori_loop(0, rows, wait, 0, unroll=DMA_LOOP_UNROLL)
    g = g_ref[...]
    cur = buf.at[slot]
    moe = ((g[:, 0:1] * cur[0] + g[:, 1:2] * cur[1]) + (g[:, 2:3] * cur[2] + g[:, 3:4] * cur[3]))
    o_ref[...] = x_ref[...] + moe


def _combine(x1, gates_padded, y, dest_flat):
    t, d = x1.shape
    rows = min(COMBINE_ROWS, t)
    return pl.pallas_call(
        _combine_kernel,
        grid_spec=pltpu.PrefetchScalarGridSpec(
            num_scalar_prefetch=1,
            grid=(t // rows,),
            in_specs=[pl.BlockSpec((rows, d), lambda i, dest: (i, 0)),
                      pl.BlockSpec((rows, LANES), lambda i, dest: (i, 0)),
                      pl.BlockSpec(memory_space=pl.ANY)],
            out_specs=pl.BlockSpec((rows, d), lambda i, dest: (i, 0)),
            scratch_shapes=[pltpu.VMEM((2, TOP_K, rows, d), F32), pltpu.SemaphoreType.DMA((2,))],
        ),
        out_shape=jax.ShapeDtypeStruct((t, d), F32),
        compiler_params=_params(("arbitrary",)),
        name="combine",
    )(dest_flat, x1, gates_padded, y)


def _moe(x1, h2, logits, w_gate_up, b_gate_up, w_down, b_down):
    t, d = x1.shape
    n_exp = w_gate_up.shape[0]
    rows = EXPERT_ROWS
    idx_p, gate_p, rank_p, cnt_p = _route(logits)
    idx, rank = idx_p[:, :TOP_K], rank_p[:, :TOP_K]
    counts = cnt_p[0, :n_exp]

    padded = (counts + rows - 1) // rows * rows
    pend = jnp.cumsum(padded)
    pstart = pend - padded
    n_blocks = (t * TOP_K + n_exp * (rows - 1) + rows - 1) // rows
    dest_flat = (pstart[idx] + rank).astype(jnp.int32).reshape(-1)
    block_start = jnp.arange(n_blocks, dtype=jnp.int32) * rows
    block_e = jnp.minimum(jnp.sum(pend[None, :] <= block_start[:, None], axis=1), n_exp - 1).astype(jnp.int32)
    n_used = (pend[-1] // rows).astype(jnp.int32).reshape(1)
    has_rows = counts > 0
    expert_run = (jnp.cumsum(has_rows.astype(jnp.int32)) - 1).astype(jnp.int32)
    n_runs = jnp.sum(has_rows.astype(jnp.int32)).reshape(1)
    run_id = expert_run[block_e]
    is_last_run = (run_id == n_runs[0] - 1).astype(jnp.int32)
    following = jnp.where(is_last_run == 1, 0, run_id + 1)
    expert_ids = jnp.arange(n_exp, dtype=jnp.int32)
    next_e = jnp.sum(jnp.where(has_rows[None, :] & (expert_run[None, :] == following[:, None]),
                               expert_ids[None, :], 0), axis=1).astype(jnp.int32)
    filled = jnp.clip(counts[block_e] - (block_start - pstart[block_e]), 0, rows).astype(jnp.int32)
    sched = (block_e, n_used, run_id, next_e, is_last_run, n_runs, filled)

    last_block = jnp.where(has_rows, pend // rows - 1, -1).astype(jnp.int32)
    xs = _dispatch(h2, dest_flat, last_block, n_used, n_blocks * rows)
    act = _gate_up(xs, w_gate_up, b_gate_up.reshape(n_exp, 1, -1), sched)
    y = _down(act, w_down, b_down.reshape(n_exp, 1, -1), sched)
    return _combine(x1, gate_p, y, dest_flat)


def _rope_tables(positions):
    half = ROT_DIM // 2
    inv_freq = ROPE_THETA ** (-jnp.arange(0, ROT_DIM, 2, dtype=F32) / ROT_DIM)
    ang = positions.reshape(-1).astype(F32)[:, None] * inv_freq
    cos, sin = jnp.cos(ang), jnp.sin(ang)
    t = ang.shape[0]
    cos_t = jnp.concatenate([cos, cos, jnp.ones((t, HEAD - ROT_DIM), F32)], axis=1)
    sa_t = jnp.concatenate([-sin, jnp.zeros((t, HEAD - half), F32)], axis=1)
    sb_t = jnp.concatenate([jnp.zeros((t, half), F32), sin, jnp.zeros((t, HEAD - ROT_DIM), F32)], axis=1)
    return cos_t, sa_t, sb_t


def _layer(x2d, mem2d, batch, seq, cos_t, sa_t, sb_t, layer_idx, g_mix_norm, w_in, b_gate, diff_q_norm,
           diff_k_norm, diff_lambda, diff_subln, dil_q_norm, dil_k_norm, g_mem_norm, w_mem_kv, mem_q_norm,
           mem_k_norm, w_branch_diff, w_branch_dil, w_branch_mem, w_out, g_ffn_norm, w_router, b_router,
           w_gate_up, b_gate_up, w_down, b_down):
    d = x2d.shape[1]
    diff_cols = DIFF_HEADS * 2 * HEAD
    dil_cols = DIL_HEADS * HEAD
    mem_cols = MEM_HEADS * MEM_HEAD_DIM
    off_dq, off_dk, off_dv = 0, diff_cols, 2 * diff_cols
    off_lq = 3 * diff_cols
    off_lk, off_lv = off_lq + dil_cols, off_lq + 2 * dil_cols
    off_mq = off_lq + 3 * dil_cols
    off_gate = off_mq + mem_cols
    n_cols = off_gate + 3 * d
    assert w_in.shape == (d, n_cols)
    tn = PROJ_TN
    tiles = lambda off, width: tuple(range(off // tn, (off + width) // tn))
    rope_tiles = (tiles(off_dq, diff_cols) + tiles(off_dk, diff_cols) + tiles(off_lq, dil_cols)
                  + tiles(off_lk, dil_cols))

    rep = lambda g, width: jnp.tile(g.astype(F32), width // g.shape[0])
    diff_q_gain = diff_q_norm.astype(F32) * (HEAD ** -0.5 * LOG2E)
    dil_q_gain = dil_q_norm.astype(F32) * (HEAD ** -0.5 * LOG2E)

    def score_bound(q_gain, k_gain):
        return (1.01 * HEAD * jnp.max(jnp.abs(q_gain)) * jnp.max(jnp.abs(k_gain.astype(F32)))).reshape(1)

    diff_bound = score_bound(diff_q_gain, diff_k_norm)
    dil_bound = score_bound(dil_q_gain, dil_k_norm)
    gain_row = jnp.concatenate([
        rep(diff_q_gain, diff_cols), rep(diff_k_norm, diff_cols), jnp.ones((diff_cols,), F32),
        rep(dil_q_gain, dil_cols), rep(dil_k_norm, dil_cols), jnp.ones((dil_cols,), F32),
        rep(mem_q_norm, mem_cols) * MEM_HEAD_DIM ** -0.5, jnp.ones((3 * d,), F32)]).reshape(1, n_cols)
    bias_row = jnp.concatenate([jnp.zeros((off_gate,), F32), b_gate.astype(F32)]).reshape(1, n_cols)

    h = _rmsnorm(x2d, g_mix_norm, BF16)
    proj = _input_projection(h, w_in, gain_row, bias_row, cos_t, sa_t, sb_t, rope_tiles, off_mq // tn,
                             off_gate // tn)

    lam_init = 0.8 - 0.6 * float(np.exp(-0.3 * layer_idx))
    o_diff = _diff_attention(proj, batch, seq, off_dq // DIFF_V_DIM, off_dk // DIFF_V_DIM, off_dv // DIFF_V_DIM,
                             diff_bound, diff_lambda.astype(F32), diff_subln, lam_init)
    o_dil = _dilated_attention(proj, batch, seq, off_lq // HEAD, off_lk // HEAD, off_lv // HEAD, dil_bound)
    kv = _mem_kv(mem2d, g_mem_norm, w_mem_kv.astype(BF16), rep(mem_k_norm, mem_cols).reshape(1, mem_cols), batch)
    o_mem = _memory_attention(proj, kv, batch, seq, off_mq // mem_cols)

    merged = _branch_merge(o_diff, o_dil, o_mem, w_branch_diff.astype(BF16), w_branch_dil.astype(BF16),
                           w_branch_mem.astype(BF16), proj, off_gate // tn, d)

    n_exp = w_router.shape[1]
    wr = jnp.zeros((d, LANES), F32).at[:, :n_exp].set(w_router)
    wr_hi, wr_lo = _split_bf16(wr)
    br_row = jnp.zeros((1, LANES), F32).at[0, :n_exp].set(b_router)
    x1, h2, logits = _output_projection(merged, w_out.astype(BF16), x2d, g_ffn_norm, wr_hi, wr_lo, br_row)
    return _moe(x1, h2, logits, w_gate_up, b_gate_up, w_down, b_down)


def kernel(x, mem, positions, g_mix_norm, w_in, b_gate, diff_q_norm, diff_k_norm, diff_lambda, diff_subln,
           dil_q_norm, dil_k_norm, g_mem_norm, w_mem_kv, mem_q_norm, mem_k_norm, w_branch_diff, w_branch_dil,
           w_branch_mem, w_out, g_ffn_norm, w_router, b_router, w_gate_up, b_gate_up, w_down, b_down):
    batch, seq, d = x.shape
    cos_t, sa_t, sb_t = _rope_tables(positions)
    x2d = x.reshape(batch * seq, d)
    mem2d = mem.reshape(-1, d)
    for l in range(g_mix_norm.shape[0]):
        x2d = _layer(x2d, mem2d, batch, seq, cos_t, sa_t, sb_t, l, g_mix_norm[l], w_in[l], b_gate[l],
                     diff_q_norm[l], diff_k_norm[l], diff_lambda[l], diff_subln[l], dil_q_norm[l],
                     dil_k_norm[l], g_mem_norm[l], w_mem_kv[l], mem_q_norm[l], mem_k_norm[l],
                     w_branch_diff[l], w_branch_dil[l], w_branch_mem[l], w_out[l], g_ffn_norm[l],
                     w_router[l], b_router[l], w_gate_up[l], b_gate_up[l], w_down[l], b_down[l])
    return x2d.reshape(batch, seq, d)
```
